```python
import math
import jax, jax.numpy as jnp
from jax import lax
import numpy as np

D_MODEL = 1024
BATCH = 4
SEQ = 4096
DEPTH = 2

ATTN_HEADS = D_MODEL // 64
ATTN_HEAD_DIM = 64
ROT_DIM = ATTN_HEAD_DIM // 4
ROPE_THETA = 500000.0
MOBA_BLOCK = 256
MOBA_TOP_K = 3
MOBA_Q_CHUNK = 32
MLSTM_HEADS = 4
MLSTM_DV = D_MODEL // MLSTM_HEADS
MLSTM_DQK = MLSTM_DV // 2
MLSTM_CHUNK = 64
MLSTM_PROJ = 2 * MLSTM_HEADS * MLSTM_DQK + 2 * MLSTM_HEADS * MLSTM_DV + 2 * MLSTM_HEADS
MOE_EXPERTS = 32
MOE_TOP_K = 4
MOE_D_FF = D_MODEL
SWIGLU_LIMIT = 7.0
SWIGLU_ALPHA = 1.702
MOE_ROW_BLOCK = 256
N_MIXERS = 2
N_ATTN = (DEPTH + 1) // 2
N_MLSTM = DEPTH // 2
DEEPNORM_ALPHA = (2 * DEPTH) ** 0.25
DEEPNORM_BETA = (8 * DEPTH) ** -0.25
LN_EPS = 1e-5

kernel_name = 'moba_mlstm_interleaved_moe_deepnorm'


def layer_norm(x, g, b):
    xf = x.astype(jnp.float32)
    mu = jnp.mean(xf, axis=-1, keepdims=True)
    var = jnp.mean(jnp.square(xf - mu), axis=-1, keepdims=True)
    y = (xf - mu) * lax.rsqrt(var + LN_EPS)
    return (y * g.astype(jnp.float32) + b.astype(jnp.float32)).astype(x.dtype)


def partial_rope(x, positions):
    inv_freq = ROPE_THETA ** (-jnp.arange(0, ROT_DIM, 2, dtype=jnp.float32) / ROT_DIM)
    ang = positions.astype(jnp.float32)[:, None, :, None] * inv_freq
    cos, sin = jnp.cos(ang), jnp.sin(ang)
    xr = x[..., :ROT_DIM].astype(jnp.float32)
    x1, x2 = xr[..., :ROT_DIM // 2], xr[..., ROT_DIM // 2:]
    rot = jnp.concatenate([x1 * cos - x2 * sin, x2 * cos + x1 * sin], axis=-1)
    return jnp.concatenate([rot.astype(x.dtype), x[..., ROT_DIM:]], axis=-1)


def moba_attention(q, k, v):
    Bb, H, S, Dh = q.shape
    nb = S // MOBA_BLOCK
    kk = min(MOBA_TOP_K, nb)
    nq = S // MOBA_Q_CHUNK
    k_blk = k.reshape(Bb, H, nb, MOBA_BLOCK, Dh)
    v_blk = v.reshape(Bb, H, nb, MOBA_BLOCK, Dh)
    k_mean = jnp.mean(k_blk.astype(jnp.float32), axis=3)
    gate = jnp.einsum('bhsd,bhnd->bhsn', q.astype(jnp.float32), k_mean)
    q_blk = jnp.arange(S) // MOBA_BLOCK
    fully_past = jnp.arange(nb)[None, :] < q_blk[:, None]
    gate = jnp.where(fully_past, gate, -jnp.inf)
    _, sel = lax.top_k(gate, kk)
    valid = sel < q_blk[:, None]
    to_chunks = lambda a: jnp.moveaxis(a.reshape(Bb, H, nq, MOBA_Q_CHUNK, *a.shape[3:]), 2, 0)
    gather = jax.vmap(jax.vmap(lambda blocks, idx: blocks[idx]))
    scale = Dh ** -0.5

    def chunk(args):
        c, q_c, sel_c, valid_c = args
        q_start = c * MOBA_Q_CHUNK
        own = q_start // MOBA_BLOCK
        k_sel = gather(k_blk, sel_c)
        v_sel = gather(v_blk, sel_c)
        k_own = lax.dynamic_index_in_dim(k_blk, own, axis=2, keepdims=False)
        v_own = lax.dynamic_index_in_dim(v_blk, own, axis=2, keepdims=False)
        s_sel = (jnp.einsum('bhqd,bhqjkd->bhqjk', q_c, k_sel) * scale).astype(jnp.float32)
        s_sel = jnp.where(valid_c[..., None], s_sel, -jnp.inf).reshape(Bb, H, MOBA_Q_CHUNK, kk * MOBA_BLOCK)
        s_own = (jnp.einsum('bhqd,bhkd->bhqk', q_c, k_own) * scale).astype(jnp.float32)
        q_pos = q_start + jnp.arange(MOBA_Q_CHUNK)
        k_pos = own * MOBA_BLOCK + jnp.arange(MOBA_BLOCK)
        s_own = jnp.where(k_pos[None, :] <= q_pos[:, None], s_own, -jnp.inf)
        p = jax.nn.softmax(jnp.concatenate([s_sel, s_own], axis=-1), axis=-1).astype(v.dtype)
        p_sel = p[..., :kk * MOBA_BLOCK].reshape(Bb, H, MOBA_Q_CHUNK, kk, MOBA_BLOCK)
        p_own = p[..., kk * MOBA_BLOCK:]
        return (jnp.einsum('bhqjk,bhqjkd->bhqd', p_sel, v_sel)
                + jnp.einsum('bhqk,bhkd->bhqd', p_own, v_own))

    out = lax.map(chunk, (jnp.arange(nq), to_chunks(q), to_chunks(sel), to_chunks(valid)))
    return jnp.moveaxis(out, 0, 2).reshape(Bb, H, S, Dh)


def moba_layer(x, positions, w_qkv, w_o):
    Bb, S, _ = x.shape
    hd = ATTN_HEADS * ATTN_HEAD_DIM
    qkv = x @ w_qkv
    split = lambda a: a.reshape(Bb, S, ATTN_HEADS, ATTN_HEAD_DIM).transpose(0, 2, 1, 3)
    q, k, v = split(qkv[..., :hd]), split(qkv[..., hd:2 * hd]), split(qkv[..., 2 * hd:])
    q, k = partial_rope(q, positions), partial_rope(k, positions)
    s_pad = -(-S // MOBA_BLOCK) * MOBA_BLOCK
    pad = ((0, 0), (0, 0), (0, s_pad - S), (0, 0))
    o = moba_attention(jnp.pad(q, pad), jnp.pad(k, pad), jnp.pad(v, pad))[:, :, :S]
    return o.transpose(0, 2, 1, 3).reshape(Bb, S, hd) @ w_o


def mlstm_chunkwise(q, k, v, i_pre, f_pre):
    Bb, H, S, Dqk = q.shape
    Dv = v.shape[-1]
    nc = S // MLSTM_CHUNK
    f32 = jnp.float32
    to_chunks = lambda a: jnp.moveaxis(a.astype(f32).reshape(Bb, H, nc, MLSTM_CHUNK, *a.shape[3:]), 2, 0)
    log_f = jax.nn.log_sigmoid(f_pre.astype(f32))
    causal = jnp.tril(jnp.ones((MLSTM_CHUNK, MLSTM_CHUNK), dtype=bool))

    def step(carry, xs):
        C, n, m = carry
        q_c, k_c, v_c, i_c, lf_c = xs
        b = jnp.cumsum(lf_c, axis=-1)
        D = jnp.where(causal, b[..., :, None] - b[..., None, :] + i_c[..., None, :], -jnp.inf)
        g = b + m[..., None]
        m_t = jnp.maximum(g, jnp.max(D, axis=-1))
        w_inter = jnp.exp(g - m_t)
        A = jnp.exp(D - m_t[..., None]) * jnp.einsum('bhtd,bhsd->bhts', q_c, k_c)
        num = (w_inter[..., None] * jnp.einsum('bhtd,bhde->bhte', q_c, C)
               + jnp.einsum('bhts,bhse->bhte', A, v_c))
        den = w_inter * jnp.einsum('bhtd,bhd->bht', q_c, n) + jnp.sum(A, axis=-1)
        h = num / jnp.maximum(jnp.abs(den), jnp.exp(-m_t))[..., None]
        m_new = m_t[..., -1]
        decay = jnp.exp(b[..., -1] + m - m_new)
        w_s = jnp.exp(b[..., -1:] - b + i_c - m_new[..., None])
        C_new = decay[..., None, None] * C + jnp.einsum('bhs,bhsd,bhse->bhde', w_s, k_c, v_c)
        n_new = decay[..., None] * n + jnp.einsum('bhs,bhsd->bhd', w_s, k_c)
        return (C_new, n_new, m_new), h

    init = (jnp.zeros((Bb, H, Dqk, Dv), f32), jnp.zeros((Bb, H, Dqk), f32), jnp.zeros((Bb, H), f32))
    _, hs = lax.scan(step, init, (to_chunks(q), to_chunks(k), to_chunks(v), to_chunks(i_pre), to_chunks(log_f)))
    return jnp.moveaxis(hs, 0, 2).reshape(Bb, H, S, Dv)


def mlstm_layer(x, w_in, b_gates, norm_g, w_out):
    Bb, S, _ = x.shape
    qk_w = MLSTM_HEADS * MLSTM_DQK
    v_w = MLSTM_HEADS * MLSTM_DV
    proj = x @ w_in
    heads = lambda a, d: a.reshape(Bb, S, MLSTM_HEADS, d).transpose(0, 2, 1, 3)
    q = heads(proj[..., :qk_w], MLSTM_DQK)
    k = heads(proj[..., qk_w:2 * qk_w], MLSTM_DQK) * (MLSTM_DQK ** -0.5)
    v = heads(proj[..., 2 * qk_w:2 * qk_w + v_w], MLSTM_DV)
    o_pre = proj[..., 2 * qk_w + v_w:2 * qk_w + 2 * v_w]
    gates = (proj[..., 2 * qk_w + 2 * v_w:] + b_gates).transpose(0, 2, 1)
    h = mlstm_chunkwise(q, k, v, gates[:, :MLSTM_HEADS], gates[:, MLSTM_HEADS:])
    mu = jnp.mean(h, axis=-1, keepdims=True)
    var = jnp.mean(jnp.square(h - mu), axis=-1, keepdims=True)
    h = (h - mu) * lax.rsqrt(var + LN_EPS) * norm_g.astype(jnp.float32).reshape(MLSTM_HEADS, 1, MLSTM_DV)
    h = h.transpose(0, 2, 1, 3).reshape(Bb, S, v_w).astype(x.dtype)
    return (jax.nn.sigmoid(o_pre) * h) @ w_out


def clamped_swiglu(h):
    x_glu = jnp.minimum(h[..., ::2], SWIGLU_LIMIT)
    x_lin = jnp.clip(h[..., 1::2], -SWIGLU_LIMIT, SWIGLU_LIMIT)
    return x_glu * jax.nn.sigmoid(SWIGLU_ALPHA * x_glu) * (x_lin + 1.0)


def moe_ffn(x, router_w, router_b, w_gate_up, b_gate_up, w_down, b_down):
    Bb, S, D = x.shape
    xt = x.reshape(-1, D)
    T = xt.shape[0]
    n_assign = T * MOE_TOP_K
    logits = (xt @ router_w + router_b).astype(jnp.float32)
    top_logits, top_e = lax.top_k(logits, MOE_TOP_K)
    gates = jax.nn.softmax(top_logits, axis=-1)
    flat_e = top_e.reshape(-1)
    order = jnp.argsort(flat_e)
    sorted_e = flat_e[order]
    counts = jnp.bincount(flat_e, length=MOE_EXPERTS)
    padded = (counts + MOE_ROW_BLOCK - 1) // MOE_ROW_BLOCK * MOE_ROW_BLOCK
    pad_end = jnp.cumsum(padded)
    pad_start = pad_end - padded
    start = jnp.cumsum(counts) - counts
    dest_sorted = pad_start[sorted_e] + jnp.arange(n_assign, dtype=jnp.int32) - start[sorted_e]
    dest = jnp.zeros((n_assign,), jnp.int32).at[order].set(dest_sorted.astype(jnp.int32))
    n_blocks = -(-(n_assign + MOE_EXPERTS * (MOE_ROW_BLOCK - 1)) // MOE_ROW_BLOCK)
    n_rows = n_blocks * MOE_ROW_BLOCK
    row_token = jnp.zeros((n_rows,), jnp.int32).at[dest].set(jnp.arange(n_assign, dtype=jnp.int32) // MOE_TOP_K)
    block_e = jnp.minimum(jnp.searchsorted(pad_end, jnp.arange(n_blocks) * MOE_ROW_BLOCK, side='right'),
                          MOE_EXPERTS - 1)
    x_rows = xt[row_token].reshape(n_blocks, MOE_ROW_BLOCK, D)

    def expert_block(args):
        xb, e = args
        h = xb @ w_gate_up[e] + b_gate_up[e]
        return clamped_swiglu(h) @ w_down[e] + b_down[e]

    y_rows = lax.map(expert_block, (x_rows, block_e)).reshape(n_rows, D)
    y = y_rows[dest].reshape(T, MOE_TOP_K, D)
    out = jnp.einsum('tkd,tk->td', y, gates.astype(y.dtype))
    return out.reshape(Bb, S, D)


def _normal(key, shape, scale):
    return jax.random.normal(key, shape, jnp.float32) * scale


def setup_inputs(seed: int = 0) -> dict:
    key = jax.random.key(seed)
    ks = jax.random.split(key, 20)
    hd = ATTN_HEADS * ATTN_HEAD_DIM
    v_w = MLSTM_HEADS * MLSTM_DV
    x = _normal(ks[0], (BATCH, SEQ, D_MODEL), 1.0)
    positions = jnp.broadcast_to(jnp.arange(SEQ, dtype=jnp.int32), (BATCH, SEQ))
    attn_w_qkv = _normal(ks[1], (N_ATTN, D_MODEL, 3 * hd), D_MODEL ** -0.5)
    attn_w_o = _normal(ks[2], (N_ATTN, hd, D_MODEL), hd ** -0.5 * DEEPNORM_BETA)
    mlstm_w_in = _normal(ks[3], (N_MLSTM, D_MODEL, MLSTM_PROJ), D_MODEL ** -0.5)
    mlstm_b_gates = jnp.concatenate([
        _normal(ks[4], (N_MLSTM, MLSTM_HEADS), 0.1),
        3.0 + _normal(ks[5], (N_MLSTM, MLSTM_HEADS), 0.1)], axis=-1)
    mlstm_norm_g = 1.0 + _normal(ks[6], (N_MLSTM, v_w), 0.02)
    mlstm_w_out = _normal(ks[7], (N_MLSTM, v_w, D_MODEL), v_w ** -0.5 * DEEPNORM_BETA)
    ln_mix_g = 1.0 + _normal(ks[8], (DEPTH, D_MODEL), 0.02)
    ln_mix_b = _normal(ks[9], (DEPTH, D_MODEL), 0.02)
    ln_ffn_g = 1.0 + _normal(ks[10], (DEPTH, D_MODEL), 0.02)
    ln_ffn_b = _normal(ks[11], (DEPTH, D_MODEL), 0.02)
    router_w = _normal(ks[12], (DEPTH, D_MODEL, MOE_EXPERTS), D_MODEL ** -0.5)
    router_b = _normal(ks[13], (DEPTH, MOE_EXPERTS), 0.01)
    w_gate_up = _normal(ks[14], (DEPTH, MOE_EXPERTS, D_MODEL, 2 * MOE_D_FF), D_MODEL ** -0.5)
    b_gate_up = _normal(ks[15], (DEPTH, MOE_EXPERTS, 2 * MOE_D_FF), 0.01)
    w_down = _normal(ks[16], (DEPTH, MOE_EXPERTS, MOE_D_FF, D_MODEL), MOE_D_FF ** -0.5 * DEEPNORM_BETA)
    b_down = _normal(ks[17], (DEPTH, MOE_EXPERTS, D_MODEL), 0.01)
    return {'x': x, 'positions': positions,
            'attn_w_qkv': attn_w_qkv, 'attn_w_o': attn_w_o,
            'mlstm_w_in': mlstm_w_in, 'mlstm_b_gates': mlstm_b_gates,
            'mlstm_norm_g': mlstm_norm_g, 'mlstm_w_out': mlstm_w_out,
            'ln_mix_g': ln_mix_g, 'ln_mix_b': ln_mix_b, 'ln_ffn_g': ln_ffn_g, 'ln_ffn_b': ln_ffn_b,
            'router_w': router_w, 'router_b': router_b,
            'w_gate_up': w_gate_up, 'b_gate_up': b_gate_up, 'w_down': w_down, 'b_down': b_down}


def reference(x, positions, attn_w_qkv, attn_w_o, mlstm_w_in, mlstm_b_gates, mlstm_norm_g, mlstm_w_out,
              ln_mix_g, ln_mix_b, ln_ffn_g, ln_ffn_b, router_w, router_b,
              w_gate_up, b_gate_up, w_down, b_down):
    for layer in range(DEPTH):
        slot = layer // N_MIXERS
        if layer % N_MIXERS == 0:
            y = moba_layer(x, positions, attn_w_qkv[slot], attn_w_o[slot])
        else:
            y = mlstm_layer(x, mlstm_w_in[slot], mlstm_b_gates[slot], mlstm_norm_g[slot], mlstm_w_out[slot])
        x = layer_norm(DEEPNORM_ALPHA * x + y, ln_mix_g[layer], ln_mix_b[layer])
        f = moe_ffn(x, router_w[layer], router_b[layer], w_gate_up[layer], b_gate_up[layer],
                    w_down[layer], b_down[layer])
        x = layer_norm(DEEPNORM_ALPHA * x + f, ln_ffn_g[layer], ln_ffn_b[layer])
    return x
```

```python
import functools

import jax
import jax.numpy as jnp
from jax import lax
from jax.experimental import pallas as pl
from jax.experimental.pallas import tpu as pltpu

F32 = jnp.float32
BF16 = jnp.bfloat16
I32 = jnp.int32

DEPTH = 2
ATTN_HEAD_DIM = 64
ROT_DIM = ATTN_HEAD_DIM // 4
ROPE_THETA = 500000.0
MOBA_BLOCK = 256
MOBA_TOP_K = 3
MLSTM_HEADS = 4
MOE_EXPERTS = 32
MOE_TOP_K = 4
SWIGLU_LIMIT = 7.0
SWIGLU_ALPHA = 1.702
DEEPNORM_ALPHA = (2 * DEPTH) ** 0.25
LN_EPS = 1e-5

LANES = 128
SUBLANES = 8
VMEM_LIMIT_BYTES = 48 * 1024 * 1024

PROJ_TM = 512
PROJ_TN = 512
LN_TM = 512
MLSTM_L = 256
ROW_BLOCK = 512
DISPATCH_TM = 512
COMBINE_TM = 256

NT_DIMS = (((1,), (1,)), ((), ()))
TN_DIMS = (((0,), (0,)), ((), ()))
NEG_INF = float("-inf")


def _cparams(sem):
    return pltpu.CompilerParams(dimension_semantics=sem, vmem_limit_bytes=VMEM_LIMIT_BYTES)


def _qkv_rope_kernel(pos_ref, invf_ref, x_ref, w_ref, o_ref, c_ref, s1_ref, s2_ref, *,
                     n_rope_blocks, n_q_blocks, tn):
    j = pl.program_id(1)

    @pl.when(j == 0)
    def _():
        ang = pos_ref[...].astype(F32) * invf_ref[...]
        d = lax.broadcasted_iota(I32, ang.shape, 1) & (ATTN_HEAD_DIM - 1)
        c = jnp.cos(ang)
        s = jnp.sin(ang)
        c_ref[...] = jnp.where(d < ROT_DIM, c, 1.0)
        s1_ref[...] = jnp.where(d < ROT_DIM // 2, -s, 0.0)
        s2_ref[...] = jnp.where((d >= ROT_DIM // 2) & (d < ROT_DIM), s, 0.0)

    acc = jnp.dot(x_ref[...].astype(BF16), w_ref[...], preferred_element_type=F32)

    @pl.when(j < n_rope_blocks)
    def _():
        scale = jnp.where(j < n_q_blocks, ATTN_HEAD_DIM ** -0.5, 1.0).astype(F32)
        half = ROT_DIM // 2
        for c in range(tn // LANES):
            blk = acc[:, c * LANES:(c + 1) * LANES]
            r = (blk * c_ref[...]
                 + pltpu.roll(blk, LANES - half, 1) * s1_ref[...]
                 + pltpu.roll(blk, half, 1) * s2_ref[...])
            o_ref[:, c * LANES:(c + 1) * LANES] = (r * scale).astype(o_ref.dtype)

    @pl.when(j >= n_rope_blocks)
    def _():
        o_ref[...] = acc.astype(o_ref.dtype)


def _qkv_rope(x, pos, w_bf16, hd):
    T, D = x.shape
    N = w_bf16.shape[1]
    tm, tn = PROJ_TM, PROJ_TN
    inv_freq = ROPE_THETA ** (-jnp.arange(0, ROT_DIM, 2, dtype=F32) / ROT_DIM)
    lane_d = jnp.arange(LANES) % ATTN_HEAD_DIM
    invf = jnp.where(lane_d < ROT_DIM, inv_freq[lane_d % (ROT_DIM // 2)], 0.0).reshape(1, LANES).astype(F32)
    kern = functools.partial(_qkv_rope_kernel, n_rope_blocks=2 * hd // tn, n_q_blocks=hd // tn, tn=tn)
    return pl.pallas_call(
        kern,
        grid=(T // tm, N // tn),
        in_specs=[pl.BlockSpec((tm, 1), lambda i, j: (i, 0)),
                  pl.BlockSpec((1, LANES), lambda i, j: (0, 0)),
                  pl.BlockSpec((tm, D), lambda i, j: (i, 0)),
                  pl.BlockSpec((D, tn), lambda i, j: (0, j))],
        out_specs=pl.BlockSpec((tm, tn), lambda i, j: (i, j)),
        out_shape=jax.ShapeDtypeStruct((T, N), BF16),
        scratch_shapes=[pltpu.VMEM((tm, LANES), F32)] * 3,
        compiler_params=_cparams(("parallel", "arbitrary")),
        name="qkv_rope",
    )(pos, invf, x, w_bf16)


def _moba_kernel(q_ref, k_ref, v_ref, o_ref, kmh_ref, kml_ref, m_ref, l_ref, acc_ref, *, nb, blk):
    i = pl.program_id(2)
    hd = ATTN_HEAD_DIM

    @pl.when(i == 0)
    def _():
        rows = [jnp.mean(k_ref[j * blk:(j + 1) * blk, :].astype(F32), axis=0, keepdims=True)
                for j in range(nb)]
        km = jnp.concatenate(rows + [jnp.zeros((LANES - nb, LANES), F32)], axis=0)
        hi = km.astype(BF16)
        kmh_ref[...] = hi
        kml_ref[...] = (km - hi.astype(F32)).astype(BF16)

    q = q_ref[...]
    lane_q = lax.broadcasted_iota(I32, q.shape, 1)
    zero = jnp.zeros_like(q)
    q2 = jnp.concatenate([jnp.where(lane_q < hd, q, zero), jnp.where(lane_q >= hd, q, zero)], axis=0)

    gate = (lax.dot_general(q2, kmh_ref[...], NT_DIMS, preferred_element_type=F32)
            + lax.dot_general(q2, kml_ref[...], NT_DIMS, preferred_element_type=F32))
    lane = lax.broadcasted_iota(I32, gate.shape, 1)
    lane_f = lane.astype(F32)
    g = jnp.where(lane < i, gate, NEG_INF)
    sel = jnp.zeros(gate.shape, F32)
    for _ in range(MOBA_TOP_K):
        mx = jnp.max(g, axis=1, keepdims=True)
        idx = jnp.min(jnp.where(g == mx, lane_f, float(LANES)), axis=1, keepdims=True)
        hit = (lane_f == idx) & (mx > NEG_INF)
        sel = jnp.where(hit, 1.0, sel)
        g = jnp.where(hit, NEG_INF, g)

    def kv(j):
        start = pl.multiple_of(j * blk, blk)
        return k_ref[pl.ds(start, blk), :], v_ref[pl.ds(start, blk), :]

    ki, vi = kv(i)
    s = lax.dot_general(q2, ki, NT_DIMS, preferred_element_type=F32)
    row = lax.broadcasted_iota(I32, s.shape, 0) & (blk - 1)
    col = lax.broadcasted_iota(I32, s.shape, 1)
    s = jnp.where(col <= row, s, NEG_INF)
    m0 = jnp.max(s, axis=1, keepdims=True)
    p = jnp.exp(s - m0)
    m_ref[...] = m0
    l_ref[...] = jnp.sum(p, axis=1, keepdims=True)
    acc_ref[...] = jnp.dot(p.astype(BF16), vi, preferred_element_type=F32)

    def body(j, carry):
        kj, vj = kv(j)
        picked = jnp.max(jnp.where(lane == j, sel, 0.0), axis=1, keepdims=True) > 0.0
        sj = lax.dot_general(q2, kj, NT_DIMS, preferred_element_type=F32)
        sj = jnp.where(picked, sj, NEG_INF)
        m_old = m_ref[...]
        m_new = jnp.maximum(m_old, jnp.max(sj, axis=1, keepdims=True))
        a = jnp.exp(m_old - m_new)
        pj = jnp.exp(sj - m_new)
        l_ref[...] = a * l_ref[...] + jnp.sum(pj, axis=1, keepdims=True)
        acc_ref[...] = a * acc_ref[...] + jnp.dot(pj.astype(BF16), vj, preferred_element_type=F32)
        m_ref[...] = m_new
        return carry

    lax.fori_loop(0, i, body, 0)
    out = acc_ref[...] / l_ref[...]
    o_ref[...] = jnp.where(lane_q < hd, out[:blk], out[blk:]).astype(o_ref.dtype)


def _moba_attention(qkv, B, S, hd):
    T = B * S
    blk = MOBA_BLOCK
    nb = S // blk
    n_pairs = hd // LANES
    kern = functools.partial(_moba_kernel, nb=nb, blk=blk)
    return pl.pallas_call(
        kern,
        grid=(B, n_pairs, nb),
        in_specs=[pl.BlockSpec((blk, LANES), lambda b, h, i: (b * nb + i, h)),
                  pl.BlockSpec((S, LANES), lambda b, h, i: (b, n_pairs + h)),
                  pl.BlockSpec((S, LANES), lambda b, h, i: (b, 2 * n_pairs + h))],
        out_specs=pl.BlockSpec((blk, LANES), lambda b, h, i: (b * nb + i, h)),
        out_shape=jax.ShapeDtypeStruct((T, hd), BF16),
        scratch_shapes=[pltpu.VMEM((LANES, LANES), BF16), pltpu.VMEM((LANES, LANES), BF16),
                        pltpu.VMEM((2 * blk, 1), F32), pltpu.VMEM((2 * blk, 1), F32),
                        pltpu.VMEM((2 * blk, LANES), F32)],
        compiler_params=_cparams(("parallel", "parallel", "arbitrary")),
        name="moba_attention",
    )(qkv, qkv, qkv)


def _mlstm_proj_kernel(x_ref, w_ref, wg_ref, o_ref, g_ref):
    j = pl.program_id(1)
    xb = x_ref[...].astype(BF16)
    o_ref[...] = jnp.dot(xb, w_ref[...], preferred_element_type=F32).astype(o_ref.dtype)

    @pl.when(j == 0)
    def _():
        g_ref[...] = jnp.dot(xb, wg_ref[...], preferred_element_type=F32)


def _mlstm_proj(x, w_main, w_gate):
    T, D = x.shape
    N = w_main.shape[1]
    tm, tn = PROJ_TM, PROJ_TN
    return pl.pallas_call(
        _mlstm_proj_kernel,
        grid=(T // tm, N // tn),
        in_specs=[pl.BlockSpec((tm, D), lambda i, j: (i, 0)),
                  pl.BlockSpec((D, tn), lambda i, j: (0, j)),
                  pl.BlockSpec((D, LANES), lambda i, j: (0, 0))],
        out_specs=[pl.BlockSpec((tm, tn), lambda i, j: (i, j)),
                   pl.BlockSpec((tm, LANES), lambda i, j: (i, 0))],
        out_shape=[jax.ShapeDtypeStruct((T, N), BF16), jax.ShapeDtypeStruct((T, LANES), F32)],
        compiler_params=_cparams(("parallel", "arbitrary")),
        name="mlstm_proj",
    )(x, w_main, w_gate)


def _log_sigmoid(x):
    return jnp.minimum(x, 0.0) - jnp.log(1.0 + jnp.exp(-jnp.abs(x)))


def _mlstm_kernel(q_ref, k_ref, v_ref, o_ref, gc_ref, gr_ref, bc_ref, br_ref, ng_ref, out_ref,
                  C_ref, n_ref, m_ref, *, L, NH, DQK, DV):
    c = pl.program_id(1)

    @pl.when(c == 0)
    def _():
        C_ref[...] = jnp.zeros_like(C_ref)
        n_ref[...] = jnp.zeros_like(n_ref)
        m_ref[...] = jnp.zeros_like(m_ref)

    gcol = gc_ref[...] + bc_ref[...]
    grow = gr_ref[...] + br_ref[...]
    lane = lax.broadcasted_iota(I32, gcol.shape, 1)
    lf_col = jnp.where((lane >= NH) & (lane < 2 * NH), _log_sigmoid(gcol), 0.0)
    lf_row = _log_sigmoid(grow)
    row = lax.broadcasted_iota(I32, (L, L), 0)
    col = lax.broadcasted_iota(I32, (L, L), 1)
    causal = col <= row
    tri = causal.astype(F32)
    hp = lax.Precision.HIGHEST
    b_cols = jnp.dot(tri, lf_col, preferred_element_type=F32, precision=hp)
    b_rows = lax.dot_general(lf_row, tri, NT_DIMS, preferred_element_type=F32, precision=hp)

    for h in range(NH):
        i_row = grow[h:h + 1, :]
        i_col = gcol[:, h:h + 1]
        b_row = b_rows[NH + h:NH + h + 1, :]
        b_col = b_cols[:, NH + h:NH + h + 1]
        m_prev = m_ref[h:h + 1, 0:1]
        qh = q_ref[:, h * DQK:(h + 1) * DQK]
        kh = k_ref[:, h * DQK:(h + 1) * DQK]
        vh = v_ref[:, h * DV:(h + 1) * DV]

        D = jnp.where(causal, b_col - b_row + i_row, NEG_INF)
        g = b_col + m_prev
        m_t = jnp.maximum(g, jnp.max(D, axis=1, keepdims=True))
        w_inter = jnp.exp(g - m_t)
        qk = lax.dot_general(qh, kh, NT_DIMS, preferred_element_type=F32)
        A = jnp.exp(D - m_t) * qk
        num = (w_inter * jnp.dot(qh, C_ref[h].astype(BF16), preferred_element_type=F32)
               + jnp.dot(A.astype(BF16), vh, preferred_element_type=F32))
        qn = jnp.sum(qh.astype(F32) * n_ref[h:h + 1, :], axis=1, keepdims=True)
        den = w_inter * qn + jnp.sum(A, axis=1, keepdims=True)
        hh = num / jnp.maximum(jnp.abs(den), jnp.exp(-m_t))

        m_new = m_t[L - 1:L, :]
        b_last = b_col[L - 1:L, :]
        decay = jnp.exp(b_last + m_prev - m_new)
        w_s = jnp.exp(b_last - b_col + i_col - m_new)
        kw = kh.astype(F32) * w_s
        C_ref[h] = decay * C_ref[h] + lax.dot_general(kw.astype(BF16), vh, TN_DIMS,
                                                      preferred_element_type=F32)
        n_ref[h:h + 1, :] = decay * n_ref[h:h + 1, :] + jnp.sum(kw, axis=0, keepdims=True)
        m_ref[h:h + 1, :] = jnp.broadcast_to(m_new, (1, LANES))

        mu = jnp.mean(hh, axis=1, keepdims=True)
        hc = hh - mu
        var = jnp.mean(hc * hc, axis=1, keepdims=True)
        hn = hc * lax.rsqrt(var + LN_EPS) * ng_ref[:, h * DV:(h + 1) * DV]
        og = jax.nn.sigmoid(o_ref[:, h * DV:(h + 1) * DV].astype(F32))
        out_ref[:, h * DV:(h + 1) * DV] = (og * hn).astype(out_ref.dtype)


def _mlstm_mix(proj, gates, gates_t, b_col, b_row, norm_g, B, S):
    T = B * S
    NH = MLSTM_HEADS
    DV = norm_g.shape[1] // NH
    DQK = DV // 2
    L = MLSTM_L
    nc = S // L
    qw = NH * DQK
    vw = NH * DV
    kern = functools.partial(_mlstm_kernel, L=L, NH=NH, DQK=DQK, DV=DV)
    return pl.pallas_call(
        kern,
        grid=(B, nc),
        in_specs=[pl.BlockSpec((L, qw), lambda b, c: (b * nc + c, 0)),
                  pl.BlockSpec((L, qw), lambda b, c: (b * nc + c, 1)),
                  pl.BlockSpec((L, vw), lambda b, c: (b * nc + c, 2 * qw // vw)),
                  pl.BlockSpec((L, vw), lambda b, c: (b * nc + c, 2 * qw // vw + 1)),
                  pl.BlockSpec((L, LANES), lambda b, c: (b * nc + c, 0)),
                  pl.BlockSpec((SUBLANES, L), lambda b, c: (0, b * nc + c)),
                  pl.BlockSpec((1, LANES), lambda b, c: (0, 0)),
                  pl.BlockSpec((SUBLANES, 1), lambda b, c: (0, 0)),
                  pl.BlockSpec((1, vw), lambda b, c: (0, 0))],
        out_specs=pl.BlockSpec((L, vw), lambda b, c: (b * nc + c, 0)),
        out_shape=jax.ShapeDtypeStruct((T, vw), BF16),
        scratch_shapes=[pltpu.VMEM((NH, DQK, DV), F32), pltpu.VMEM((SUBLANES, DQK), F32),
                        pltpu.VMEM((SUBLANES, LANES), F32)],
        compiler_params=_cparams(("parallel", "arbitrary")),
        name="mlstm_mix",
    )(proj, proj, proj, proj, gates, gates_t, b_col, b_row, norm_g)


def _layer_norm(z, g, b):
    mu = jnp.mean(z, axis=-1, keepdims=True)
    zc = z - mu
    var = jnp.mean(zc * zc, axis=-1, keepdims=True)
    return zc * lax.rsqrt(var + LN_EPS) * g + b


def _mix_out_router_kernel(y_ref, w_ref, x_ref, g_ref, b_ref, rwt_ref, rb_ref,
                           x1_ref, et_ref, gt_ref, pt_ref, cnt_ref, carry_ref, *, tm, E, K):
    i = pl.program_id(0)

    @pl.when(i == 0)
    def _():
        carry_ref[...] = jnp.zeros_like(carry_ref)

    y = jnp.dot(y_ref[...], w_ref[...], preferred_element_type=F32)
    x1 = _layer_norm(DEEPNORM_ALPHA * x_ref[...] + y, g_ref[...], b_ref[...])
    x1_ref[...] = x1

    logits = lax.dot_general(rwt_ref[...], x1, NT_DIMS, preferred_element_type=F32,
                             precision=lax.Precision.HIGHEST) + rb_ref[...]
    sub = lax.broadcasted_iota(I32, logits.shape, 0).astype(F32)
    vals, hits = [], []
    cur = logits
    for _ in range(K):
        mx = jnp.max(cur, axis=0, keepdims=True)
        idx = jnp.min(jnp.where(cur == mx, sub, float(E)), axis=0, keepdims=True)
        hit = sub == idx
        vals.append(mx)
        hits.append(hit)
        cur = jnp.where(hit, NEG_INF, cur)

    chosen = jnp.zeros(logits.shape, F32)
    for hit in hits:
        chosen = jnp.where(hit, 1.0, chosen)
    r_ = lax.broadcasted_iota(I32, (tm, tm), 0)
    c_ = lax.broadcasted_iota(I32, (tm, tm), 1)
    before = (r_ < c_).astype(BF16)
    pos = jnp.dot(chosen.astype(BF16), before, preferred_element_type=F32) + carry_ref[:, 0:1]
    carry_ref[...] = carry_ref[...] + jnp.sum(chosen, axis=1, keepdims=True)
    cnt_ref[...] = carry_ref[...]

    ex = [jnp.exp(v - vals[0]) for v in vals]
    den = ex[0]
    for e_ in ex[1:]:
        den = den + e_
    for r in range(K):
        gt_ref[r:r + 1, :] = ex[r] / den
        et_ref[r:r + 1, :] = jnp.sum(jnp.where(hits[r], sub, 0.0), axis=0, keepdims=True).astype(I32)
        pt_ref[r:r + 1, :] = jnp.sum(jnp.where(hits[r], pos, 0.0), axis=0, keepdims=True).astype(I32)


def _mix_out_router(y, w_bf16, x, ln_g, ln_b, router_w, router_b):
    T, D = x.shape
    Kin = y.shape[1]
    E, K = MOE_EXPERTS, MOE_TOP_K
    tm = LN_TM
    kern = functools.partial(_mix_out_router_kernel, tm=tm, E=E, K=K)
    row = lambda i: (i, 0)
    fixed = lambda i: (0, 0)
    tok = lambda i: (0, i)
    return pl.pallas_call(
        kern,
        grid=(T // tm,),
        in_specs=[pl.BlockSpec((tm, Kin), row), pl.BlockSpec((Kin, D), fixed),
                  pl.BlockSpec((tm, D), row), pl.BlockSpec((1, D), fixed), pl.BlockSpec((1, D), fixed),
                  pl.BlockSpec((E, D), fixed), pl.BlockSpec((E, 1), fixed)],
        out_specs=[pl.BlockSpec((tm, D), row), pl.BlockSpec((K, tm), tok), pl.BlockSpec((K, tm), tok),
                   pl.BlockSpec((K, tm), tok), pl.BlockSpec((E, LANES), fixed)],
        out_shape=[jax.ShapeDtypeStruct((T, D), F32), jax.ShapeDtypeStruct((K, T), I32),
                   jax.ShapeDtypeStruct((K, T), F32), jax.ShapeDtypeStruct((K, T), I32),
                   jax.ShapeDtypeStruct((E, LANES), F32)],
        scratch_shapes=[pltpu.VMEM((E, LANES), F32)],
        compiler_params=_cparams(("arbitrary",)),
        name="mix_out_router",
    )(y, w_bf16, x, ln_g.reshape(1, D), ln_b.reshape(1, D), router_w.T, router_b.reshape(E, 1))


def _dispatch_kernel(zs_ref, zv_ref, dest_ref, x_ref, xr_ref, zbuf, sem, zsem, *, tm, K, E, RB):
    i = pl.program_id(0)

    def zero_copy(e):
        start = pl.multiple_of(zs_ref[e], RB)
        return pltpu.make_async_copy(zbuf, xr_ref.at[pl.ds(start, RB), :], zsem)

    @pl.when(i == 0)
    def _():
        zbuf[...] = jnp.zeros_like(zbuf)
        for e in range(E):
            @pl.when(zv_ref[e] > 0)
            def _():
                zero_copy(e).start()
        for e in range(E):
            @pl.when(zv_ref[e] > 0)
            def _():
                zero_copy(e).wait()

    def row_copy(t, r):
        return pltpu.make_async_copy(x_ref.at[pl.ds(t, 1), :], xr_ref.at[pl.ds(dest_ref[r, t], 1), :], sem)

    def issue(t, carry):
        for r in range(K):
            row_copy(t, r).start()
        return carry

    def drain(t, carry):
        for r in range(K):
            row_copy(t, r).wait()
        return carry

    lax.fori_loop(0, tm, issue, 0)
    lax.fori_loop(0, tm, drain, 0)


def _dispatch(x1, dest_t, zero_start, zero_valid, n_rows):
    T, D = x1.shape
    K, E, RB = MOE_TOP_K, MOE_EXPERTS, ROW_BLOCK
    tm = DISPATCH_TM
    kern = functools.partial(_dispatch_kernel, tm=tm, K=K, E=E, RB=RB)
    grid_spec = pltpu.PrefetchScalarGridSpec(
        num_scalar_prefetch=2,
        grid=(T // tm,),
        in_specs=[pl.BlockSpec((K, tm), lambda i, zs, zv: (0, i), memory_space=pltpu.SMEM),
                  pl.BlockSpec((tm, D), lambda i, zs, zv: (i, 0))],
        out_specs=pl.BlockSpec(memory_space=pl.ANY),
        scratch_shapes=[pltpu.VMEM((RB, D), F32), pltpu.SemaphoreType.DMA, pltpu.SemaphoreType.DMA],
    )
    return pl.pallas_call(
        kern,
        grid_spec=grid_spec,
        out_shape=jax.ShapeDtypeStruct((n_rows, D), F32),
        compiler_params=pltpu.CompilerParams(dimension_semantics=("arbitrary",),
                                             vmem_limit_bytes=VMEM_LIMIT_BYTES,
                                             has_side_effects=True),
        name="moe_dispatch",
    )(zero_start, zero_valid, dest_t, x1)


def _expert_kernel(be_ref, nv_ref, x_ref, wg_ref, wl_ref, bg_ref, bl_ref, wd_ref, bd_ref, y_ref):
    p = pl.program_id(0)

    @pl.when(p < nv_ref[0])
    def _():
        x = x_ref[...].astype(BF16)
        hg = jnp.dot(x, wg_ref[...], preferred_element_type=F32) + bg_ref[...]
        hl = jnp.dot(x, wl_ref[...], preferred_element_type=F32) + bl_ref[...]
        xg = jnp.minimum(hg, SWIGLU_LIMIT)
        xl = jnp.clip(hl, -SWIGLU_LIMIT, SWIGLU_LIMIT)
        act = xg * jax.nn.sigmoid(SWIGLU_ALPHA * xg) * (xl + 1.0)
        y_ref[...] = jnp.dot(act.astype(BF16), wd_ref[...], preferred_element_type=F32) + bd_ref[...]


def _experts(x_rows, blk_e, n_valid, w_glu, w_lin, b_glu, b_lin, w_down, b_down):
    n_rows, D = x_rows.shape
    E, _, F = w_glu.shape
    RB = ROW_BLOCK
    n_blocks = n_rows // RB
    rowblk = lambda p, be, nv: (jnp.minimum(p, nv[0] - 1), 0)
    wsel = lambda p, be, nv: (be[p], 0, 0)
    grid_spec = pltpu.PrefetchScalarGridSpec(
        num_scalar_prefetch=2,
        grid=(n_blocks,),
        in_specs=[pl.BlockSpec((RB, D), rowblk),
                  pl.BlockSpec((None, D, F), wsel), pl.BlockSpec((None, D, F), wsel),
                  pl.BlockSpec((None, 1, F), wsel), pl.BlockSpec((None, 1, F), wsel),
                  pl.BlockSpec((None, F, D), wsel), pl.BlockSpec((None, 1, D), wsel)],
        out_specs=pl.BlockSpec((RB, D), rowblk),
    )
    return pl.pallas_call(
        _expert_kernel,
        grid_spec=grid_spec,
        out_shape=jax.ShapeDtypeStruct((n_rows, D), F32),
        compiler_params=_cparams(("arbitrary",)),
        name="moe_experts",
    )(blk_e, n_valid, x_rows, w_glu, w_lin, b_glu, b_lin, w_down, b_down)


def _combine_kernel(dcur_ref, dnxt_ref, y_ref, x_ref, gate_ref, g_ref, b_ref, o_ref, buf, sem, *, tm, K, n_tiles):
    i = pl.program_id(0)
    slot = i % 2

    def row_copy(dref, s, t, r):
        return pltpu.make_async_copy(y_ref.at[pl.ds(dref[r, t], 1), :], buf.at[s, r, pl.ds(t, 1), :], sem.at[s])

    def issue(dref, s):
        def body(t, carry):
            for r in range(K):
                row_copy(dref, s, t, r).start()
            return carry
        lax.fori_loop(0, tm, body, 0)

    @pl.when(i == 0)
    def _():
        issue(dcur_ref, 0)

    @pl.when(i + 1 < n_tiles)
    def _():
        issue(dnxt_ref, 1 - slot)

    def drain(t, carry):
        for r in range(K):
            row_copy(dcur_ref, slot, t, r).wait()
        return carry
    lax.fori_loop(0, tm, drain, 0)

    f = gate_ref[:, 0:1] * buf[slot, 0]
    for r in range(1, K):
        f = f + gate_ref[:, r:r + 1] * buf[slot, r]
    o_ref[...] = _layer_norm(DEEPNORM_ALPHA * x_ref[...] + f, g_ref[...], b_ref[...])


def _combine(y_rows, dest_t, x1, gates, ln_g, ln_b):
    T, D = x1.shape
    K = MOE_TOP_K
    tm = COMBINE_TM
    n_tiles = T // tm
    kern = functools.partial(_combine_kernel, tm=tm, K=K, n_tiles=n_tiles)
    return pl.pallas_call(
        kern,
        grid=(n_tiles,),
        in_specs=[pl.BlockSpec((K, tm), lambda i: (0, i), memory_space=pltpu.SMEM),
                  pl.BlockSpec((K, tm), lambda i: (0, jnp.minimum(i + 1, n_tiles - 1)), memory_space=pltpu.SMEM),
                  pl.BlockSpec(memory_space=pl.ANY),
                  pl.BlockSpec((tm, D), lambda i: (i, 0)),
                  pl.BlockSpec((tm, K), lambda i: (i, 0)),
                  pl.BlockSpec((1, D), lambda i: (0, 0)), pl.BlockSpec((1, D), lambda i: (0, 0))],
        out_specs=pl.BlockSpec((tm, D), lambda i: (i, 0)),
        out_shape=jax.ShapeDtypeStruct((T, D), F32),
        scratch_shapes=[pltpu.VMEM((2, K, tm, D), F32), pltpu.SemaphoreType.DMA((2,))],
        compiler_params=_cparams(("arbitrary",)),
        name="moe_combine",
    )(dest_t, dest_t, y_rows, x1, gates, ln_g.reshape(1, D), ln_b.reshape(1, D))


def _moe(x1, e_t, g_t, p_t, cnt, w_gate_up, b_gate_up, w_down, b_down, ln_g, ln_b):
    T, D = x1.shape
    E, K, RB = MOE_EXPERTS, MOE_TOP_K, ROW_BLOCK
    n_blocks = -(-(T * K + E * (RB - 1)) // RB)
    n_rows = n_blocks * RB

    counts = cnt[:, 0].astype(I32)
    padded = (counts + RB - 1) // RB * RB
    pad_end = jnp.cumsum(padded)
    pad_start = pad_end - padded
    dest_t = (jnp.take(pad_start, e_t) + p_t).astype(I32)
    n_valid = (pad_end[-1] // RB).astype(I32)
    blk_ids = jnp.minimum(jnp.arange(n_blocks, dtype=I32), n_valid - 1)
    blk_e = jnp.minimum(jnp.searchsorted(pad_end, blk_ids * RB, side="right"), E - 1).astype(I32)
    zero_start = jnp.maximum(pad_end - RB, 0).astype(I32)
    zero_valid = (counts > 0).astype(I32)

    w_glu = w_gate_up[:, :, 0::2].astype(BF16)
    w_lin = w_gate_up[:, :, 1::2].astype(BF16)
    b_glu = b_gate_up[:, None, 0::2]
    b_lin = b_gate_up[:, None, 1::2]

    x_rows = _dispatch(x1, dest_t, zero_start, zero_valid, n_rows)
    y_rows = _experts(x_rows, blk_e, n_valid.reshape(1), w_glu, w_lin, b_glu, b_lin,
                      w_down.astype(BF16), b_down[:, None, :])
    return _combine(y_rows, dest_t, x1, g_t.T, ln_g, ln_b)


def kernel(x, positions, attn_w_qkv, attn_w_o, mlstm_w_in, mlstm_b_gates, mlstm_norm_g, mlstm_w_out,
           ln_mix_g, ln_mix_b, ln_ffn_g, ln_ffn_b, router_w, router_b,
           w_gate_up, b_gate_up, w_down, b_down):
    B, S, D = x.shape
    T = B * S
    xt = x.reshape(T, D)
    NH = MLSTM_HEADS

    for layer in range(DEPTH):
        slot = layer // 2
        if layer % 2 == 0:
            hd = attn_w_o.shape[1]
            qkv = _qkv_rope(xt, positions.reshape(T, 1), attn_w_qkv[slot].astype(BF16), hd)
            mixed = _moba_attention(qkv, B, S, hd)
            w_out = attn_w_o[slot].astype(BF16)
        else:
            vw = mlstm_w_out.shape[1]
            qw = vw // 2
            w_in = mlstm_w_in[slot]
            dqk = qw // NH
            col_scale = jnp.concatenate([jnp.ones((qw,), F32), jnp.full((qw,), dqk ** -0.5, F32),
                                         jnp.ones((2 * vw,), F32)])
            w_main = (w_in[:, :2 * qw + 2 * vw] * col_scale).astype(BF16)
            w_gate = jnp.pad(w_in[:, 2 * qw + 2 * vw:], ((0, 0), (0, LANES - 2 * NH))).astype(BF16)
            proj, gates = _mlstm_proj(xt, w_main, w_gate)
            bg = mlstm_b_gates[slot]
            b_col = jnp.pad(bg, (0, LANES - 2 * NH)).reshape(1, LANES)
            b_row = bg.reshape(2 * NH, 1)
            mixed = _mlstm_mix(proj, gates, gates[:, :2 * NH].T, b_col, b_row,
                               mlstm_norm_g[slot].reshape(1, vw), B, S)
            w_out = mlstm_w_out[slot].astype(BF16)
        x1, e_t, g_t, p_t, cnt = _mix_out_router(mixed, w_out, xt, ln_mix_g[layer], ln_mix_b[layer],
                                                 router_w[layer], router_b[layer])
        xt = _moe(x1, e_t, g_t, p_t, cnt, w_gate_up[layer], b_gate_up[layer], w_down[layer], b_down[layer],
                  ln_ffn_g[layer], ln_ffn_b[layer])
    return xt.reshape(B, S, D)
```

```python
import functools

import jax
import jax.numpy as jnp
from jax import lax
from jax.experimental import pallas as pl
from jax.experimental.pallas import tpu as pltpu

F32 = jnp.float32
BF16 = jnp.bfloat16
I32 = jnp.int32

DEPTH = 2
ATTN_HEAD_DIM = 64
ROT_DIM = ATTN_HEAD_DIM // 4
ROPE_THETA = 500000.0
MOBA_BLOCK = 256
MOBA_TOP_K = 3
MLSTM_HEADS = 4
MOE_EXPERTS = 32
MOE_TOP_K = 4
SWIGLU_LIMIT = 7.0
SWIGLU_ALPHA = 1.702
DEEPNORM_ALPHA = (2 * DEPTH) ** 0.25
LN_EPS = 1e-5

LANES = 128
SUBLANES = 8
BF16_SUBLANES = 16
MXU_DIM = 256
VMEM_LIMIT_BYTES = 48 * 1024 * 1024
EXPERT_VMEM_LIMIT_BYTES = 58 * 1024 * 1024

PROJ_TM = 512
PROJ_TN = 512
LN_TM = 512
MLSTM_L = 256
ROW_BLOCK = 512
DISPATCH_TM = 512
COMBINE_TM = 256
PAST_UNROLL = 2

NT_DIMS = (((1,), (1,)), ((), ()))
TN_DIMS = (((0,), (0,)), ((), ()))
NEG_INF = float("-inf")


def _cparams(sem, vmem=VMEM_LIMIT_BYTES):
    return pltpu.CompilerParams(dimension_semantics=sem, vmem_limit_bytes=vmem)


def _qkv_rope_kernel(pos_ref, invf_ref, x_ref, w_ref, o_ref, *, n_rope_chunks, n_q_chunks, tn):
    ang = pos_ref[...].astype(F32) * invf_ref[...]
    d = lax.broadcasted_iota(I32, ang.shape, 1) & (ATTN_HEAD_DIM - 1)
    cos = jnp.cos(ang)
    sin = jnp.sin(ang)
    half = ROT_DIM // 2
    c_tab = jnp.where(d < ROT_DIM, cos, 1.0)
    s_up = jnp.where(d < half, -sin, 0.0)
    s_dn = jnp.where((d >= half) & (d < ROT_DIM), sin, 0.0)

    xb = x_ref[...].astype(BF16)
    for c in range(w_ref.shape[1] // tn):
        acc = jnp.dot(xb, w_ref[:, c * tn:(c + 1) * tn], preferred_element_type=F32)
        if c >= n_rope_chunks:
            o_ref[:, c * tn:(c + 1) * tn] = acc.astype(o_ref.dtype)
            continue
        scale = ATTN_HEAD_DIM ** -0.5 if c < n_q_chunks else 1.0
        for s in range(tn // LANES):
            blk = acc[:, s * LANES:(s + 1) * LANES]
            r = (blk * c_tab + pltpu.roll(blk, LANES - half, 1) * s_up + pltpu.roll(blk, half, 1) * s_dn)
            lo = c * tn + s * LANES
            o_ref[:, lo:lo + LANES] = (r * scale).astype(o_ref.dtype)


def _qkv_rope(x, pos, w_bf16, hd):
    T, D = x.shape
    N = w_bf16.shape[1]
    tm, tn = PROJ_TM, PROJ_TN
    inv_freq = ROPE_THETA ** (-jnp.arange(0, ROT_DIM, 2, dtype=F32) / ROT_DIM)
    lane_d = jnp.arange(LANES) % ATTN_HEAD_DIM
    invf = jnp.where(lane_d < ROT_DIM, inv_freq[lane_d % (ROT_DIM // 2)], 0.0).reshape(1, LANES).astype(F32)
    kern = functools.partial(_qkv_rope_kernel, n_rope_chunks=2 * hd // tn, n_q_chunks=hd // tn, tn=tn)
    return pl.pallas_call(
        kern,
        grid=(T // tm,),
        in_specs=[pl.BlockSpec((tm, 1), lambda i: (i, 0)),
                  pl.BlockSpec((1, LANES), lambda i: (0, 0)),
                  pl.BlockSpec((tm, D), lambda i: (i, 0)),
                  pl.BlockSpec((D, N), lambda i: (0, 0))],
        out_specs=pl.BlockSpec((tm, N), lambda i: (i, 0)),
        out_shape=jax.ShapeDtypeStruct((T, N), BF16),
        compiler_params=_cparams(("parallel",)),
        name="qkv_rope",
    )(pos, invf, x, w_bf16)


def _moba_kernel(q_ref, k_ref, v_ref, o_ref, kmh_ref, kml_ref, vt_ref, q2t_ref, sel_ref,
                 m_ref, l_ref, acc_ref, *, nb, blk):
    i = pl.program_id(2)
    hd = ATTN_HEAD_DIM
    nbp = kmh_ref.shape[0]

    @pl.when(i == 0)
    def _():
        rows = [jnp.mean(k_ref[j * blk:(j + 1) * blk, :].astype(F32), axis=0, keepdims=True)
                for j in range(nb)]
        if nbp > nb:
            rows.append(jnp.zeros((nbp - nb, LANES), F32))
        km = jnp.concatenate(rows, axis=0)
        hi = km.astype(BF16)
        kmh_ref[...] = hi
        kml_ref[...] = (km - hi.astype(F32)).astype(BF16)
        for j in range(nb):
            vt_ref[j] = v_ref[j * blk:(j + 1) * blk, :].astype(F32).T.astype(BF16)

    qt = q_ref[...].astype(F32).T
    sub_q = lax.broadcasted_iota(I32, qt.shape, 0)
    zero = jnp.zeros_like(qt)
    q2t_ref[...] = jnp.concatenate([jnp.where(sub_q < hd, qt, zero), jnp.where(sub_q >= hd, qt, zero)],
                                   axis=1).astype(BF16)
    q2t = q2t_ref[...]

    gate = (jnp.dot(kmh_ref[...], q2t, preferred_element_type=F32)
            + jnp.dot(kml_ref[...], q2t, preferred_element_type=F32))
    sub = lax.broadcasted_iota(I32, gate.shape, 0)
    sub_f = sub.astype(F32)
    g = jnp.where(sub < i, gate, NEG_INF)
    sel = jnp.zeros(gate.shape, F32)
    for _ in range(MOBA_TOP_K):
        mx = jnp.max(g, axis=0, keepdims=True)
        idx = jnp.min(jnp.where(g == mx, sub_f, float(nbp)), axis=0, keepdims=True)
        hit = (sub_f == idx) & (mx > NEG_INF)
        sel = jnp.where(hit, 1.0, sel)
        g = jnp.where(hit, NEG_INF, g)
    sel_ref[...] = sel

    def pv(vt, pb):
        return jnp.concatenate([jnp.dot(vt[:hd], pb[:, :blk], preferred_element_type=F32),
                                jnp.dot(vt[hd:], pb[:, blk:], preferred_element_type=F32)], axis=1)

    ki = k_ref[pl.ds(pl.multiple_of(i * blk, blk), blk), :]
    s = jnp.dot(ki, q2t, preferred_element_type=F32)
    key = lax.broadcasted_iota(I32, s.shape, 0)
    qry = lax.broadcasted_iota(I32, s.shape, 1) & (blk - 1)
    s = jnp.where(key <= qry, s, NEG_INF)
    m0 = jnp.max(s, axis=0, keepdims=True)
    p = jnp.exp(s - m0)
    m_ref[...] = m0
    l_ref[...] = jnp.sum(p, axis=0, keepdims=True)
    acc_ref[...] = pv(vt_ref[i], p.astype(BF16))

    def body(t, carry):
        js, ss = [], []
        for u in range(PAST_UNROLL):
            j_raw = PAST_UNROLL * t + u
            j = jnp.minimum(j_raw, i - 1)
            kj = k_ref[pl.ds(pl.multiple_of(j * blk, blk), blk), :]
            picked = (sel_ref[pl.ds(j, 1), :] > 0.0) & (j_raw < i)
            js.append(j)
            ss.append(jnp.where(picked, jnp.dot(kj, q2t_ref[...], preferred_element_type=F32), NEG_INF))
        m_old = m_ref[...]
        m_new = m_old
        for sj in ss:
            m_new = jnp.maximum(m_new, jnp.max(sj, axis=0, keepdims=True))
        a = jnp.exp(m_old - m_new)
        l_new = a * l_ref[...]
        acc_new = a * acc_ref[...]
        for j, sj in zip(js, ss):
            pj = jnp.exp(sj - m_new)
            l_new = l_new + jnp.sum(pj, axis=0, keepdims=True)
            acc_new = acc_new + pv(vt_ref[j], pj.astype(BF16))
        l_ref[...] = l_new
        acc_ref[...] = acc_new
        m_ref[...] = m_new
        return carry

    lax.fori_loop(0, (i + PAST_UNROLL - 1) // PAST_UNROLL, body, 0)
    out_t = acc_ref[...] / l_ref[...]
    o_ref[...] = jnp.concatenate([out_t[:, :blk], out_t[:, blk:]], axis=0).T.astype(o_ref.dtype)


def _moba_attention(qkv, B, S, hd):
    T = B * S
    blk = MOBA_BLOCK
    nb = S // blk
    nbp = -(-nb // BF16_SUBLANES) * BF16_SUBLANES
    n_pairs = hd // LANES
    kern = functools.partial(_moba_kernel, nb=nb, blk=blk)
    return pl.pallas_call(
        kern,
        grid=(B, n_pairs, nb),
        in_specs=[pl.BlockSpec((blk, LANES), lambda b, h, i: (b * nb + i, h)),
                  pl.BlockSpec((S, LANES), lambda b, h, i: (b, n_pairs + h)),
                  pl.BlockSpec((S, LANES), lambda b, h, i: (b, 2 * n_pairs + h))],
        out_specs=pl.BlockSpec((blk, LANES), lambda b, h, i: (b * nb + i, h)),
        out_shape=jax.ShapeDtypeStruct((T, hd), BF16),
        scratch_shapes=[pltpu.VMEM((nbp, LANES), BF16), pltpu.VMEM((nbp, LANES), BF16),
                        pltpu.VMEM((nb, LANES, blk), BF16), pltpu.VMEM((LANES, 2 * blk), BF16),
                        pltpu.VMEM((nbp, 2 * blk), F32),
                        pltpu.VMEM((1, 2 * blk), F32), pltpu.VMEM((1, 2 * blk), F32),
                        pltpu.VMEM((ATTN_HEAD_DIM, 2 * blk), F32)],
        compiler_params=_cparams(("parallel", "parallel", "arbitrary")),
        name="moba_attention",
    )(qkv, qkv, qkv)


def _mlstm_proj_kernel(x_ref, w_ref, wg_ref, o_ref, g_ref, *, tn):
    xb = x_ref[...].astype(BF16)
    for c in range(w_ref.shape[1] // tn):
        o_ref[:, c * tn:(c + 1) * tn] = jnp.dot(xb, w_ref[:, c * tn:(c + 1) * tn],
                                                preferred_element_type=F32).astype(o_ref.dtype)
    g_ref[...] = jnp.dot(xb, wg_ref[...], preferred_element_type=F32)


def _mlstm_proj(x, w_main, w_gate):
    T, D = x.shape
    N = w_main.shape[1]
    tm = PROJ_TM
    return pl.pallas_call(
        functools.partial(_mlstm_proj_kernel, tn=PROJ_TN),
        grid=(T // tm,),
        in_specs=[pl.BlockSpec((tm, D), lambda i: (i, 0)),
                  pl.BlockSpec((D, N), lambda i: (0, 0)),
                  pl.BlockSpec((D, LANES), lambda i: (0, 0))],
        out_specs=[pl.BlockSpec((tm, N), lambda i: (i, 0)),
                   pl.BlockSpec((tm, LANES), lambda i: (i, 0))],
        out_shape=[jax.ShapeDtypeStruct((T, N), BF16), jax.ShapeDtypeStruct((T, LANES), F32)],
        compiler_params=_cparams(("parallel",)),
        name="mlstm_proj",
    )(x, w_main, w_gate)


def _log_sigmoid(x):
    return jnp.minimum(x, 0.0) - jnp.log(1.0 + jnp.exp(-jnp.abs(x)))


def _mlstm_kernel(q_ref, k_ref, v_ref, o_ref, gc_ref, gr_ref, bc_ref, br_ref, ng_ref, out_ref,
                  C_ref, n_ref, m_ref, *, L, NH, DQK, DV):
    c = pl.program_id(1)

    @pl.when(c == 0)
    def _():
        C_ref[...] = jnp.zeros_like(C_ref)
        n_ref[...] = jnp.zeros_like(n_ref)
        m_ref[...] = jnp.zeros_like(m_ref)

    gcol = gc_ref[...] + bc_ref[...]
    grow = gr_ref[...] + br_ref[...]
    lane = lax.broadcasted_iota(I32, gcol.shape, 1)
    lf_col = jnp.where((lane >= NH) & (lane < 2 * NH), _log_sigmoid(gcol), 0.0)
    lf_row = _log_sigmoid(grow)
    row = lax.broadcasted_iota(I32, (L, L), 0)
    col = lax.broadcasted_iota(I32, (L, L), 1)
    causal = col <= row
    tri = causal.astype(F32)
    hp = lax.Precision.HIGHEST
    b_cols = jnp.dot(tri, lf_col, preferred_element_type=F32, precision=hp)
    b_rows = lax.dot_general(lf_row, tri, NT_DIMS, preferred_element_type=F32, precision=hp)

    for h in range(NH):
        i_row = grow[h:h + 1, :]
        i_col = gcol[:, h:h + 1]
        b_row = b_rows[NH + h:NH + h + 1, :]
        b_col = b_cols[:, NH + h:NH + h + 1]
        m_prev = m_ref[h:h + 1, 0:1]
        qh = q_ref[:, h * DQK:(h + 1) * DQK]
        kh = k_ref[:, h * DQK:(h + 1) * DQK]
        vh = v_ref[:, h * DV:(h + 1) * DV]

        D = jnp.where(causal, b_col - b_row + i_row, NEG_INF)
        g = b_col + m_prev
        m_t = jnp.maximum(g, jnp.max(D, axis=1, keepdims=True))
        w_inter = jnp.exp(g - m_t)
        qk = lax.dot_general(qh, kh, NT_DIMS, preferred_element_type=F32)
        A = jnp.exp(D - m_t) * qk
        num = (w_inter * jnp.dot(qh, C_ref[h].astype(BF16), preferred_element_type=F32)
               + jnp.dot(A.astype(BF16), vh, preferred_element_type=F32))
        qn = jnp.sum(qh.astype(F32) * n_ref[h:h + 1, :], axis=1, keepdims=True)
        den = w_inter * qn + jnp.sum(A, axis=1, keepdims=True)
        hh = num / jnp.maximum(jnp.abs(den), jnp.exp(-m_t))

        m_new = m_t[L - 1:L, :]
        b_last = b_col[L - 1:L, :]
        decay = jnp.exp(b_last + m_prev - m_new)
        w_s = jnp.exp(b_last - b_col + i_col - m_new)
        kw = kh.astype(F32) * w_s
        C_ref[h] = decay * C_ref[h] + lax.dot_general(kw.astype(BF16), vh, TN_DIMS,
                                                      preferred_element_type=F32)
        n_ref[h:h + 1, :] = decay * n_ref[h:h + 1, :] + jnp.sum(kw, axis=0, keepdims=True)
        m_ref[h:h + 1, :] = jnp.broadcast_to(m_new, (1, LANES))

        mu = jnp.mean(hh, axis=1, keepdims=True)
        hc = hh - mu
        var = jnp.mean(hc * hc, axis=1, keepdims=True)
        hn = hc * lax.rsqrt(var + LN_EPS) * ng_ref[:, h * DV:(h + 1) * DV]
        og = jax.nn.sigmoid(o_ref[:, h * DV:(h + 1) * DV].astype(F32))
        out_ref[:, h * DV:(h + 1) * DV] = (og * hn).astype(out_ref.dtype)


def _mlstm_mix(proj, gates, gates_t, b_col, b_row, norm_g, B, S):
    T = B * S
    NH = MLSTM_HEADS
    DV = norm_g.shape[1] // NH
    DQK = DV // 2
    L = MLSTM_L
    nc = S // L
    qw = NH * DQK
    vw = NH * DV
    kern = functools.partial(_mlstm_kernel, L=L, NH=NH, DQK=DQK, DV=DV)
    return pl.pallas_call(
        kern,
        grid=(B, nc),
        in_specs=[pl.BlockSpec((L, qw), lambda b, c: (b * nc + c, 0)),
                  pl.BlockSpec((L, qw), lambda b, c: (b * nc + c, 1)),
                  pl.BlockSpec((L, vw), lambda b, c: (b * nc + c, 2 * qw // vw)),
                  pl.BlockSpec((L, vw), lambda b, c: (b * nc + c, 2 * qw // vw + 1)),
                  pl.BlockSpec((L, LANES), lambda b, c: (b * nc + c, 0)),
                  pl.BlockSpec((SUBLANES, L), lambda b, c: (0, b * nc + c)),
                  pl.BlockSpec((1, LANES), lambda b, c: (0, 0)),
                  pl.BlockSpec((SUBLANES, 1), lambda b, c: (0, 0)),
                  pl.BlockSpec((1, vw), lambda b, c: (0, 0))],
        out_specs=pl.BlockSpec((L, vw), lambda b, c: (b * nc + c, 0)),
        out_shape=jax.ShapeDtypeStruct((T, vw), BF16),
        scratch_shapes=[pltpu.VMEM((NH, DQK, DV), F32), pltpu.VMEM((SUBLANES, DQK), F32),
                        pltpu.VMEM((SUBLANES, LANES), F32)],
        compiler_params=_cparams(("parallel", "arbitrary")),
        name="mlstm_mix",
    )(proj, proj, proj, proj, gates, gates_t, b_col, b_row, norm_g)


def _layer_norm(z, g, b):
    mu = jnp.mean(z, axis=-1, keepdims=True)
    zc = z - mu
    var = jnp.mean(zc * zc, axis=-1, keepdims=True)
    return zc * lax.rsqrt(var + LN_EPS) * g + b


def _mix_out_router_kernel(y_ref, w_ref, x_ref, g_ref, b_ref, rwt_ref, rb_ref,
                           x1_ref, et_ref, gt_ref, pt_ref, cnt_ref, carry_ref, *, tm, E, K):
    i = pl.program_id(0)

    @pl.when(i == 0)
    def _():
        carry_ref[...] = jnp.zeros_like(carry_ref)

    y = jnp.dot(y_ref[...], w_ref[...], preferred_element_type=F32)
    x1 = _layer_norm(DEEPNORM_ALPHA * x_ref[...] + y, g_ref[...], b_ref[...])
    x1_ref[...] = x1

    logits = lax.dot_general(rwt_ref[...], x1, NT_DIMS, preferred_element_type=F32,
                             precision=lax.Precision.HIGHEST) + rb_ref[...]
    sub = lax.broadcasted_iota(I32, logits.shape, 0).astype(F32)
    vals, hits = [], []
    cur = logits
    for _ in range(K):
        mx = jnp.max(cur, axis=0, keepdims=True)
        idx = jnp.min(jnp.where(cur == mx, sub, float(E)), axis=0, keepdims=True)
        hit = sub == idx
        vals.append(mx)
        hits.append(hit)
        cur = jnp.where(hit, NEG_INF, cur)

    chosen = jnp.zeros(logits.shape, F32)
    for hit in hits:
        chosen = jnp.where(hit, 1.0, chosen)
    r_ = lax.broadcasted_iota(I32, (tm, tm), 0)
    c_ = lax.broadcasted_iota(I32, (tm, tm), 1)
    before = (r_ < c_).astype(BF16)
    pos = jnp.dot(chosen.astype(BF16), before, preferred_element_type=F32) + carry_ref[:, 0:1]
    carry_ref[...] = carry_ref[...] + jnp.sum(chosen, axis=1, keepdims=True)
    cnt_ref[...] = carry_ref[...]

    ex = [jnp.exp(v - vals[0]) for v in vals]
    den = ex[0]
    for e_ in ex[1:]:
        den = den + e_
    for r in range(K):
        gt_ref[r:r + 1, :] = ex[r] / den
        et_ref[r:r + 1, :] = jnp.sum(jnp.where(hits[r], sub, 0.0), axis=0, keepdims=True).astype(I32)
        pt_ref[r:r + 1, :] = jnp.sum(jnp.where(hits[r], pos, 0.0), axis=0, keepdims=True).astype(I32)


def _mix_out_router(y, w_bf16, x, ln_g, ln_b, router_w, router_b):
    T, D = x.shape
    Kin = y.shape[1]
    E, K = MOE_EXPERTS, MOE_TOP_K
    tm = LN_TM
    kern = functools.partial(_mix_out_router_kernel, tm=tm, E=E, K=K)
    row = lambda i: (i, 0)
    fixed = lambda i: (0, 0)
    tok = lambda i: (0, i)
    return pl.pallas_call(
        kern,
        grid=(T // tm,),
        in_specs=[pl.BlockSpec((tm, Kin), row), pl.BlockSpec((Kin, D), fixed),
                  pl.BlockSpec((tm, D), row), pl.BlockSpec((1, D), fixed), pl.BlockSpec((1, D), fixed),
                  pl.BlockSpec((E, D), fixed), pl.BlockSpec((E, 1), fixed)],
        out_specs=[pl.BlockSpec((tm, D), row), pl.BlockSpec((K, tm), tok), pl.BlockSpec((K, tm), tok),
                   pl.BlockSpec((K, tm), tok), pl.BlockSpec((E, LANES), fixed)],
        out_shape=[jax.ShapeDtypeStruct((T, D), F32), jax.ShapeDtypeStruct((K, T), I32),
                   jax.ShapeDtypeStruct((K, T), F32), jax.ShapeDtypeStruct((K, T), I32),
                   jax.ShapeDtypeStruct((E, LANES), F32)],
        scratch_shapes=[pltpu.VMEM((E, LANES), F32)],
        compiler_params=_cparams(("arbitrary",)),
        name="mix_out_router",
    )(y, w_bf16, x, ln_g.reshape(1, D), ln_b.reshape(1, D), router_w.T, router_b.reshape(E, 1))


def _dispatch_kernel(zs_ref, zv_ref, nv_ref, dest_ref, x_ref, xr_ref, zbuf, sem, zsem, *, tm, K, E, RB, n_blocks):
    i = pl.program_id(0)

    def zero_copy(start):
        return pltpu.make_async_copy(zbuf, xr_ref.at[pl.ds(pl.multiple_of(start, RB), RB), :], zsem)

    @pl.when(i == 0)
    def _():
        zbuf[...] = jnp.zeros_like(zbuf)
        for e in range(E):
            @pl.when(zv_ref[e] > 0)
            def _():
                zero_copy(zs_ref[e]).start()

        def tail_start(p, carry):
            zero_copy(p * RB).start()
            return carry

        def tail_wait(p, carry):
            zero_copy(p * RB).wait()
            return carry

        lax.fori_loop(nv_ref[0], n_blocks, tail_start, 0)
        for e in range(E):
            @pl.when(zv_ref[e] > 0)
            def _():
                zero_copy(zs_ref[e]).wait()
        lax.fori_loop(nv_ref[0], n_blocks, tail_wait, 0)

    def row_copy(t, r):
        return pltpu.make_async_copy(x_ref.at[pl.ds(t, 1), :], xr_ref.at[pl.ds(dest_ref[r, t], 1), :], sem)

    def issue(t, carry):
        for r in range(K):
            row_copy(t, r).start()
        return carry

    def drain(t, carry):
        for r in range(K):
            row_copy(t, r).wait()
        return carry

    lax.fori_loop(0, tm, issue, 0)
    lax.fori_loop(0, tm, drain, 0)


def _dispatch(x1, dest_t, zero_start, zero_valid, n_valid, n_rows):
    T, D = x1.shape
    K, E, RB = MOE_TOP_K, MOE_EXPERTS, ROW_BLOCK
    tm = DISPATCH_TM
    kern = functools.partial(_dispatch_kernel, tm=tm, K=K, E=E, RB=RB, n_blocks=n_rows // RB)
    grid_spec = pltpu.PrefetchScalarGridSpec(
        num_scalar_prefetch=3,
        grid=(T // tm,),
        in_specs=[pl.BlockSpec((K, tm), lambda i, zs, zv, nv: (0, i), memory_space=pltpu.SMEM),
                  pl.BlockSpec((tm, D), lambda i, zs, zv, nv: (i, 0))],
        out_specs=pl.BlockSpec(memory_space=pl.ANY),
        scratch_shapes=[pltpu.VMEM((RB, D), F32), pltpu.SemaphoreType.DMA, pltpu.SemaphoreType.DMA],
    )
    return pl.pallas_call(
        kern,
        grid_spec=grid_spec,
        out_shape=jax.ShapeDtypeStruct((n_rows, D), F32),
        compiler_params=pltpu.CompilerParams(dimension_semantics=("arbitrary",),
                                             vmem_limit_bytes=VMEM_LIMIT_BYTES,
                                             has_side_effects=True),
        name="moe_dispatch",
    )(zero_start, zero_valid, n_valid, dest_t, x1)


def _expert_kernel(be_ref, nv_ref, x_ref, wgu_ref, bg_ref, bl_ref, wd_ref, bd_ref, y_ref,
                   wg_s, wl_s, wd_s):
    p = pl.program_id(0)
    valid = p < nv_ref[0]
    fresh = (p == 0) | (be_ref[p] != be_ref[jnp.maximum(p - 1, 0)])
    half = MXU_DIM // 2

    @pl.when(valid & fresh)
    def _():
        r_ = lax.broadcasted_iota(I32, (MXU_DIM, MXU_DIM), 0)
        c_ = lax.broadcasted_iota(I32, (MXU_DIM, MXU_DIM), 1)
        src = jnp.where(c_ < half, 2 * c_, 2 * (c_ - half) + 1)
        perm = jnp.where(r_ == src, 1.0, 0.0).astype(BF16)
        for cb in range(wgu_ref.shape[1] // MXU_DIM):
            blk = wgu_ref[:, cb * MXU_DIM:(cb + 1) * MXU_DIM].astype(BF16)
            sep = jnp.dot(blk, perm, preferred_element_type=F32).astype(BF16)
            wg_s[:, cb * half:(cb + 1) * half] = sep[:, :half]
            wl_s[:, cb * half:(cb + 1) * half] = sep[:, half:]
        wd_s[...] = wd_ref[...].astype(BF16)

    @pl.when(valid)
    def _():
        x = x_ref[...].astype(BF16)
        hg = jnp.dot(x, wg_s[...], preferred_element_type=F32) + bg_ref[...]
        hl = jnp.dot(x, wl_s[...], preferred_element_type=F32) + bl_ref[...]
        xg = jnp.minimum(hg, SWIGLU_LIMIT)
        xl = jnp.clip(hl, -SWIGLU_LIMIT, SWIGLU_LIMIT)
        act = xg * jax.nn.sigmoid(SWIGLU_ALPHA * xg) * (xl + 1.0)
        y_ref[...] = jnp.dot(act.astype(BF16), wd_s[...], preferred_element_type=F32) + bd_ref[...]

    @pl.when(jnp.logical_not(valid))
    def _():
        y_ref[...] = jnp.zeros_like(y_ref)


def _experts(x_rows, blk_e, n_valid, w_gate_up, b_glu, b_lin, w_down, b_down):
    n_rows, D = x_rows.shape
    E, _, F2 = w_gate_up.shape
    F = F2 // 2
    RB = ROW_BLOCK
    n_blocks = n_rows // RB
    xblk = lambda p, be, nv: (jnp.minimum(p, nv[0] - 1), 0)
    yblk = lambda p, be, nv: (p, 0)
    wsel = lambda p, be, nv: (be[p], 0, 0)
    grid_spec = pltpu.PrefetchScalarGridSpec(
        num_scalar_prefetch=2,
        grid=(n_blocks,),
        in_specs=[pl.BlockSpec((RB, D), xblk),
                  pl.BlockSpec((None, D, F2), wsel),
                  pl.BlockSpec((None, 1, F), wsel), pl.BlockSpec((None, 1, F), wsel),
                  pl.BlockSpec((None, F, D), wsel), pl.BlockSpec((None, 1, D), wsel)],
        out_specs=pl.BlockSpec((RB, D), yblk),
        scratch_shapes=[pltpu.VMEM((D, F), BF16), pltpu.VMEM((D, F), BF16), pltpu.VMEM((F, D), BF16)],
    )
    return pl.pallas_call(
        _expert_kernel,
        grid_spec=grid_spec,
        out_shape=jax.ShapeDtypeStruct((n_rows, D), F32),
        compiler_params=_cparams(("arbitrary",), EXPERT_VMEM_LIMIT_BYTES),
        name="moe_experts",
    )(blk_e, n_valid, x_rows, w_gate_up, b_glu, b_lin, w_down, b_down)


def _combine_kernel(dcur_ref, dnxt_ref, y_ref, x_ref, gate_ref, g_ref, b_ref, o_ref, buf, sem, *, tm, K, n_tiles):
    i = pl.program_id(0)
    slot = i % 2

    def row_copy(dref, s, t, r):
        return pltpu.make_async_copy(y_ref.at[pl.ds(dref[r, t], 1), :], buf.at[s, r, pl.ds(t, 1), :], sem.at[s])

    def issue(dref, s):
        def body(t, carry):
            for r in range(K):
                row_copy(dref, s, t, r).start()
            return carry
        lax.fori_loop(0, tm, body, 0)

    @pl.when(i == 0)
    def _():
        issue(dcur_ref, 0)

    @pl.when(i + 1 < n_tiles)
    def _():
        issue(dnxt_ref, 1 - slot)

    def drain(t, carry):
        for r in range(K):
            row_copy(dcur_ref, slot, t, r).wait()
        return carry
    lax.fori_loop(0, tm, drain, 0)

    f = gate_ref[:, 0:1] * buf[slot, 0]
    for r in range(1, K):
        f = f + gate_ref[:, r:r + 1] * buf[slot, r]
    o_ref[...] = _layer_norm(DEEPNORM_ALPHA * x_ref[...] + f, g_ref[...], b_ref[...])


def _combine(y_rows, dest_t, x1, gates, ln_g, ln_b):
    T, D = x1.shape
    K = MOE_TOP_K
    tm = COMBINE_TM
    n_tiles = T // tm
    kern = functools.partial(_combine_kernel, tm=tm, K=K, n_tiles=n_tiles)
    return pl.pallas_call(
        kern,
        grid=(n_tiles,),
        in_specs=[pl.BlockSpec((K, tm), lambda i: (0, i), memory_space=pltpu.SMEM),
                  pl.BlockSpec((K, tm), lambda i: (0, jnp.minimum(i + 1, n_tiles - 1)), memory_space=pltpu.SMEM),
                  pl.BlockSpec(memory_space=pl.ANY),
                  pl.BlockSpec((tm, D), lambda i: (i, 0)),
                  pl.BlockSpec((tm, K), lambda i: (i, 0)),
                  pl.BlockSpec((1, D), lambda i: (0, 0)), pl.BlockSpec((1, D), lambda i: (0, 0))],
        out_specs=pl.BlockSpec((tm, D), lambda i: (i, 0)),
        out_shape=jax.ShapeDtypeStruct((T, D), F32),
        scratch_shapes=[pltpu.VMEM((2, K, tm, D), F32), pltpu.SemaphoreType.DMA((2,))],
        compiler_params=_cparams(("arbitrary",)),
        name="moe_combine",
    )(dest_t, dest_t, y_rows, x1, gates, ln_g.reshape(1, D), ln_b.reshape(1, D))


def _moe(x1, e_t, g_t, p_t, cnt, w_gate_up, b_gate_up, w_down, b_down, ln_g, ln_b):
    T, D = x1.shape
    E, K, RB = MOE_EXPERTS, MOE_TOP_K, ROW_BLOCK
    n_blocks = -(-(T * K + E * (RB - 1)) // RB)
    n_rows = n_blocks * RB

    counts = cnt[:, 0].astype(I32)
    padded = (counts + RB - 1) // RB * RB
    pad_end = jnp.cumsum(padded)
    pad_start = pad_end - padded
    experts = jnp.arange(E, dtype=I32)
    group_start = jnp.sum(jnp.where(e_t[:, :, None] == experts, pad_start, 0), axis=-1)
    dest_t = (group_start + p_t).astype(I32)
    n_valid = (pad_end[-1] // RB).astype(I32)
    blk_ids = jnp.minimum(jnp.arange(n_blocks, dtype=I32), n_valid - 1)
    blk_e = jnp.minimum(jnp.sum((pad_end[None, :] <= blk_ids[:, None] * RB).astype(I32), axis=1), E - 1)
    zero_start = jnp.maximum(pad_end - RB, 0).astype(I32)
    zero_valid = (counts > 0).astype(I32)
    n_valid = n_valid.reshape(1)

    x_rows = _dispatch(x1, dest_t, zero_start, zero_valid, n_valid, n_rows)
    y_rows = _experts(x_rows, blk_e, n_valid, w_gate_up, b_gate_up[:, None, 0::2], b_gate_up[:, None, 1::2],
                      w_down, b_down[:, None, :])
    return _combine(y_rows, dest_t, x1, g_t.T, ln_g, ln_b)


def kernel(x, positions, attn_w_qkv, attn_w_o, mlstm_w_in, mlstm_b_gates, mlstm_norm_g, mlstm_w_out,
           ln_mix_g, ln_mix_b, ln_ffn_g, ln_ffn_b, router_w, router_b,
           w_gate_up, b_gate_up, w_down, b_down):
    B, S, D = x.shape
    T = B * S
    xt = x.reshape(T, D)
    NH = MLSTM_HEADS

    for layer in range(DEPTH):
        slot = layer // 2
        if layer % 2 == 0:
            hd = attn_w_o.shape[1]
            qkv = _qkv_rope(xt, positions.reshape(T, 1), attn_w_qkv[slot].astype(BF16), hd)
            mixed = _moba_attention(qkv, B, S, hd)
            w_out = attn_w_o[slot].astype(BF16)
        else:
            vw = mlstm_w_out.shape[1]
            qw = vw // 2
            w_in = mlstm_w_in[slot]
            dqk = qw // NH
            col_scale = jnp.concatenate([jnp.ones((qw,), F32), jnp.full((qw,), dqk ** -0.5, F32),
                                         jnp.ones((2 * vw,), F32)])
            w_main = (w_in[:, :2 * qw + 2 * vw] * col_scale).astype(BF16)
            w_gate = jnp.pad(w_in[:, 2 * qw + 2 * vw:], ((0, 0), (0, LANES - 2 * NH))).astype(BF16)
            proj, gates = _mlstm_proj(xt, w_main, w_gate)
            bg = mlstm_b_gates[slot]
            b_col = jnp.pad(bg, (0, LANES - 2 * NH)).reshape(1, LANES)
            b_row = bg.reshape(2 * NH, 1)
            mixed = _mlstm_mix(proj, gates, gates[:, :2 * NH].T, b_col, b_row,
                               mlstm_norm_g[slot].reshape(1, vw), B, S)
            w_out = mlstm_w_out[slot].astype(BF16)
        x1, e_t, g_t, p_t, cnt = _mix_out_router(mixed, w_out, xt, ln_mix_g[layer], ln_mix_b[layer],
                                                 router_w[layer], router_b[layer])
        xt = _moe(x1, e_t, g_t, p_t, cnt, w_gate_up[layer], b_gate_up[layer], w_down[layer], b_down[layer],
                  ln_ffn_g[layer], ln_ffn_b[layer])
    return xt.reshape(B, S, D)
```

```python
import functools

import jax
import jax.numpy as jnp
from jax import lax
from jax.experimental import pallas as pl
from jax.experimental.pallas import tpu as pltpu

F32 = jnp.float32
BF16 = jnp.bfloat16
I32 = jnp.int32

DEPTH = 2
ATTN_HEAD_DIM = 64
ROT_DIM = ATTN_HEAD_DIM // 4
ROPE_THETA = 500000.0
MOBA_BLOCK = 256
MOBA_TOP_K = 3
MLSTM_HEADS = 4
MOE_EXPERTS = 32
MOE_TOP_K = 4
SWIGLU_LIMIT = 7.0
SWIGLU_ALPHA = 1.702
DEEPNORM_ALPHA = (2 * DEPTH) ** 0.25
LN_EPS = 1e-5

LANES = 128
SUBLANES = 8
BF16_SUBLANES = 16
MXU_DIM = 256
VMEM_LIMIT_BYTES = 48 * 1024 * 1024
EXPERT_VMEM_LIMIT_BYTES = 58 * 1024 * 1024

PROJ_TM = 512
PROJ_TN = 512
LN_TM = 512
MLSTM_L = 256
ROW_BLOCK = 512
DISPATCH_TM = 512
COMBINE_TM = 256

NT_DIMS = (((1,), (1,)), ((), ()))
TN_DIMS = (((0,), (0,)), ((), ()))
NEG_INF = float("-inf")
MASKED = -1e30
LOG2_E = 1.4426950408889634


def _cparams(sem, vmem=VMEM_LIMIT_BYTES):
    return pltpu.CompilerParams(dimension_semantics=sem, vmem_limit_bytes=vmem)


def _qkv_rope_kernel(pos_ref, invf_ref, x_ref, w_ref, o_ref, *, n_rope_chunks, n_q_chunks, tn):
    ang = pos_ref[...].astype(F32) * invf_ref[...]
    d = lax.broadcasted_iota(I32, ang.shape, 1) & (ATTN_HEAD_DIM - 1)
    cos = jnp.cos(ang)
    sin = jnp.sin(ang)
    half = ROT_DIM // 2
    c_tab = jnp.where(d < ROT_DIM, cos, 1.0)
    s_up = jnp.where(d < half, -sin, 0.0)
    s_dn = jnp.where((d >= half) & (d < ROT_DIM), sin, 0.0)

    xb = x_ref[...].astype(BF16)
    for c in range(w_ref.shape[1] // tn):
        acc = jnp.dot(xb, w_ref[:, c * tn:(c + 1) * tn], preferred_element_type=F32)
        if c >= n_rope_chunks:
            o_ref[:, c * tn:(c + 1) * tn] = acc.astype(o_ref.dtype)
            continue
        scale = ATTN_HEAD_DIM ** -0.5 * LOG2_E if c < n_q_chunks else 1.0
        for s in range(tn // LANES):
            blk = acc[:, s * LANES:(s + 1) * LANES]
            r = (blk * c_tab + pltpu.roll(blk, LANES - half, 1) * s_up + pltpu.roll(blk, half, 1) * s_dn)
            lo = c * tn + s * LANES
            o_ref[:, lo:lo + LANES] = (r * scale).astype(o_ref.dtype)


def _qkv_rope(x, pos, w_bf16, hd):
    T, D = x.shape
    N = w_bf16.shape[1]
    tm, tn = PROJ_TM, PROJ_TN
    inv_freq = ROPE_THETA ** (-jnp.arange(0, ROT_DIM, 2, dtype=F32) / ROT_DIM)
    lane_d = jnp.arange(LANES) % ATTN_HEAD_DIM
    invf = jnp.where(lane_d < ROT_DIM, inv_freq[lane_d % (ROT_DIM // 2)], 0.0).reshape(1, LANES).astype(F32)
    kern = functools.partial(_qkv_rope_kernel, n_rope_chunks=2 * hd // tn, n_q_chunks=hd // tn, tn=tn)
    return pl.pallas_call(
        kern,
        grid=(T // tm,),
        in_specs=[pl.BlockSpec((tm, 1), lambda i: (i, 0)),
                  pl.BlockSpec((1, LANES), lambda i: (0, 0)),
                  pl.BlockSpec((tm, D), lambda i: (i, 0)),
                  pl.BlockSpec((D, N), lambda i: (0, 0))],
        out_specs=pl.BlockSpec((tm, N), lambda i: (i, 0)),
        out_shape=jax.ShapeDtypeStruct((T, N), BF16),
        compiler_params=_cparams(("parallel",)),
        name="qkv_rope",
    )(pos, invf, x, w_bf16)


def _moba_kernel(q_ref, k_ref, v_ref, o_ref, kmh_ref, kml_ref, ka_ref, vt_ref, q2t_ref,
                 m_ref, acc_ref, *score_bufs, nb, blk):
    s_refs, smax_refs = score_bufs[:4], score_bufs[4:]
    i = pl.program_id(2)
    hd = ATTN_HEAD_DIM
    nbp = kmh_ref.shape[0]
    va = vt_ref.shape[2]

    @pl.when(i == 0)
    def _():
        rows = [jnp.mean(k_ref[j * blk:(j + 1) * blk, :].astype(F32), axis=0, keepdims=True)
                for j in range(nb)]
        if nbp > nb:
            rows.append(jnp.zeros((nbp - nb, LANES), F32))
        km = jnp.concatenate(rows, axis=0)
        hi = km.astype(BF16)
        kmh_ref[...] = hi
        kml_ref[...] = (km - hi.astype(F32)).astype(BF16)
        lane = lax.broadcasted_iota(I32, (blk, LANES), 1)
        sub_v = lax.broadcasted_iota(I32, (va - hd, blk), 0)
        ones_row = jnp.where(sub_v == 0, 1.0, 0.0).astype(BF16)
        for j in range(nb):
            ka_ref[j * blk:(j + 1) * blk, :LANES] = k_ref[j * blk:(j + 1) * blk, :]
            ka_ref[j * blk:(j + 1) * blk, LANES:] = jnp.where(lane == j, 1.0, 0.0).astype(BF16)
            vt = v_ref[j * blk:(j + 1) * blk, :].astype(F32).T.astype(BF16)
            vt_ref[j, 0] = jnp.concatenate([vt[:hd], ones_row], axis=0)
            vt_ref[j, 1] = jnp.concatenate([vt[hd:], ones_row], axis=0)
        q2t_ref[LANES + nbp:, :] = jnp.zeros((LANES - nbp, 2 * blk), BF16)

    qt = q_ref[...].astype(F32).T
    sub_q = lax.broadcasted_iota(I32, qt.shape, 0)
    zero = jnp.zeros_like(qt)
    q2t = jnp.concatenate([jnp.where(sub_q < hd, qt, zero), jnp.where(sub_q >= hd, qt, zero)],
                          axis=1).astype(BF16)
    q2t_ref[:LANES, :] = q2t

    gate = (jnp.dot(kmh_ref[...], q2t, preferred_element_type=F32)
            + jnp.dot(kml_ref[...], q2t, preferred_element_type=F32))
    sub = lax.broadcasted_iota(I32, gate.shape, 0)
    sub_f = sub.astype(F32)
    g = jnp.where(sub < i, gate, NEG_INF)
    sel = jnp.zeros(gate.shape, F32)
    for _ in range(MOBA_TOP_K):
        mx = jnp.max(g, axis=0, keepdims=True)
        idx = jnp.min(jnp.where(g == mx, sub_f, float(nbp)), axis=0, keepdims=True)
        hit = (sub_f == idx) & (mx > NEG_INF)
        sel = jnp.where(hit, 1.0, sel)
        g = jnp.where(hit, NEG_INF, g)
    q2t_ref[LANES:LANES + nbp, :] = jnp.where(sel > 0.0, 0.0, MASKED).astype(BF16)

    def produce(pos, buf):
        s_ref, smax_ref = buf
        if isinstance(pos, int) and pos == 0:
            ki = k_ref[pl.ds(pl.multiple_of(i * blk, blk), blk), :]
            s = jnp.dot(ki, q2t, preferred_element_type=F32)
            key = lax.broadcasted_iota(I32, s.shape, 0)
            qry = lax.broadcasted_iota(I32, s.shape, 1) & (blk - 1)
            s = jnp.where(key <= qry, s, MASKED)
        else:
            j = jnp.minimum(pos - 1, i)
            kj = ka_ref[pl.ds(pl.multiple_of(j * blk, blk), blk), :]
            s = jnp.dot(kj, q2t_ref[...], preferred_element_type=F32)
        s_ref[...] = s
        smax_ref[...] = jnp.max(s, axis=0, keepdims=True)

    def consume(pos, buf):
        s_ref, smax_ref = buf
        j = jnp.where(pos == 0, i, jnp.minimum(pos - 1, i))
        m_old = m_ref[...]
        m_new = jnp.maximum(m_old, smax_ref[...])
        a = jnp.exp2(m_old - m_new)
        pb = jnp.exp2(s_ref[...] - m_new).astype(BF16)
        upd = jnp.concatenate([jnp.dot(vt_ref[j, 0], pb[:, :blk], preferred_element_type=F32),
                               jnp.dot(vt_ref[j, 1], pb[:, blk:], preferred_element_type=F32)], axis=1)
        acc_ref[...] = a * acc_ref[...] + upd
        m_ref[...] = m_new

    m_ref[...] = jnp.full(m_ref.shape, MASKED, F32)
    acc_ref[...] = jnp.zeros_like(acc_ref)
    buf_a, buf_b, buf_c, buf_d = [(s_refs[n], smax_refs[n]) for n in range(4)]
    produce(0, buf_a)
    produce(1, buf_b)

    def body(t, carry):
        base = 4 * t
        produce(base + 2, buf_c)
        produce(base + 3, buf_d)
        consume(base, buf_a)
        consume(base + 1, buf_b)
        produce(base + 4, buf_a)
        produce(base + 5, buf_b)
        consume(base + 2, buf_c)
        consume(base + 3, buf_d)
        return carry

    lax.fori_loop(0, i // 4 + 1, body, 0)
    acc = acc_ref[...]
    out_t = acc[:hd] / acc[hd:hd + 1]
    o_ref[...] = jnp.concatenate([out_t[:, :blk], out_t[:, blk:]], axis=0).T.astype(o_ref.dtype)


def _moba_attention(qkv, B, S, hd):
    T = B * S
    blk = MOBA_BLOCK
    nb = S // blk
    nbp = -(-nb // BF16_SUBLANES) * BF16_SUBLANES
    va = ATTN_HEAD_DIM + BF16_SUBLANES
    n_pairs = hd // LANES
    kern = functools.partial(_moba_kernel, nb=nb, blk=blk)
    return pl.pallas_call(
        kern,
        grid=(B, n_pairs, nb),
        in_specs=[pl.BlockSpec((blk, LANES), lambda b, h, i: (b * nb + i, h)),
                  pl.BlockSpec((S, LANES), lambda b, h, i: (b, n_pairs + h)),
                  pl.BlockSpec((S, LANES), lambda b, h, i: (b, 2 * n_pairs + h))],
        out_specs=pl.BlockSpec((blk, LANES), lambda b, h, i: (b * nb + i, h)),
        out_shape=jax.ShapeDtypeStruct((T, hd), BF16),
        scratch_shapes=[pltpu.VMEM((nbp, LANES), BF16), pltpu.VMEM((nbp, LANES), BF16),
                        pltpu.VMEM((S, 2 * LANES), BF16), pltpu.VMEM((nb, 2, va, blk), BF16),
                        pltpu.VMEM((2 * LANES, 2 * blk), BF16),
                        pltpu.VMEM((1, 2 * blk), F32), pltpu.VMEM((va, 2 * blk), F32)]
                       + [pltpu.VMEM((blk, 2 * blk), F32)] * 4 + [pltpu.VMEM((1, 2 * blk), F32)] * 4,
        compiler_params=_cparams(("parallel", "parallel", "arbitrary")),
        name="moba_attention",
    )(qkv, qkv, qkv)


def _mlstm_proj_kernel(x_ref, w_ref, wg_ref, o_ref, g_ref, *, tn):
    xb = x_ref[...].astype(BF16)
    for c in range(w_ref.shape[1] // tn):
        o_ref[:, c * tn:(c + 1) * tn] = jnp.dot(xb, w_ref[:, c * tn:(c + 1) * tn],
                                                preferred_element_type=F32).astype(o_ref.dtype)
    g_ref[...] = jnp.dot(xb, wg_ref[...], preferred_element_type=F32)


def _mlstm_proj(x, w_main, w_gate):
    T, D = x.shape
    N = w_main.shape[1]
    tm = PROJ_TM
    return pl.pallas_call(
        functools.partial(_mlstm_proj_kernel, tn=PROJ_TN),
        grid=(T // tm,),
        in_specs=[pl.BlockSpec((tm, D), lambda i: (i, 0)),
                  pl.BlockSpec((D, N), lambda i: (0, 0)),
                  pl.BlockSpec((D, LANES), lambda i: (0, 0))],
        out_specs=[pl.BlockSpec((tm, N), lambda i: (i, 0)),
                   pl.BlockSpec((tm, LANES), lambda i: (i, 0))],
        out_shape=[jax.ShapeDtypeStruct((T, N), BF16), jax.ShapeDtypeStruct((T, LANES), F32)],
        compiler_params=_cparams(("parallel",)),
        name="mlstm_proj",
    )(x, w_main, w_gate)


def _log_sigmoid(x):
    return jnp.minimum(x, 0.0) - jnp.log(1.0 + jnp.exp(-jnp.abs(x)))


def _mlstm_kernel(q_ref, k_ref, v_ref, o_ref, gc_ref, gr_ref, bc_ref, br_ref, ng_ref, out_ref,
                  C_ref, n_ref, m_ref, *, L, NH, DQK, DV):
    c = pl.program_id(1)

    @pl.when(c == 0)
    def _():
        C_ref[...] = jnp.zeros_like(C_ref)
        n_ref[...] = jnp.zeros_like(n_ref)
        m_ref[...] = jnp.zeros_like(m_ref)

    gcol = gc_ref[...] + bc_ref[...]
    grow = gr_ref[...] + br_ref[...]
    lane = lax.broadcasted_iota(I32, gcol.shape, 1)
    lf_col = jnp.where((lane >= NH) & (lane < 2 * NH), _log_sigmoid(gcol), 0.0)
    lf_row = _log_sigmoid(grow)
    row = lax.broadcasted_iota(I32, (L, L), 0)
    col = lax.broadcasted_iota(I32, (L, L), 1)
    causal = col <= row
    tri = causal.astype(F32)
    hp = lax.Precision.HIGHEST
    b_cols = jnp.dot(tri, lf_col, preferred_element_type=F32, precision=hp)
    b_rows = lax.dot_general(lf_row, tri, NT_DIMS, preferred_element_type=F32, precision=hp)

    for h in range(NH):
        i_row = grow[h:h + 1, :]
        i_col = gcol[:, h:h + 1]
        b_row = b_rows[NH + h:NH + h + 1, :]
        b_col = b_cols[:, NH + h:NH + h + 1]
        m_prev = m_ref[h:h + 1, 0:1]
        qh = q_ref[:, h * DQK:(h + 1) * DQK]
        kh = k_ref[:, h * DQK:(h + 1) * DQK]
        vh = v_ref[:, h * DV:(h + 1) * DV]

        D = jnp.where(causal, b_col - b_row + i_row, NEG_INF)
        g = b_col + m_prev
        m_t = jnp.maximum(g, jnp.max(D, axis=1, keepdims=True))
        w_inter = jnp.exp(g - m_t)
        qk = lax.dot_general(qh, kh, NT_DIMS, preferred_element_type=F32)
        A = jnp.exp(D - m_t) * qk
        num = (w_inter * jnp.dot(qh, C_ref[h].astype(BF16), preferred_element_type=F32)
               + jnp.dot(A.astype(BF16), vh, preferred_element_type=F32))
        qn = jnp.sum(qh.astype(F32) * n_ref[h:h + 1, :], axis=1, keepdims=True)
        den = w_inter * qn + jnp.sum(A, axis=1, keepdims=True)
        hh = num / jnp.maximum(jnp.abs(den), jnp.exp(-m_t))

        m_new = m_t[L - 1:L, :]
        b_last = b_col[L - 1:L, :]
        decay = jnp.exp(b_last + m_prev - m_new)
        w_s = jnp.exp(b_last - b_col + i_col - m_new)
        kw = kh.astype(F32) * w_s
        C_ref[h] = decay * C_ref[h] + lax.dot_general(kw.astype(BF16), vh, TN_DIMS,
                                                      preferred_element_type=F32)
        n_ref[h:h + 1, :] = decay * n_ref[h:h + 1, :] + jnp.sum(kw, axis=0, keepdims=True)
        m_ref[h:h + 1, :] = jnp.broadcast_to(m_new, (1, LANES))

        mu = jnp.mean(hh, axis=1, keepdims=True)
        hc = hh - mu
        var = jnp.mean(hc * hc, axis=1, keepdims=True)
        hn = hc * lax.rsqrt(var + LN_EPS) * ng_ref[:, h * DV:(h + 1) * DV]
        og = jax.nn.sigmoid(o_ref[:, h * DV:(h + 1) * DV].astype(F32))
        out_ref[:, h * DV:(h + 1) * DV] = (og * hn).astype(out_ref.dtype)


def _mlstm_mix(proj, gates, gates_t, b_col, b_row, norm_g, B, S):
    T = B * S
    NH = MLSTM_HEADS
    DV = norm_g.shape[1] // NH
    DQK = DV // 2
    L = MLSTM_L
    nc = S // L
    qw = NH * DQK
    vw = NH * DV
    kern = functools.partial(_mlstm_kernel, L=L, NH=NH, DQK=DQK, DV=DV)
    return pl.pallas_call(
        kern,
        grid=(B, nc),
        in_specs=[pl.BlockSpec((L, qw), lambda b, c: (b * nc + c, 0)),
                  pl.BlockSpec((L, qw), lambda b, c: (b * nc + c, 1)),
                  pl.BlockSpec((L, vw), lambda b, c: (b * nc + c, 2 * qw // vw)),
                  pl.BlockSpec((L, vw), lambda b, c: (b * nc + c, 2 * qw // vw + 1)),
                  pl.BlockSpec((L, LANES), lambda b, c: (b * nc + c, 0)),
                  pl.BlockSpec((SUBLANES, L), lambda b, c: (0, b * nc + c)),
                  pl.BlockSpec((1, LANES), lambda b, c: (0, 0)),
                  pl.BlockSpec((SUBLANES, 1), lambda b, c: (0, 0)),
                  pl.BlockSpec((1, vw), lambda b, c: (0, 0))],
        out_specs=pl.BlockSpec((L, vw), lambda b, c: (b * nc + c, 0)),
        out_shape=jax.ShapeDtypeStruct((T, vw), BF16),
        scratch_shapes=[pltpu.VMEM((NH, DQK, DV), F32), pltpu.VMEM((SUBLANES, DQK), F32),
                        pltpu.VMEM((SUBLANES, LANES), F32)],
        compiler_params=_cparams(("parallel", "arbitrary")),
        name="mlstm_mix",
    )(proj, proj, proj, proj, gates, gates_t, b_col, b_row, norm_g)


def _layer_norm(z, g, b):
    mu = jnp.mean(z, axis=-1, keepdims=True)
    zc = z - mu
    var = jnp.mean(zc * zc, axis=-1, keepdims=True)
    return zc * lax.rsqrt(var + LN_EPS) * g + b


def _mix_out_router_kernel(y_ref, w_ref, x_ref, g_ref, b_ref, rwt_ref, rb_ref,
                           x1_ref, et_ref, gt_ref, pt_ref, cnt_ref, carry_ref, *, tm, E, K):
    i = pl.program_id(0)

    @pl.when(i == 0)
    def _():
        carry_ref[...] = jnp.zeros_like(carry_ref)

    y = jnp.dot(y_ref[...], w_ref[...], preferred_element_type=F32)
    x1 = _layer_norm(DEEPNORM_ALPHA * x_ref[...] + y, g_ref[...], b_ref[...])
    x1_ref[...] = x1

    logits = lax.dot_general(rwt_ref[...], x1, NT_DIMS, preferred_element_type=F32,
                             precision=lax.Precision.HIGHEST) + rb_ref[...]
    sub = lax.broadcasted_iota(I32, logits.shape, 0).astype(F32)
    vals, hits = [], []
    cur = logits
    for _ in range(K):
        mx = jnp.max(cur, axis=0, keepdims=True)
        idx = jnp.min(jnp.where(cur == mx, sub, float(E)), axis=0, keepdims=True)
        hit = sub == idx
        vals.append(mx)
        hits.append(hit)
        cur = jnp.where(hit, NEG_INF, cur)

    chosen = jnp.zeros(logits.shape, F32)
    for hit in hits:
        chosen = jnp.where(hit, 1.0, chosen)
    r_ = lax.broadcasted_iota(I32, (tm, tm), 0)
    c_ = lax.broadcasted_iota(I32, (tm, tm), 1)
    before = (r_ < c_).astype(BF16)
    pos = jnp.dot(chosen.astype(BF16), before, preferred_element_type=F32) + carry_ref[:, 0:1]
    carry_ref[...] = carry_ref[...] + jnp.sum(chosen, axis=1, keepdims=True)
    cnt_ref[...] = carry_ref[...]

    ex = [jnp.exp(v - vals[0]) for v in vals]
    den = ex[0]
    for e_ in ex[1:]:
        den = den + e_
    for r in range(K):
        gt_ref[r:r + 1, :] = ex[r] / den
        et_ref[r:r + 1, :] = jnp.sum(jnp.where(hits[r], sub, 0.0), axis=0, keepdims=True).astype(I32)
        pt_ref[r:r + 1, :] = jnp.sum(jnp.where(hits[r], pos, 0.0), axis=0, keepdims=True).astype(I32)


def _mix_out_router(y, w_bf16, x, ln_g, ln_b, router_w, router_b):
    T, D = x.shape
    Kin = y.shape[1]
    E, K = MOE_EXPERTS, MOE_TOP_K
    tm = LN_TM
    kern = functools.partial(_mix_out_router_kernel, tm=tm, E=E, K=K)
    row = lambda i: (i, 0)
    fixed = lambda i: (0, 0)
    tok = lambda i: (0, i)
    return pl.pallas_call(
        kern,
        grid=(T // tm,),
        in_specs=[pl.BlockSpec((tm, Kin), row), pl.BlockSpec((Kin, D), fixed),
                  pl.BlockSpec((tm, D), row), pl.BlockSpec((1, D), fixed), pl.BlockSpec((1, D), fixed),
                  pl.BlockSpec((E, D), fixed), pl.BlockSpec((E, 1), fixed)],
        out_specs=[pl.BlockSpec((tm, D), row), pl.BlockSpec((K, tm), tok), pl.BlockSpec((K, tm), tok),
                   pl.BlockSpec((K, tm), tok), pl.BlockSpec((E, LANES), fixed)],
        out_shape=[jax.ShapeDtypeStruct((T, D), F32), jax.ShapeDtypeStruct((K, T), I32),
                   jax.ShapeDtypeStruct((K, T), F32), jax.ShapeDtypeStruct((K, T), I32),
                   jax.ShapeDtypeStruct((E, LANES), F32)],
        scratch_shapes=[pltpu.VMEM((E, LANES), F32)],
        compiler_params=_cparams(("arbitrary",)),
        name="mix_out_router",
    )(y, w_bf16, x, ln_g.reshape(1, D), ln_b.reshape(1, D), router_w.T, router_b.reshape(E, 1))


def _dispatch_kernel(zs_ref, zv_ref, nv_ref, dest_ref, x_ref, xr_ref, zbuf, sem, zsem, *, tm, K, E, RB, n_blocks):
    i = pl.program_id(0)

    def zero_copy(start):
        return pltpu.make_async_copy(zbuf, xr_ref.at[pl.ds(pl.multiple_of(start, RB), RB), :], zsem)

    @pl.when(i == 0)
    def _():
        zbuf[...] = jnp.zeros_like(zbuf)
        for e in range(E):
            @pl.when(zv_ref[e] > 0)
            def _():
                zero_copy(zs_ref[e]).start()

        def tail_start(p, carry):
            zero_copy(p * RB).start()
            return carry

        def tail_wait(p, carry):
            zero_copy(p * RB).wait()
            return carry

        lax.fori_loop(nv_ref[0], n_blocks, tail_start, 0)
        for e in range(E):
            @pl.when(zv_ref[e] > 0)
            def _():
                zero_copy(zs_ref[e]).wait()
        lax.fori_loop(nv_ref[0], n_blocks, tail_wait, 0)

    def row_copy(t, r):
        return pltpu.make_async_copy(x_ref.at[pl.ds(t, 1), :], xr_ref.at[pl.ds(dest_ref[r, t], 1), :], sem)

    def issue(t, carry):
        for r in range(K):
            row_copy(t, r).start()
        return carry

    def drain(t, carry):
        for r in range(K):
            row_copy(t, r).wait()
        return carry

    lax.fori_loop(0, tm, issue, 0)
    lax.fori_loop(0, tm, drain, 0)


def _dispatch(x1, dest_t, zero_start, zero_valid, n_valid, n_rows):
    T, D = x1.shape
    K, E, RB = MOE_TOP_K, MOE_EXPERTS, ROW_BLOCK
    tm = DISPATCH_TM
    kern = functools.partial(_dispatch_kernel, tm=tm, K=K, E=E, RB=RB, n_blocks=n_rows // RB)
    grid_spec = pltpu.PrefetchScalarGridSpec(
        num_scalar_prefetch=3,
        grid=(T // tm,),
        in_specs=[pl.BlockSpec((K, tm), lambda i, zs, zv, nv: (0, i), memory_space=pltpu.SMEM),
                  pl.BlockSpec((tm, D), lambda i, zs, zv, nv: (i, 0))],
        out_specs=pl.BlockSpec(memory_space=pl.ANY),
        scratch_shapes=[pltpu.VMEM((RB, D), F32), pltpu.SemaphoreType.DMA, pltpu.SemaphoreType.DMA],
    )
    return pl.pallas_call(
        kern,
        grid_spec=grid_spec,
        out_shape=jax.ShapeDtypeStruct((n_rows, D), F32),
        compiler_params=pltpu.CompilerParams(dimension_semantics=("arbitrary",),
                                             vmem_limit_bytes=VMEM_LIMIT_BYTES,
                                             has_side_effects=True),
        name="moe_dispatch",
    )(zero_start, zero_valid, n_valid, dest_t, x1)


def _expert_kernel(be_ref, nv_ref, x_ref, wgu_ref, bg_ref, bl_ref, wd_ref, bd_ref, y_ref,
                   wg_s, wl_s, wd_s):
    p = pl.program_id(0)
    valid = p < nv_ref[0]
    fresh = (p == 0) | (be_ref[p] != be_ref[jnp.maximum(p - 1, 0)])
    half = MXU_DIM // 2

    @pl.when(valid & fresh)
    def _():
        r_ = lax.broadcasted_iota(I32, (MXU_DIM, MXU_DIM), 0)
        c_ = lax.broadcasted_iota(I32, (MXU_DIM, MXU_DIM), 1)
        src = jnp.where(c_ < half, 2 * c_, 2 * (c_ - half) + 1)
        perm = jnp.where(r_ == src, 1.0, 0.0).astype(BF16)
        for cb in range(wgu_ref.shape[1] // MXU_DIM):
            blk = wgu_ref[:, cb * MXU_DIM:(cb + 1) * MXU_DIM].astype(BF16)
            sep = jnp.dot(blk, perm, preferred_element_type=F32).astype(BF16)
            wg_s[:, cb * half:(cb + 1) * half] = sep[:, :half]
            wl_s[:, cb * half:(cb + 1) * half] = sep[:, half:]
        wd_s[...] = wd_ref[...].astype(BF16)

    @pl.when(valid)
    def _():
        x = x_ref[...].astype(BF16)
        hg = jnp.dot(x, wg_s[...], preferred_element_type=F32) + bg_ref[...]
        hl = jnp.dot(x, wl_s[...], preferred_element_type=F32) + bl_ref[...]
        xg = jnp.minimum(hg, SWIGLU_LIMIT)
        xl = jnp.clip(hl, -SWIGLU_LIMIT, SWIGLU_LIMIT)
        act = xg * jax.nn.sigmoid(SWIGLU_ALPHA * xg) * (xl + 1.0)
        y_ref[...] = jnp.dot(act.astype(BF16), wd_s[...], preferred_element_type=F32) + bd_ref[...]

    @pl.when(jnp.logical_not(valid))
    def _():
        y_ref[...] = jnp.zeros_like(y_ref)


def _experts(x_rows, blk_e, n_valid, layer, w_gate_up, b_glu, b_lin, w_down, b_down):
    n_rows, D = x_rows.shape
    _, E, _, F2 = w_gate_up.shape
    F = F2 // 2
    RB = ROW_BLOCK
    n_blocks = n_rows // RB
    xblk = lambda p, be, nv: (jnp.minimum(p, nv[0] - 1), 0)
    yblk = lambda p, be, nv: (p, 0)
    wsel = lambda p, be, nv: (be[p], 0, 0)
    wsel_l = lambda p, be, nv: (layer, be[p], 0, 0)
    grid_spec = pltpu.PrefetchScalarGridSpec(
        num_scalar_prefetch=2,
        grid=(n_blocks,),
        in_specs=[pl.BlockSpec((RB, D), xblk),
                  pl.BlockSpec((None, None, D, F2), wsel_l),
                  pl.BlockSpec((None, 1, F), wsel), pl.BlockSpec((None, 1, F), wsel),
                  pl.BlockSpec((None, None, F, D), wsel_l), pl.BlockSpec((None, 1, D), wsel)],
        out_specs=pl.BlockSpec((RB, D), yblk),
        scratch_shapes=[pltpu.VMEM((D, F), BF16), pltpu.VMEM((D, F), BF16), pltpu.VMEM((F, D), BF16)],
    )
    return pl.pallas_call(
        _expert_kernel,
        grid_spec=grid_spec,
        out_shape=jax.ShapeDtypeStruct((n_rows, D), F32),
        compiler_params=_cparams(("arbitrary",), EXPERT_VMEM_LIMIT_BYTES),
        name="moe_experts",
    )(blk_e, n_valid, x_rows, w_gate_up, b_glu, b_lin, w_down, b_down)


def _combine_kernel(dcur_ref, dnxt_ref, y_ref, x_ref, gate_ref, g_ref, b_ref, o_ref, buf, sem, *, tm, K, n_tiles):
    i = pl.program_id(0)
    slot = i % 2

    def row_copy(dref, s, t, r):
        return pltpu.make_async_copy(y_ref.at[pl.ds(dref[r, t], 1), :], buf.at[s, r, pl.ds(t, 1), :], sem.at[s])

    def issue(dref, s):
        def body(t, carry):
            for r in range(K):
                row_copy(dref, s, t, r).start()
            return carry
        lax.fori_loop(0, tm, body, 0)

    @pl.when(i == 0)
    def _():
        issue(dcur_ref, 0)

    @pl.when(i + 1 < n_tiles)
    def _():
        issue(dnxt_ref, 1 - slot)

    def drain(t, carry):
        for r in range(K):
            row_copy(dcur_ref, slot, t, r).wait()
        return carry
    lax.fori_loop(0, tm, drain, 0)

    f = gate_ref[:, 0:1] * buf[slot, 0]
    for r in range(1, K):
        f = f + gate_ref[:, r:r + 1] * buf[slot, r]
    o_ref[...] = _layer_norm(DEEPNORM_ALPHA * x_ref[...] + f, g_ref[...], b_ref[...])


def _combine(y_rows, dest_t, x1, gates, ln_g, ln_b):
    T, D = x1.shape
    K = MOE_TOP_K
    tm = COMBINE_TM
    n_tiles = T // tm
    kern = functools.partial(_combine_kernel, tm=tm, K=K, n_tiles=n_tiles)
    return pl.pallas_call(
        kern,
        grid=(n_tiles,),
        in_specs=[pl.BlockSpec((K, tm), lambda i: (0, i), memory_space=pltpu.SMEM),
                  pl.BlockSpec((K, tm), lambda i: (0, jnp.minimum(i + 1, n_tiles - 1)), memory_space=pltpu.SMEM),
                  pl.BlockSpec(memory_space=pl.ANY),
                  pl.BlockSpec((tm, D), lambda i: (i, 0)),
                  pl.BlockSpec((tm, K), lambda i: (i, 0)),
                  pl.BlockSpec((1, D), lambda i: (0, 0)), pl.BlockSpec((1, D), lambda i: (0, 0))],
        out_specs=pl.BlockSpec((tm, D), lambda i: (i, 0)),
        out_shape=jax.ShapeDtypeStruct((T, D), F32),
        scratch_shapes=[pltpu.VMEM((2, K, tm, D), F32), pltpu.SemaphoreType.DMA((2,))],
        compiler_params=_cparams(("arbitrary",)),
        name="moe_combine",
    )(dest_t, dest_t, y_rows, x1, gates, ln_g.reshape(1, D), ln_b.reshape(1, D))


def _moe(x1, e_t, g_t, p_t, cnt, layer, w_gate_up, b_gate_up, w_down, b_down, ln_g, ln_b):
    T, D = x1.shape
    E, K, RB = MOE_EXPERTS, MOE_TOP_K, ROW_BLOCK
    n_blocks = -(-(T * K + E * (RB - 1)) // RB)
    n_rows = n_blocks * RB

    counts = cnt[:, 0].astype(I32)
    padded = (counts + RB - 1) // RB * RB
    pad_end = jnp.cumsum(padded)
    pad_start = pad_end - padded
    experts = jnp.arange(E, dtype=I32)
    group_start = jnp.sum(jnp.where(e_t[:, :, None] == experts, pad_start, 0), axis=-1)
    dest_t = (group_start + p_t).astype(I32)
    n_valid = (pad_end[-1] // RB).astype(I32)
    blk_ids = jnp.minimum(jnp.arange(n_blocks, dtype=I32), n_valid - 1)
    blk_e = jnp.minimum(jnp.sum((pad_end[None, :] <= blk_ids[:, None] * RB).astype(I32), axis=1), E - 1)
    zero_start = jnp.maximum(pad_end - RB, 0).astype(I32)
    zero_valid = (counts > 0).astype(I32)
    n_valid = n_valid.reshape(1)

    x_rows = _dispatch(x1, dest_t, zero_start, zero_valid, n_valid, n_rows)
    y_rows = _experts(x_rows, blk_e, n_valid, layer, w_gate_up, b_gate_up[:, None, 0::2],
                      b_gate_up[:, None, 1::2], w_down, b_down[:, None, :])
    return _combine(y_rows, dest_t, x1, g_t.T, ln_g, ln_b)


def kernel(x, positions, attn_w_qkv, attn_w_o, mlstm_w_in, mlstm_b_gates, mlstm_norm_g, mlstm_w_out,
           ln_mix_g, ln_mix_b, ln_ffn_g, ln_ffn_b, router_w, router_b,
           w_gate_up, b_gate_up, w_down, b_down):
    B, S, D = x.shape
    T = B * S
    xt = x.reshape(T, D)
    NH = MLSTM_HEADS

    for layer in range(DEPTH):
        slot = layer // 2
        if layer % 2 == 0:
            hd = attn_w_o.shape[1]
            qkv = _qkv_rope(xt, positions.reshape(T, 1), attn_w_qkv[slot].astype(BF16), hd)
            mixed = _moba_attention(qkv, B, S, hd)
            w_out = attn_w_o[slot].astype(BF16)
        else:
            vw = mlstm_w_out.shape[1]
            qw = vw // 2
            w_in = mlstm_w_in[slot]
            dqk = qw // NH
            col_scale = jnp.concatenate([jnp.ones((qw,), F32), jnp.full((qw,), dqk ** -0.5, F32),
                                         jnp.ones((2 * vw,), F32)])
            w_main = (w_in[:, :2 * qw + 2 * vw] * col_scale).astype(BF16)
            w_gate = jnp.pad(w_in[:, 2 * qw + 2 * vw:], ((0, 0), (0, LANES - 2 * NH))).astype(BF16)
            proj, gates = _mlstm_proj(xt, w_main, w_gate)
            bg = mlstm_b_gates[slot]
            b_col = jnp.pad(bg, (0, LANES - 2 * NH)).reshape(1, LANES)
            b_row = bg.reshape(2 * NH, 1)
            mixed = _mlstm_mix(proj, gates, gates[:, :2 * NH].T, b_col, b_row,
                               mlstm_norm_g[slot].reshape(1, vw), B, S)
            w_out = mlstm_w_out[slot].astype(BF16)
        x1, e_t, g_t, p_t, cnt = _mix_out_router(mixed, w_out, xt, ln_mix_g[layer], ln_mix_b[layer],
                                                 router_w[layer], router_b[layer])
        xt = _moe(x1, e_t, g_t, p_t, cnt, layer, w_gate_up, b_gate_up[layer], w_down, b_down[layer],
                  ln_ffn_g[layer], ln_ffn_b[layer])
    return xt.reshape(B, S, D)
```

```python
import functools

import jax
import jax.numpy as jnp
from jax import lax
from jax.experimental import pallas as pl
from jax.experimental.pallas import tpu as pltpu

F32 = jnp.float32
BF16 = jnp.bfloat16
I32 = jnp.int32

DEPTH = 2
ATTN_HEAD_DIM = 64
ROT_DIM = ATTN_HEAD_DIM // 4
ROPE_THETA = 500000.0
MOBA_BLOCK = 256
MOBA_TOP_K = 3
MLSTM_HEADS = 4
MOE_EXPERTS = 32
MOE_TOP_K = 4
SWIGLU_LIMIT = 7.0
SWIGLU_ALPHA = 1.702
DEEPNORM_ALPHA = (2 * DEPTH) ** 0.25
LN_EPS = 1e-5

LANES = 128
SUBLANES = 8
BF16_SUBLANES = 16
MXU_DIM = 256
VMEM_LIMIT_BYTES = 48 * 1024 * 1024
EXPERT_VMEM_LIMIT_BYTES = 58 * 1024 * 1024

PROJ_TM = 512
PROJ_TN = 512
MOE_TILE = 512
MLSTM_L = 256
ROW_BLOCK = 512
SORT_CHUNK = 256

NT_DIMS = (((1,), (1,)), ((), ()))
TN_DIMS = (((0,), (0,)), ((), ()))
NEG_INF = float("-inf")
MASKED = -1e30
LOG2_E = 1.4426950408889634


def _cparams(sem, vmem=VMEM_LIMIT_BYTES):
    return pltpu.CompilerParams(dimension_semantics=sem, vmem_limit_bytes=vmem)


def _local_rows(tm):
    rows = tm * MOE_TOP_K + MOE_EXPERTS * SUBLANES
    return -(-rows // SORT_CHUNK) * SORT_CHUNK


def _qkv_rope_kernel(pos_ref, invf_ref, x_ref, w_ref, o_ref, *, n_rope_chunks, n_q_chunks, tn):
    ang = pos_ref[...].astype(F32) * invf_ref[...]
    d = lax.broadcasted_iota(I32, ang.shape, 1) & (ATTN_HEAD_DIM - 1)
    cos = jnp.cos(ang)
    sin = jnp.sin(ang)
    half = ROT_DIM // 2
    c_tab = jnp.where(d < ROT_DIM, cos, 1.0)
    s_up = jnp.where(d < half, -sin, 0.0)
    s_dn = jnp.where((d >= half) & (d < ROT_DIM), sin, 0.0)

    xb = x_ref[...].astype(BF16)
    for c in range(w_ref.shape[1] // tn):
        acc = jnp.dot(xb, w_ref[:, c * tn:(c + 1) * tn], preferred_element_type=F32)
        if c >= n_rope_chunks:
            o_ref[:, c * tn:(c + 1) * tn] = acc.astype(o_ref.dtype)
            continue
        scale = ATTN_HEAD_DIM ** -0.5 * LOG2_E if c < n_q_chunks else 1.0
        for s in range(tn // LANES):
            blk = acc[:, s * LANES:(s + 1) * LANES]
            r = (blk * c_tab + pltpu.roll(blk, LANES - half, 1) * s_up + pltpu.roll(blk, half, 1) * s_dn)
            lo = c * tn + s * LANES
            o_ref[:, lo:lo + LANES] = (r * scale).astype(o_ref.dtype)


def _qkv_rope(x, pos, w_bf16, hd):
    T, D = x.shape
    N = w_bf16.shape[1]
    tm, tn = PROJ_TM, PROJ_TN
    inv_freq = ROPE_THETA ** (-jnp.arange(0, ROT_DIM, 2, dtype=F32) / ROT_DIM)
    lane_d = jnp.arange(LANES) % ATTN_HEAD_DIM
    invf = jnp.where(lane_d < ROT_DIM, inv_freq[lane_d % (ROT_DIM // 2)], 0.0).reshape(1, LANES).astype(F32)
    kern = functools.partial(_qkv_rope_kernel, n_rope_chunks=2 * hd // tn, n_q_chunks=hd // tn, tn=tn)
    return pl.pallas_call(
        kern,
        grid=(T // tm,),
        in_specs=[pl.BlockSpec((tm, 1), lambda i: (i, 0)),
                  pl.BlockSpec((1, LANES), lambda i: (0, 0)),
                  pl.BlockSpec((tm, D), lambda i: (i, 0)),
                  pl.BlockSpec((D, N), lambda i: (0, 0))],
        out_specs=pl.BlockSpec((tm, N), lambda i: (i, 0)),
        out_shape=jax.ShapeDtypeStruct((T, N), BF16),
        compiler_params=_cparams(("parallel",)),
        name="qkv_rope",
    )(pos, invf, x, w_bf16)


def _moba_kernel(q_ref, k_ref, v_ref, o_ref, kmh_ref, kml_ref, ka_ref, vt_ref, q2t_ref,
                 m_ref, acc_ref, *score_bufs, nb, blk):
    s_refs, smax_refs = score_bufs[:4], score_bufs[4:]
    i = pl.program_id(2)
    hd = ATTN_HEAD_DIM
    nbp = kmh_ref.shape[0]
    va = vt_ref.shape[2]

    @pl.when(i == 0)
    def _():
        rows = [jnp.mean(k_ref[j * blk:(j + 1) * blk, :].astype(F32), axis=0, keepdims=True)
                for j in range(nb)]
        if nbp > nb:
            rows.append(jnp.zeros((nbp - nb, LANES), F32))
        km = jnp.concatenate(rows, axis=0)
        hi = km.astype(BF16)
        kmh_ref[...] = hi
        kml_ref[...] = (km - hi.astype(F32)).astype(BF16)
        lane = lax.broadcasted_iota(I32, (blk, LANES), 1)
        sub_v = lax.broadcasted_iota(I32, (va - hd, blk), 0)
        ones_row = jnp.where(sub_v == 0, 1.0, 0.0).astype(BF16)
        for j in range(nb):
            ka_ref[j * blk:(j + 1) * blk, :LANES] = k_ref[j * blk:(j + 1) * blk, :]
            ka_ref[j * blk:(j + 1) * blk, LANES:] = jnp.where(lane == j, 1.0, 0.0).astype(BF16)
            vt = v_ref[j * blk:(j + 1) * blk, :].astype(F32).T.astype(BF16)
            vt_ref[j, 0] = jnp.concatenate([vt[:hd], ones_row], axis=0)
            vt_ref[j, 1] = jnp.concatenate([vt[hd:], ones_row], axis=0)
        q2t_ref[LANES + nbp:, :] = jnp.zeros((LANES - nbp, 2 * blk), BF16)

    qt = q_ref[...].astype(F32).T
    sub_q = lax.broadcasted_iota(I32, qt.shape, 0)
    zero = jnp.zeros_like(qt)
    q2t = jnp.concatenate([jnp.where(sub_q < hd, qt, zero), jnp.where(sub_q >= hd, qt, zero)],
                          axis=1).astype(BF16)
    q2t_ref[:LANES, :] = q2t

    gate = (jnp.dot(kmh_ref[...], q2t, preferred_element_type=F32)
            + jnp.dot(kml_ref[...], q2t, preferred_element_type=F32))
    sub = lax.broadcasted_iota(I32, gate.shape, 0)
    sub_f = sub.astype(F32)
    g = jnp.where(sub < i, gate, NEG_INF)
    sel = jnp.zeros(gate.shape, F32)
    for _ in range(MOBA_TOP_K):
        mx = jnp.max(g, axis=0, keepdims=True)
        idx = jnp.min(jnp.where(g == mx, sub_f, float(nbp)), axis=0, keepdims=True)
        hit = (sub_f == idx) & (mx > NEG_INF)
        sel = jnp.where(hit, 1.0, sel)
        g = jnp.where(hit, NEG_INF, g)
    q2t_ref[LANES:LANES + nbp, :] = jnp.where(sel > 0.0, 0.0, MASKED).astype(BF16)

    def produce(pos, buf):
        s_ref, smax_ref = buf
        if isinstance(pos, int) and pos == 0:
            ki = k_ref[pl.ds(pl.multiple_of(i * blk, blk), blk), :]
            s = jnp.dot(ki, q2t, preferred_element_type=F32)
            key = lax.broadcasted_iota(I32, s.shape, 0)
            qry = lax.broadcasted_iota(I32, s.shape, 1) & (blk - 1)
            s = jnp.where(key <= qry, s, MASKED)
        else:
            j = jnp.minimum(pos - 1, i)
            kj = ka_ref[pl.ds(pl.multiple_of(j * blk, blk), blk), :]
            s = jnp.dot(kj, q2t_ref[...], preferred_element_type=F32)
        s_ref[...] = s
        smax_ref[...] = jnp.max(s, axis=0, keepdims=True)

    def consume(pos, buf):
        s_ref, smax_ref = buf
        j = jnp.where(pos == 0, i, jnp.minimum(pos - 1, i))
        m_old = m_ref[...]
        m_new = jnp.maximum(m_old, smax_ref[...])
        a = jnp.exp2(m_old - m_new)
        pb = jnp.exp2(s_ref[...] - m_new).astype(BF16)
        upd = jnp.concatenate([jnp.dot(vt_ref[j, 0], pb[:, :blk], preferred_element_type=F32),
                               jnp.dot(vt_ref[j, 1], pb[:, blk:], preferred_element_type=F32)], axis=1)
        acc_ref[...] = a * acc_ref[...] + upd
        m_ref[...] = m_new

    m_ref[...] = jnp.full(m_ref.shape, MASKED, F32)
    acc_ref[...] = jnp.zeros_like(acc_ref)
    buf_a, buf_b, buf_c, buf_d = [(s_refs[n], smax_refs[n]) for n in range(4)]
    produce(0, buf_a)
    produce(1, buf_b)

    def body(t, carry):
        base = 4 * t
        produce(base + 2, buf_c)
        produce(base + 3, buf_d)
        consume(base, buf_a)
        consume(base + 1, buf_b)
        produce(base + 4, buf_a)
        produce(base + 5, buf_b)
        consume(base + 2, buf_c)
        consume(base + 3, buf_d)
        return carry

    lax.fori_loop(0, i // 4 + 1, body, 0)
    acc = acc_ref[...]
    out_t = acc[:hd] / acc[hd:hd + 1]
    o_ref[...] = jnp.concatenate([out_t[:, :blk], out_t[:, blk:]], axis=0).T.astype(o_ref.dtype)


def _moba_attention(qkv, B, S, hd):
    T = B * S
    blk = MOBA_BLOCK
    nb = S // blk
    nbp = -(-nb // BF16_SUBLANES) * BF16_SUBLANES
    va = ATTN_HEAD_DIM + BF16_SUBLANES
    n_pairs = hd // LANES
    kern = functools.partial(_moba_kernel, nb=nb, blk=blk)
    return pl.pallas_call(
        kern,
        grid=(B, n_pairs, nb),
        in_specs=[pl.BlockSpec((blk, LANES), lambda b, h, i: (b * nb + i, h)),
                  pl.BlockSpec((S, LANES), lambda b, h, i: (b, n_pairs + h)),
                  pl.BlockSpec((S, LANES), lambda b, h, i: (b, 2 * n_pairs + h))],
        out_specs=pl.BlockSpec((blk, LANES), lambda b, h, i: (b * nb + i, h)),
        out_shape=jax.ShapeDtypeStruct((T, hd), BF16),
        scratch_shapes=[pltpu.VMEM((nbp, LANES), BF16), pltpu.VMEM((nbp, LANES), BF16),
                        pltpu.VMEM((S, 2 * LANES), BF16), pltpu.VMEM((nb, 2, va, blk), BF16),
                        pltpu.VMEM((2 * LANES, 2 * blk), BF16),
                        pltpu.VMEM((1, 2 * blk), F32), pltpu.VMEM((va, 2 * blk), F32)]
                       + [pltpu.VMEM((blk, 2 * blk), F32)] * 4 + [pltpu.VMEM((1, 2 * blk), F32)] * 4,
        compiler_params=_cparams(("parallel", "parallel", "arbitrary")),
        name="moba_attention",
    )(qkv, qkv, qkv)


def _mlstm_proj_kernel(x_ref, w_ref, wg_ref, o_ref, g_ref, *, tn):
    xb = x_ref[...].astype(BF16)
    for c in range(w_ref.shape[1] // tn):
        o_ref[:, c * tn:(c + 1) * tn] = jnp.dot(xb, w_ref[:, c * tn:(c + 1) * tn],
                                                preferred_element_type=F32).astype(o_ref.dtype)
    g_ref[...] = jnp.dot(xb, wg_ref[...], preferred_element_type=F32)


def _mlstm_proj(x, w_main, w_gate):
    T, D = x.shape
    N = w_main.shape[1]
    tm = PROJ_TM
    return pl.pallas_call(
        functools.partial(_mlstm_proj_kernel, tn=PROJ_TN),
        grid=(T // tm,),
        in_specs=[pl.BlockSpec((tm, D), lambda i: (i, 0)),
                  pl.BlockSpec((D, N), lambda i: (0, 0)),
                  pl.BlockSpec((D, LANES), lambda i: (0, 0))],
        out_specs=[pl.BlockSpec((tm, N), lambda i: (i, 0)),
                   pl.BlockSpec((tm, LANES), lambda i: (i, 0))],
        out_shape=[jax.ShapeDtypeStruct((T, N), BF16), jax.ShapeDtypeStruct((T, LANES), F32)],
        compiler_params=_cparams(("parallel",)),
        name="mlstm_proj",
    )(x, w_main, w_gate)


def _log_sigmoid(x):
    return jnp.minimum(x, 0.0) - jnp.log(1.0 + jnp.exp(-jnp.abs(x)))


def _mlstm_kernel(q_ref, k_ref, v_ref, o_ref, gc_ref, gr_ref, bc_ref, br_ref, ng_ref, out_ref,
                  C_ref, n_ref, m_ref, *, L, NH, DQK, DV):
    c = pl.program_id(1)

    @pl.when(c == 0)
    def _():
        C_ref[...] = jnp.zeros_like(C_ref)
        n_ref[...] = jnp.zeros_like(n_ref)
        m_ref[...] = jnp.zeros_like(m_ref)

    gcol = gc_ref[...] + bc_ref[...]
    grow = gr_ref[...] + br_ref[...]
    lane = lax.broadcasted_iota(I32, gcol.shape, 1)
    lf_col = jnp.where((lane >= NH) & (lane < 2 * NH), _log_sigmoid(gcol), 0.0)
    lf_row = _log_sigmoid(grow)
    row = lax.broadcasted_iota(I32, (L, L), 0)
    col = lax.broadcasted_iota(I32, (L, L), 1)
    causal = col <= row
    tri = causal.astype(F32)
    hp = lax.Precision.HIGHEST
    b_cols = jnp.dot(tri, lf_col, preferred_element_type=F32, precision=hp)
    b_rows = lax.dot_general(lf_row, tri, NT_DIMS, preferred_element_type=F32, precision=hp)

    for h in range(NH):
        i_row = grow[h:h + 1, :]
        i_col = gcol[:, h:h + 1]
        b_row = b_rows[NH + h:NH + h + 1, :]
        b_col = b_cols[:, NH + h:NH + h + 1]
        m_prev = m_ref[h:h + 1, 0:1]
        qh = q_ref[:, h * DQK:(h + 1) * DQK]
        kh = k_ref[:, h * DQK:(h + 1) * DQK]
        vh = v_ref[:, h * DV:(h + 1) * DV]

        D = jnp.where(causal, b_col - b_row + i_row, NEG_INF)
        g = b_col + m_prev
        m_t = jnp.maximum(g, jnp.max(D, axis=1, keepdims=True))
        w_inter = jnp.exp(g - m_t)
        qk = lax.dot_general(qh, kh, NT_DIMS, preferred_element_type=F32)
        A = jnp.exp(D - m_t) * qk
        num = (w_inter * jnp.dot(qh, C_ref[h].astype(BF16), preferred_element_type=F32)
               + jnp.dot(A.astype(BF16), vh, preferred_element_type=F32))
        qn = jnp.sum(qh.astype(F32) * n_ref[h:h + 1, :], axis=1, keepdims=True)
        den = w_inter * qn + jnp.sum(A, axis=1, keepdims=True)
        hh = num / jnp.maximum(jnp.abs(den), jnp.exp(-m_t))

        m_new = m_t[L - 1:L, :]
        b_last = b_col[L - 1:L, :]
        decay = jnp.exp(b_last + m_prev - m_new)
        w_s = jnp.exp(b_last - b_col + i_col - m_new)
        kw = kh.astype(F32) * w_s
        C_ref[h] = decay * C_ref[h] + lax.dot_general(kw.astype(BF16), vh, TN_DIMS,
                                                      preferred_element_type=F32)
        n_ref[h:h + 1, :] = decay * n_ref[h:h + 1, :] + jnp.sum(kw, axis=0, keepdims=True)
        m_ref[h:h + 1, :] = jnp.broadcast_to(m_new, (1, LANES))

        mu = jnp.mean(hh, axis=1, keepdims=True)
        hc = hh - mu
        var = jnp.mean(hc * hc, axis=1, keepdims=True)
        hn = hc * lax.rsqrt(var + LN_EPS) * ng_ref[:, h * DV:(h + 1) * DV]
        og = jax.nn.sigmoid(o_ref[:, h * DV:(h + 1) * DV].astype(F32))
        out_ref[:, h * DV:(h + 1) * DV] = (og * hn).astype(out_ref.dtype)


def _mlstm_mix(proj, gates, gates_t, b_col, b_row, norm_g, B, S):
    T = B * S
    NH = MLSTM_HEADS
    DV = norm_g.shape[1] // NH
    DQK = DV // 2
    L = MLSTM_L
    nc = S // L
    qw = NH * DQK
    vw = NH * DV
    kern = functools.partial(_mlstm_kernel, L=L, NH=NH, DQK=DQK, DV=DV)
    return pl.pallas_call(
        kern,
        grid=(B, nc),
        in_specs=[pl.BlockSpec((L, qw), lambda b, c: (b * nc + c, 0)),
                  pl.BlockSpec((L, qw), lambda b, c: (b * nc + c, 1)),
                  pl.BlockSpec((L, vw), lambda b, c: (b * nc + c, 2 * qw // vw)),
                  pl.BlockSpec((L, vw), lambda b, c: (b * nc + c, 2 * qw // vw + 1)),
                  pl.BlockSpec((L, LANES), lambda b, c: (b * nc + c, 0)),
                  pl.BlockSpec((SUBLANES, L), lambda b, c: (0, b * nc + c)),
                  pl.BlockSpec((1, LANES), lambda b, c: (0, 0)),
                  pl.BlockSpec((SUBLANES, 1), lambda b, c: (0, 0)),
                  pl.BlockSpec((1, vw), lambda b, c: (0, 0))],
        out_specs=pl.BlockSpec((L, vw), lambda b, c: (b * nc + c, 0)),
        out_shape=jax.ShapeDtypeStruct((T, vw), BF16),
        scratch_shapes=[pltpu.VMEM((NH, DQK, DV), F32), pltpu.VMEM((SUBLANES, DQK), F32),
                        pltpu.VMEM((SUBLANES, LANES), F32)],
        compiler_params=_cparams(("parallel", "arbitrary")),
        name="mlstm_mix",
    )(proj, proj, proj, proj, gates, gates_t, b_col, b_row, norm_g)


def _layer_norm(z, g, b):
    mu = jnp.mean(z, axis=-1, keepdims=True)
    zc = z - mu
    var = jnp.mean(zc * zc, axis=-1, keepdims=True)
    return zc * lax.rsqrt(var + LN_EPS) * g + b


def _mix_out_router_kernel(y_ref, w_ref, x_ref, g_ref, b_ref, rwt_ref, rb_ref,
                           x1_ref, et_ref, gt_ref, pt_ref, cnt_ref, *, tm, E, K):
    y = jnp.dot(y_ref[...], w_ref[...], preferred_element_type=F32)
    x1 = _layer_norm(DEEPNORM_ALPHA * x_ref[...] + y, g_ref[...], b_ref[...])
    x1_ref[...] = x1

    logits = lax.dot_general(rwt_ref[...], x1, NT_DIMS, preferred_element_type=F32,
                             precision=lax.Precision.HIGHEST) + rb_ref[...]
    sub = lax.broadcasted_iota(I32, logits.shape, 0).astype(F32)
    vals, hits = [], []
    cur = logits
    for _ in range(K):
        mx = jnp.max(cur, axis=0, keepdims=True)
        idx = jnp.min(jnp.where(cur == mx, sub, float(E)), axis=0, keepdims=True)
        hit = sub == idx
        vals.append(mx)
        hits.append(hit)
        cur = jnp.where(hit, NEG_INF, cur)

    chosen = jnp.zeros(logits.shape, F32)
    for hit in hits:
        chosen = jnp.where(hit, 1.0, chosen)
    r_ = lax.broadcasted_iota(I32, (tm, tm), 0)
    c_ = lax.broadcasted_iota(I32, (tm, tm), 1)
    before = (r_ < c_).astype(BF16)
    pos = jnp.dot(chosen.astype(BF16), before, preferred_element_type=F32)
    cnt_ref[...] = jnp.broadcast_to(jnp.sum(chosen, axis=1, keepdims=True), cnt_ref.shape)

    ex = [jnp.exp(v - vals[0]) for v in vals]
    den = ex[0]
    for e_ in ex[1:]:
        den = den + e_
    for r in range(K):
        gt_ref[r:r + 1, :] = ex[r] / den
        et_ref[r:r + 1, :] = jnp.sum(jnp.where(hits[r], sub, 0.0), axis=0, keepdims=True).astype(I32)
        pt_ref[r:r + 1, :] = jnp.sum(jnp.where(hits[r], pos, 0.0), axis=0, keepdims=True).astype(I32)


def _mix_out_router(y, w_bf16, x, ln_g, ln_b, router_w, router_b):
    T, D = x.shape
    Kin = y.shape[1]
    E, K = MOE_EXPERTS, MOE_TOP_K
    tm = MOE_TILE
    kern = functools.partial(_mix_out_router_kernel, tm=tm, E=E, K=K)
    row = lambda i: (i, 0)
    fixed = lambda i: (0, 0)
    tok = lambda i: (0, i)
    return pl.pallas_call(
        kern,
        grid=(T // tm,),
        in_specs=[pl.BlockSpec((tm, Kin), row), pl.BlockSpec((Kin, D), fixed),
                  pl.BlockSpec((tm, D), row), pl.BlockSpec((1, D), fixed), pl.BlockSpec((1, D), fixed),
                  pl.BlockSpec((E, D), fixed), pl.BlockSpec((E, 1), fixed)],
        out_specs=[pl.BlockSpec((tm, D), row), pl.BlockSpec((K, tm), tok), pl.BlockSpec((K, tm), tok),
                   pl.BlockSpec((K, tm), tok), pl.BlockSpec((E, LANES), tok)],
        out_shape=[jax.ShapeDtypeStruct((T, D), F32), jax.ShapeDtypeStruct((K, T), I32),
                   jax.ShapeDtypeStruct((K, T), F32), jax.ShapeDtypeStruct((K, T), I32),
                   jax.ShapeDtypeStruct((E, (T // tm) * LANES), F32)],
        compiler_params=_cparams(("parallel",)),
        name="mix_out_router",
    )(y, w_bf16, x, ln_g.reshape(1, D), ln_b.reshape(1, D), router_w.T, router_b.reshape(E, 1))


def _segment(tables, tile, e, E):
    c8_ref, lo_ref, gd_ref = tables
    idx = tile * E + e
    n = pl.multiple_of(c8_ref[idx], SUBLANES)
    local = pl.multiple_of(lo_ref[idx], SUBLANES)
    glob = pl.multiple_of(gd_ref[idx], SUBLANES)
    return n, local, glob


def _dispatch_kernel(c8_ref, lo_ref, gd_ref, zs_ref, zv_ref, nv_ref, lpos_ref, x_ref, xr_ref,
                     perm_s, sorted_s, zbuf, sem, zsem, *, tm, K, E, RB, n_blocks, LR, CH):
    i = pl.program_id(0)
    tables = (c8_ref, lo_ref, gd_ref)

    def zero_copy(start):
        return pltpu.make_async_copy(zbuf, xr_ref.at[pl.ds(pl.multiple_of(start, RB), RB), :], zsem)

    @pl.when(i == 0)
    def _():
        zbuf[...] = jnp.zeros_like(zbuf)
        for e in range(E):
            @pl.when(zv_ref[e] > 0)
            def _():
                zero_copy(zs_ref[e]).start()

        def tail_start(p, carry):
            zero_copy(p * RB).start()
            return carry

        def tail_wait(p, carry):
            zero_copy(p * RB).wait()
            return carry

        lax.fori_loop(nv_ref[0], n_blocks, tail_start, 0)
        for e in range(E):
            @pl.when(zv_ref[e] > 0)
            def _():
                zero_copy(zs_ref[e]).wait()
        lax.fori_loop(nv_ref[0], n_blocks, tail_wait, 0)

    def seg_copy(tile, e):
        n, local, glob = _segment(tables, tile, e, E)
        return pltpu.make_async_copy(sorted_s.at[pl.ds(local, n), :], xr_ref.at[pl.ds(glob, n), :], sem)

    def for_segments(tile, fn):
        for e in range(E):
            @pl.when(c8_ref[tile * E + e] > 0)
            def _():
                fn(seg_copy(tile, e))

    lpos = lpos_ref[...]
    for c in range(LR // CH):
        rows = lax.broadcasted_iota(I32, (CH, tm), 0) + c * CH
        hit = jnp.zeros((CH, tm), F32)
        for r in range(K):
            hit = hit + jnp.where(rows == lpos[r:r + 1, :], 1.0, 0.0)
        perm_s[c * CH:(c + 1) * CH, :] = hit.astype(BF16)

    @pl.when(i > 0)
    def _():
        for_segments(i - 1, lambda cp: cp.wait())

    xb = x_ref[...].astype(BF16)
    for c in range(LR // CH):
        sorted_s[c * CH:(c + 1) * CH, :] = jnp.dot(perm_s[c * CH:(c + 1) * CH, :], xb,
                                                   preferred_element_type=F32)
    for_segments(i, lambda cp: cp.start())

    @pl.when(i == pl.num_programs(0) - 1)
    def _():
        for_segments(i, lambda cp: cp.wait())


def _dispatch(x1, lpos_t, seg_tables, zero_start, zero_valid, n_valid, n_rows):
    T, D = x1.shape
    K, E, RB = MOE_TOP_K, MOE_EXPERTS, ROW_BLOCK
    tm = MOE_TILE
    LR = _local_rows(tm)
    kern = functools.partial(_dispatch_kernel, tm=tm, K=K, E=E, RB=RB, n_blocks=n_rows // RB, LR=LR, CH=SORT_CHUNK)
    grid_spec = pltpu.PrefetchScalarGridSpec(
        num_scalar_prefetch=6,
        grid=(T // tm,),
        in_specs=[pl.BlockSpec((K, tm), lambda i, *_: (0, i)),
                  pl.BlockSpec((tm, D), lambda i, *_: (i, 0))],
        out_specs=pl.BlockSpec(memory_space=pl.ANY),
        scratch_shapes=[pltpu.VMEM((LR, tm), BF16), pltpu.VMEM((LR, D), F32), pltpu.VMEM((RB, D), F32),
                        pltpu.SemaphoreType.DMA, pltpu.SemaphoreType.DMA],
    )
    return pl.pallas_call(
        kern,
        grid_spec=grid_spec,
        out_shape=jax.ShapeDtypeStruct((n_rows, D), F32),
        compiler_params=pltpu.CompilerParams(dimension_semantics=("arbitrary",),
                                             vmem_limit_bytes=VMEM_LIMIT_BYTES,
                                             has_side_effects=True),
        name="moe_dispatch",
    )(*seg_tables, zero_start, zero_valid, n_valid, lpos_t, x1)


def _expert_kernel(be_ref, nv_ref, x_ref, wgu_ref, bg_ref, bl_ref, wd_ref, bd_ref, y_ref,
                   wg_s, wl_s, wd_s):
    p = pl.program_id(0)
    valid = p < nv_ref[0]
    fresh = (p == 0) | (be_ref[p] != be_ref[jnp.maximum(p - 1, 0)])
    half = MXU_DIM // 2

    @pl.when(valid & fresh)
    def _():
        r_ = lax.broadcasted_iota(I32, (MXU_DIM, MXU_DIM), 0)
        c_ = lax.broadcasted_iota(I32, (MXU_DIM, MXU_DIM), 1)
        src = jnp.where(c_ < half, 2 * c_, 2 * (c_ - half) + 1)
        perm = jnp.where(r_ == src, 1.0, 0.0).astype(BF16)
        for cb in range(wgu_ref.shape[1] // MXU_DIM):
            blk = wgu_ref[:, cb * MXU_DIM:(cb + 1) * MXU_DIM].astype(BF16)
            sep = jnp.dot(blk, perm, preferred_element_type=F32).astype(BF16)
            wg_s[:, cb * half:(cb + 1) * half] = sep[:, :half]
            wl_s[:, cb * half:(cb + 1) * half] = sep[:, half:]
        wd_s[...] = wd_ref[...].astype(BF16)

    @pl.when(valid)
    def _():
        x = x_ref[...].astype(BF16)
        hg = jnp.dot(x, wg_s[...], preferred_element_type=F32) + bg_ref[...]
        hl = jnp.dot(x, wl_s[...], preferred_element_type=F32) + bl_ref[...]
        xg = jnp.minimum(hg, SWIGLU_LIMIT)
        xl = jnp.clip(hl, -SWIGLU_LIMIT, SWIGLU_LIMIT)
        act = xg * jax.nn.sigmoid(SWIGLU_ALPHA * xg) * (xl + 1.0)
        y_ref[...] = jnp.dot(act.astype(BF16), wd_s[...], preferred_element_type=F32) + bd_ref[...]

    @pl.when(jnp.logical_not(valid))
    def _():
        y_ref[...] = jnp.zeros_like(y_ref)


def _experts(x_rows, blk_e, n_valid, layer, w_gate_up, b_glu, b_lin, w_down, b_down):
    n_rows, D = x_rows.shape
    _, E, _, F2 = w_gate_up.shape
    F = F2 // 2
    RB = ROW_BLOCK
    n_blocks = n_rows // RB
    xblk = lambda p, be, nv: (jnp.minimum(p, nv[0] - 1), 0)
    yblk = lambda p, be, nv: (p, 0)
    wsel = lambda p, be, nv: (be[p], 0, 0)
    wsel_l = lambda p, be, nv: (layer, be[p], 0, 0)
    grid_spec = pltpu.PrefetchScalarGridSpec(
        num_scalar_prefetch=2,
        grid=(n_blocks,),
        in_specs=[pl.BlockSpec((RB, D), xblk),
                  pl.BlockSpec((None, None, D, F2), wsel_l),
                  pl.BlockSpec((None, 1, F), wsel), pl.BlockSpec((None, 1, F), wsel),
                  pl.BlockSpec((None, None, F, D), wsel_l), pl.BlockSpec((None, 1, D), wsel)],
        out_specs=pl.BlockSpec((RB, D), yblk),
        scratch_shapes=[pltpu.VMEM((D, F), BF16), pltpu.VMEM((D, F), BF16), pltpu.VMEM((F, D), BF16)],
    )
    return pl.pallas_call(
        _expert_kernel,
        grid_spec=grid_spec,
        out_shape=jax.ShapeDtypeStruct((n_rows, D), F32),
        compiler_params=_cparams(("arbitrary",), EXPERT_VMEM_LIMIT_BYTES),
        name="moe_experts",
    )(blk_e, n_valid, x_rows, w_gate_up, b_glu, b_lin, w_down, b_down)


def _combine_kernel(c8_ref, lo_ref, gd_ref, y_ref, lpos_ref, gate_ref, x_ref, g_ref, b_ref, o_ref,
                    ys, sem, *, tm, K, E, LR, CH):
    i = pl.program_id(0)
    n_tiles = pl.num_programs(0)
    slot = i % 2
    tables = (c8_ref, lo_ref, gd_ref)

    def seg_copy(tile, e, s):
        n, local, glob = _segment(tables, tile, e, E)
        return pltpu.make_async_copy(y_ref.at[pl.ds(glob, n), :], ys.at[s, pl.ds(local, n), :], sem.at[s])

    def for_segments(tile, s, fn):
        for e in range(E):
            @pl.when(c8_ref[tile * E + e] > 0)
            def _():
                fn(seg_copy(tile, e, s))

    @pl.when(i == 0)
    def _():
        ys[...] = jnp.zeros_like(ys)
        for_segments(0, 0, lambda cp: cp.start())

    @pl.when(i + 1 < n_tiles)
    def _():
        for_segments(i + 1, 1 - slot, lambda cp: cp.start())

    for_segments(i, slot, lambda cp: cp.wait())

    lpos = lpos_ref[...]
    gate = gate_ref[...]
    f = jnp.zeros((tm, x_ref.shape[1]), F32)
    for c in range(LR // CH):
        cols = lax.broadcasted_iota(I32, (tm, CH), 1) + c * CH
        w = jnp.zeros((tm, CH), F32)
        for r in range(K):
            w = w + jnp.where(cols == lpos[:, r:r + 1], gate[:, r:r + 1], 0.0)
        f = f + jnp.dot(w.astype(BF16), ys[slot, c * CH:(c + 1) * CH, :].astype(BF16),
                        preferred_element_type=F32)
    o_ref[...] = _layer_norm(DEEPNORM_ALPHA * x_ref[...] + f, g_ref[...], b_ref[...])


def _combine(y_rows, lpos_c, seg_tables, x1, gates_c, ln_g, ln_b):
    T, D = x1.shape
    K, E = MOE_TOP_K, MOE_EXPERTS
    tm = MOE_TILE
    LR = _local_rows(tm)
    kern = functools.partial(_combine_kernel, tm=tm, K=K, E=E, LR=LR, CH=SORT_CHUNK)
    grid_spec = pltpu.PrefetchScalarGridSpec(
        num_scalar_prefetch=3,
        grid=(T // tm,),
        in_specs=[pl.BlockSpec(memory_space=pl.ANY),
                  pl.BlockSpec((tm, K), lambda i, *_: (i, 0)),
                  pl.BlockSpec((tm, K), lambda i, *_: (i, 0)),
                  pl.BlockSpec((tm, D), lambda i, *_: (i, 0)),
                  pl.BlockSpec((1, D), lambda i, *_: (0, 0)), pl.BlockSpec((1, D), lambda i, *_: (0, 0))],
        out_specs=pl.BlockSpec((tm, D), lambda i, *_: (i, 0)),
        scratch_shapes=[pltpu.VMEM((2, LR, D), F32), pltpu.SemaphoreType.DMA((2,))],
    )
    return pl.pallas_call(
        kern,
        grid_spec=grid_spec,
        out_shape=jax.ShapeDtypeStruct((T, D), F32),
        compiler_params=_cparams(("arbitrary",)),
        name="moe_combine",
    )(*seg_tables, y_rows, lpos_c, gates_c, x1, ln_g.reshape(1, D), ln_b.reshape(1, D))


def _moe(x1, e_t, g_t, p_t, cnt, layer, w_gate_up, b_gate_up, w_down, b_down, ln_g, ln_b):
    T, D = x1.shape
    E, K, RB = MOE_EXPERTS, MOE_TOP_K, ROW_BLOCK
    tm = MOE_TILE
    n_tiles = T // tm
    n_blocks = -(-(T * K + n_tiles * E * (SUBLANES - 1) + E * (RB - 1)) // RB)
    n_rows = n_blocks * RB

    seg_len = (cnt[:, ::LANES].T.astype(I32) + SUBLANES - 1) // SUBLANES * SUBLANES
    local_off = jnp.cumsum(seg_len, axis=1) - seg_len
    group_rows = jnp.sum(seg_len, axis=0)
    padded = (group_rows + RB - 1) // RB * RB
    pad_end = jnp.cumsum(padded)
    pad_start = pad_end - padded
    global_off = pad_start[None, :] + jnp.cumsum(seg_len, axis=0) - seg_len
    seg_tables = (seg_len.reshape(-1), local_off.reshape(-1).astype(I32), global_off.reshape(-1).astype(I32))

    experts = jnp.arange(E, dtype=I32)
    e_tiles = e_t.reshape(K, n_tiles, tm)
    off_sel = jnp.sum(jnp.where(e_tiles[..., None] == experts, local_off[None, :, None, :], 0), axis=-1)
    lpos_t = (off_sel.reshape(K, T) + p_t).astype(I32)

    n_valid = (pad_end[-1] // RB).astype(I32)
    blk_ids = jnp.minimum(jnp.arange(n_blocks, dtype=I32), n_valid - 1)
    blk_e = jnp.minimum(jnp.sum((pad_end[None, :] <= blk_ids[:, None] * RB).astype(I32), axis=1), E - 1)
    zero_start = jnp.maximum(pad_end - RB, 0).astype(I32)
    zero_valid = (group_rows > 0).astype(I32)
    n_valid = n_valid.reshape(1)

    x_rows = _dispatch(x1, lpos_t, seg_tables, zero_start, zero_valid, n_valid, n_rows)
    y_rows = _experts(x_rows, blk_e, n_valid, layer, w_gate_up, b_gate_up[:, None, 0::2],
                      b_gate_up[:, None, 1::2], w_down, b_down[:, None, :])
    return _combine(y_rows, lpos_t.T, seg_tables, x1, g_t.T, ln_g, ln_b)


def kernel(x, positions, attn_w_qkv, attn_w_o, mlstm_w_in, mlstm_b_gates, mlstm_norm_g, mlstm_w_out,
           ln_mix_g, ln_mix_b, ln_ffn_g, ln_ffn_b, router_w, router_b,
           w_gate_up, b_gate_up, w_down, b_down):
    B, S, D = x.shape
    T = B * S
    xt = x.reshape(T, D)
    NH = MLSTM_HEADS

    for layer in range(DEPTH):
        slot = layer // 2
        if layer % 2 == 0:
            hd = attn_w_o.shape[1]
            qkv = _qkv_rope(xt, positions.reshape(T, 1), attn_w_qkv[slot].astype(BF16), hd)
            mixed = _moba_attention(qkv, B, S, hd)
            w_out = attn_w_o[slot].astype(BF16)
        else:
            vw = mlstm_w_out.shape[1]
            qw = vw // 2
            w_in = mlstm_w_in[slot]
            dqk = qw // NH
            col_scale = jnp.concatenate([jnp.ones((qw,), F32), jnp.full((qw,), dqk ** -0.5, F32),
                                         jnp.ones((2 * vw,), F32)])
            w_main = (w_in[:, :2 * qw + 2 * vw] * col_scale).astype(BF16)
            w_gate = jnp.pad(w_in[:, 2 * qw + 2 * vw:], ((0, 0), (0, LANES - 2 * NH))).astype(BF16)
            proj, gates = _mlstm_proj(xt, w_main, w_gate)
            bg = mlstm_b_gates[slot]
            b_col = jnp.pad(bg, (0, LANES - 2 * NH)).reshape(1, LANES)
            b_row = bg.reshape(2 * NH, 1)
            mixed = _mlstm_mix(proj, gates, gates[:, :2 * NH].T, b_col, b_row,
                               mlstm_norm_g[slot].reshape(1, vw), B, S)
            w_out = mlstm_w_out[slot].astype(BF16)
        x1, e_t, g_t, p_t, cnt = _mix_out_router(mixed, w_out, xt, ln_mix_g[layer], ln_mix_b[layer],
                                                 router_w[layer], router_b[layer])
        xt = _moe(x1, e_t, g_t, p_t, cnt, layer, w_gate_up, b_gate_up[layer], w_down, b_down[layer],
                  ln_ffn_g[layer], ln_ffn_b[layer])
    return xt.reshape(B, S, D)
```

```python
import functools

import jax
import jax.numpy as jnp
from jax import lax
from jax.experimental import pallas as pl
from jax.experimental.pallas import tpu as pltpu

F32 = jnp.float32
BF16 = jnp.bfloat16
I32 = jnp.int32

DEPTH = 2
ATTN_HEAD_DIM = 64
ROT_DIM = ATTN_HEAD_DIM // 4
ROPE_THETA = 500000.0
MOBA_BLOCK = 256
MOBA_TOP_K = 3
MLSTM_HEADS = 4
MOE_EXPERTS = 32
MOE_TOP_K = 4
SWIGLU_LIMIT = 7.0
SWIGLU_ALPHA = 1.702
DEEPNORM_ALPHA = (2 * DEPTH) ** 0.25
LN_EPS = 1e-5

LANES = 128
SUBLANES = 8
BF16_SUBLANES = 16
MXU_DIM = 256
VMEM_LIMIT_BYTES = 48 * 1024 * 1024
EXPERT_VMEM_LIMIT_BYTES = 58 * 1024 * 1024

PROJ_TM = 512
PROJ_TN = 512
MOE_TILE = 512
MLSTM_L = 256
ROW_BLOCK = 512
SORT_CHUNK = 256

NT_DIMS = (((1,), (1,)), ((), ()))
TN_DIMS = (((0,), (0,)), ((), ()))
NEG_INF = float("-inf")
MASKED = -1e30
LOG2_E = 1.4426950408889634


def _cparams(sem, vmem=VMEM_LIMIT_BYTES):
    return pltpu.CompilerParams(dimension_semantics=sem, vmem_limit_bytes=vmem)


def _local_rows(tm):
    rows = tm * MOE_TOP_K + MOE_EXPERTS * SUBLANES
    return -(-rows // SORT_CHUNK) * SORT_CHUNK


def _qkv_rope_kernel(pos_ref, invf_ref, x_ref, w_ref, o_ref, *, n_rope_chunks, n_q_chunks, tn):
    ang = pos_ref[...].astype(F32) * invf_ref[...]
    d = lax.broadcasted_iota(I32, ang.shape, 1) & (ATTN_HEAD_DIM - 1)
    cos = jnp.cos(ang)
    sin = jnp.sin(ang)
    half = ROT_DIM // 2
    c_tab = jnp.where(d < ROT_DIM, cos, 1.0)
    s_up = jnp.where(d < half, -sin, 0.0)
    s_dn = jnp.where((d >= half) & (d < ROT_DIM), sin, 0.0)

    xb = x_ref[...].astype(BF16)
    for c in range(w_ref.shape[1] // tn):
        acc = jnp.dot(xb, w_ref[:, c * tn:(c + 1) * tn], preferred_element_type=F32)
        if c >= n_rope_chunks:
            o_ref[:, c * tn:(c + 1) * tn] = acc.astype(o_ref.dtype)
            continue
        scale = ATTN_HEAD_DIM ** -0.5 * LOG2_E if c < n_q_chunks else 1.0
        for s in range(tn // LANES):
            blk = acc[:, s * LANES:(s + 1) * LANES]
            r = (blk * c_tab + pltpu.roll(blk, LANES - half, 1) * s_up + pltpu.roll(blk, half, 1) * s_dn)
            lo = c * tn + s * LANES
            o_ref[:, lo:lo + LANES] = (r * scale).astype(o_ref.dtype)


def _qkv_rope(x, pos, w_bf16, hd):
    T, D = x.shape
    N = w_bf16.shape[1]
    tm, tn = PROJ_TM, PROJ_TN
    inv_freq = ROPE_THETA ** (-jnp.arange(0, ROT_DIM, 2, dtype=F32) / ROT_DIM)
    lane_d = jnp.arange(LANES) % ATTN_HEAD_DIM
    invf = jnp.where(lane_d < ROT_DIM, inv_freq[lane_d % (ROT_DIM // 2)], 0.0).reshape(1, LANES).astype(F32)
    kern = functools.partial(_qkv_rope_kernel, n_rope_chunks=2 * hd // tn, n_q_chunks=hd // tn, tn=tn)
    return pl.pallas_call(
        kern,
        grid=(T // tm,),
        in_specs=[pl.BlockSpec((tm, 1), lambda i: (i, 0)),
                  pl.BlockSpec((1, LANES), lambda i: (0, 0)),
                  pl.BlockSpec((tm, D), lambda i: (i, 0)),
                  pl.BlockSpec((D, N), lambda i: (0, 0))],
        out_specs=pl.BlockSpec((tm, N), lambda i: (i, 0)),
        out_shape=jax.ShapeDtypeStruct((T, N), BF16),
        compiler_params=_cparams(("parallel",)),
        name="qkv_rope",
    )(pos, invf, x, w_bf16)


def _moba_kernel(q_ref, k_ref, v_ref, o_ref, kmh_ref, kml_ref, ka_ref, vt_ref, q2t_ref,
                 m_ref, acc_ref, *score_bufs, nb, blk):
    s_refs, smax_refs = score_bufs[:4], score_bufs[4:]
    i = pl.program_id(2)
    hd = ATTN_HEAD_DIM
    nbp = kmh_ref.shape[0]
    va = vt_ref.shape[2]

    @pl.when(i == 0)
    def _():
        rows = [jnp.mean(k_ref[j * blk:(j + 1) * blk, :].astype(F32), axis=0, keepdims=True)
                for j in range(nb)]
        if nbp > nb:
            rows.append(jnp.zeros((nbp - nb, LANES), F32))
        km = jnp.concatenate(rows, axis=0)
        hi = km.astype(BF16)
        kmh_ref[...] = hi
        kml_ref[...] = (km - hi.astype(F32)).astype(BF16)
        lane = lax.broadcasted_iota(I32, (blk, LANES), 1)
        sub_v = lax.broadcasted_iota(I32, (va - hd, blk), 0)
        ones_row = jnp.where(sub_v == 0, 1.0, 0.0).astype(BF16)
        for j in range(nb):
            ka_ref[j * blk:(j + 1) * blk, :LANES] = k_ref[j * blk:(j + 1) * blk, :]
            ka_ref[j * blk:(j + 1) * blk, LANES:] = jnp.where(lane == j, 1.0, 0.0).astype(BF16)
            vt = v_ref[j * blk:(j + 1) * blk, :].astype(F32).T.astype(BF16)
            vt_ref[j, 0] = jnp.concatenate([vt[:hd], ones_row], axis=0)
            vt_ref[j, 1] = jnp.concatenate([vt[hd:], ones_row], axis=0)
        q2t_ref[LANES + nbp:, :] = jnp.zeros((LANES - nbp, 2 * blk), BF16)

    qt = q_ref[...].astype(F32).T
    sub_q = lax.broadcasted_iota(I32, qt.shape, 0)
    zero = jnp.zeros_like(qt)
    q2t = jnp.concatenate([jnp.where(sub_q < hd, qt, zero), jnp.where(sub_q >= hd, qt, zero)],
                          axis=1).astype(BF16)
    q2t_ref[:LANES, :] = q2t

    gate = (jnp.dot(kmh_ref[...], q2t, preferred_element_type=F32)
            + jnp.dot(kml_ref[...], q2t, preferred_element_type=F32))
    sub = lax.broadcasted_iota(I32, gate.shape, 0)
    sub_f = sub.astype(F32)
    g = jnp.where(sub < i, gate, NEG_INF)
    sel = jnp.zeros(gate.shape, F32)
    for _ in range(MOBA_TOP_K):
        mx = jnp.max(g, axis=0, keepdims=True)
        idx = jnp.min(jnp.where(g == mx, sub_f, float(nbp)), axis=0, keepdims=True)
        hit = (sub_f == idx) & (mx > NEG_INF)
        sel = jnp.where(hit, 1.0, sel)
        g = jnp.where(hit, NEG_INF, g)
    q2t_ref[LANES:LANES + nbp, :] = jnp.where(sel > 0.0, 0.0, MASKED).astype(BF16)

    def produce(pos, buf):
        s_ref, smax_ref = buf
        if isinstance(pos, int) and pos == 0:
            ki = k_ref[pl.ds(pl.multiple_of(i * blk, blk), blk), :]
            s = jnp.dot(ki, q2t, preferred_element_type=F32)
            key = lax.broadcasted_iota(I32, s.shape, 0)
            qry = lax.broadcasted_iota(I32, s.shape, 1) & (blk - 1)
            s = jnp.where(key <= qry, s, MASKED)
        else:
            j = jnp.minimum(pos - 1, i)
            kj = ka_ref[pl.ds(pl.multiple_of(j * blk, blk), blk), :]
            s = jnp.dot(kj, q2t_ref[...], preferred_element_type=F32)
        s_ref[...] = s
        smax_ref[...] = jnp.max(s, axis=0, keepdims=True)

    def consume(pos, buf):
        s_ref, smax_ref = buf
        j = jnp.where(pos == 0, i, jnp.minimum(pos - 1, i))
        m_old = m_ref[...]
        m_new = jnp.maximum(m_old, smax_ref[...])
        a = jnp.exp2(m_old - m_new)
        pb = jnp.exp2(s_ref[...] - m_new).astype(BF16)
        upd = jnp.concatenate([jnp.dot(vt_ref[j, 0], pb[:, :blk], preferred_element_type=F32),
                               jnp.dot(vt_ref[j, 1], pb[:, blk:], preferred_element_type=F32)], axis=1)
        acc_ref[...] = a * acc_ref[...] + upd
        m_ref[...] = m_new

    m_ref[...] = jnp.full(m_ref.shape, MASKED, F32)
    acc_ref[...] = jnp.zeros_like(acc_ref)
    buf_a, buf_b, buf_c, buf_d = [(s_refs[n], smax_refs[n]) for n in range(4)]
    produce(0, buf_a)
    produce(1, buf_b)

    def body(t, carry):
        base = 4 * t
        produce(base + 2, buf_c)
        produce(base + 3, buf_d)
        consume(base, buf_a)
        consume(base + 1, buf_b)
        produce(base + 4, buf_a)
        produce(base + 5, buf_b)
        consume(base + 2, buf_c)
        consume(base + 3, buf_d)
        return carry

    lax.fori_loop(0, i // 4 + 1, body, 0)
    acc = acc_ref[...]
    out_t = acc[:hd] / acc[hd:hd + 1]
    o_ref[...] = jnp.concatenate([out_t[:, :blk], out_t[:, blk:]], axis=0).T.astype(o_ref.dtype)


def _moba_attention(qkv, B, S, hd):
    T = B * S
    blk = MOBA_BLOCK
    nb = S // blk
    nbp = -(-nb // BF16_SUBLANES) * BF16_SUBLANES
    va = ATTN_HEAD_DIM + BF16_SUBLANES
    n_pairs = hd // LANES
    kern = functools.partial(_moba_kernel, nb=nb, blk=blk)
    return pl.pallas_call(
        kern,
        grid=(B, n_pairs, nb),
        in_specs=[pl.BlockSpec((blk, LANES), lambda b, h, i: (b * nb + i, h)),
                  pl.BlockSpec((S, LANES), lambda b, h, i: (b, n_pairs + h)),
                  pl.BlockSpec((S, LANES), lambda b, h, i: (b, 2 * n_pairs + h))],
        out_specs=pl.BlockSpec((blk, LANES), lambda b, h, i: (b * nb + i, h)),
        out_shape=jax.ShapeDtypeStruct((T, hd), BF16),
        scratch_shapes=[pltpu.VMEM((nbp, LANES), BF16), pltpu.VMEM((nbp, LANES), BF16),
                        pltpu.VMEM((S, 2 * LANES), BF16), pltpu.VMEM((nb, 2, va, blk), BF16),
                        pltpu.VMEM((2 * LANES, 2 * blk), BF16),
                        pltpu.VMEM((1, 2 * blk), F32), pltpu.VMEM((va, 2 * blk), F32)]
                       + [pltpu.VMEM((blk, 2 * blk), F32)] * 4 + [pltpu.VMEM((1, 2 * blk), F32)] * 4,
        compiler_params=_cparams(("parallel", "parallel", "arbitrary")),
        name="moba_attention",
    )(qkv, qkv, qkv)


def _mlstm_proj_kernel(x_ref, w_ref, wg_ref, o_ref, g_ref, *, tn):
    xb = x_ref[...].astype(BF16)
    for c in range(w_ref.shape[1] // tn):
        o_ref[:, c * tn:(c + 1) * tn] = jnp.dot(xb, w_ref[:, c * tn:(c + 1) * tn],
                                                preferred_element_type=F32).astype(o_ref.dtype)
    g_ref[...] = jnp.dot(xb, wg_ref[...], preferred_element_type=F32)


def _mlstm_proj(x, w_main, w_gate):
    T, D = x.shape
    N = w_main.shape[1]
    tm = PROJ_TM
    return pl.pallas_call(
        functools.partial(_mlstm_proj_kernel, tn=PROJ_TN),
        grid=(T // tm,),
        in_specs=[pl.BlockSpec((tm, D), lambda i: (i, 0)),
                  pl.BlockSpec((D, N), lambda i: (0, 0)),
                  pl.BlockSpec((D, LANES), lambda i: (0, 0))],
        out_specs=[pl.BlockSpec((tm, N), lambda i: (i, 0)),
                   pl.BlockSpec((tm, LANES), lambda i: (i, 0))],
        out_shape=[jax.ShapeDtypeStruct((T, N), BF16), jax.ShapeDtypeStruct((T, LANES), F32)],
        compiler_params=_cparams(("parallel",)),
        name="mlstm_proj",
    )(x, w_main, w_gate)


def _log_sigmoid(x):
    return jnp.minimum(x, 0.0) - jnp.log(1.0 + jnp.exp(-jnp.abs(x)))


def _mlstm_kernel(q_ref, k_ref, v_ref, o_ref, gc_ref, gr_ref, bc_ref, br_ref, ng_ref, out_ref,
                  C_ref, n_ref, m_ref, *, L, NH, DQK, DV):
    c = pl.program_id(1)

    @pl.when(c == 0)
    def _():
        C_ref[...] = jnp.zeros_like(C_ref)
        n_ref[...] = jnp.zeros_like(n_ref)
        m_ref[...] = jnp.zeros_like(m_ref)

    gcol = gc_ref[...] + bc_ref[...]
    grow = gr_ref[...] + br_ref[...]
    lane = lax.broadcasted_iota(I32, gcol.shape, 1)
    lf_col = jnp.where((lane >= NH) & (lane < 2 * NH), _log_sigmoid(gcol), 0.0)
    lf_row = _log_sigmoid(grow)
    row = lax.broadcasted_iota(I32, (L, L), 0)
    col = lax.broadcasted_iota(I32, (L, L), 1)
    causal = col <= row
    tri = causal.astype(F32)
    hp = lax.Precision.HIGHEST
    b_cols = jnp.dot(tri, lf_col, preferred_element_type=F32, precision=hp)
    b_rows = lax.dot_general(lf_row, tri, NT_DIMS, preferred_element_type=F32, precision=hp)

    for h in range(NH):
        i_row = grow[h:h + 1, :]
        i_col = gcol[:, h:h + 1]
        b_row = b_rows[NH + h:NH + h + 1, :]
        b_col = b_cols[:, NH + h:NH + h + 1]
        m_prev = m_ref[h:h + 1, 0:1]
        qh = q_ref[:, h * DQK:(h + 1) * DQK]
        kh = k_ref[:, h * DQK:(h + 1) * DQK]
        vh = v_ref[:, h * DV:(h + 1) * DV]

        D = jnp.where(causal, b_col - b_row + i_row, NEG_INF)
        g = b_col + m_prev
        m_t = jnp.maximum(g, jnp.max(D, axis=1, keepdims=True))
        w_inter = jnp.exp(g - m_t)
        qk = lax.dot_general(qh, kh, NT_DIMS, preferred_element_type=F32)
        A = jnp.exp(D - m_t) * qk
        num = (w_inter * jnp.dot(qh, C_ref[h].astype(BF16), preferred_element_type=F32)
               + jnp.dot(A.astype(BF16), vh, preferred_element_type=F32))
        qn = jnp.sum(qh.astype(F32) * n_ref[h:h + 1, :], axis=1, keepdims=True)
        den = w_inter * qn + jnp.sum(A, axis=1, keepdims=True)
        hh = num / jnp.maximum(jnp.abs(den), jnp.exp(-m_t))

        m_new = m_t[L - 1:L, :]
        b_last = b_col[L - 1:L, :]
        decay = jnp.exp(b_last + m_prev - m_new)
        w_s = jnp.exp(b_last - b_col + i_col - m_new)
        kw = kh.astype(F32) * w_s
        C_ref[h] = decay * C_ref[h] + lax.dot_general(kw.astype(BF16), vh, TN_DIMS,
                                                      preferred_element_type=F32)
        n_ref[h:h + 1, :] = decay * n_ref[h:h + 1, :] + jnp.sum(kw, axis=0, keepdims=True)
        m_ref[h:h + 1, :] = jnp.broadcast_to(m_new, (1, LANES))

        mu = jnp.mean(hh, axis=1, keepdims=True)
        hc = hh - mu
        var = jnp.mean(hc * hc, axis=1, keepdims=True)
        hn = hc * lax.rsqrt(var + LN_EPS) * ng_ref[:, h * DV:(h + 1) * DV]
        og = jax.nn.sigmoid(o_ref[:, h * DV:(h + 1) * DV].astype(F32))
        out_ref[:, h * DV:(h + 1) * DV] = (og * hn).astype(out_ref.dtype)


def _mlstm_mix(proj, gates, gates_t, b_col, b_row, norm_g, B, S):
    T = B * S
    NH = MLSTM_HEADS
    DV = norm_g.shape[1] // NH
    DQK = DV // 2
    L = MLSTM_L
    nc = S // L
    qw = NH * DQK
    vw = NH * DV
    kern = functools.partial(_mlstm_kernel, L=L, NH=NH, DQK=DQK, DV=DV)
    return pl.pallas_call(
        kern,
        grid=(B, nc),
        in_specs=[pl.BlockSpec((L, qw), lambda b, c: (b * nc + c, 0)),
                  pl.BlockSpec((L, qw), lambda b, c: (b * nc + c, 1)),
                  pl.BlockSpec((L, vw), lambda b, c: (b * nc + c, 2 * qw // vw)),
                  pl.BlockSpec((L, vw), lambda b, c: (b * nc + c, 2 * qw // vw + 1)),
                  pl.BlockSpec((L, LANES), lambda b, c: (b * nc + c, 0)),
                  pl.BlockSpec((SUBLANES, L), lambda b, c: (0, b * nc + c)),
                  pl.BlockSpec((1, LANES), lambda b, c: (0, 0)),
                  pl.BlockSpec((SUBLANES, 1), lambda b, c: (0, 0)),
                  pl.BlockSpec((1, vw), lambda b, c: (0, 0))],
        out_specs=pl.BlockSpec((L, vw), lambda b, c: (b * nc + c, 0)),
        out_shape=jax.ShapeDtypeStruct((T, vw), BF16),
        scratch_shapes=[pltpu.VMEM((NH, DQK, DV), F32), pltpu.VMEM((SUBLANES, DQK), F32),
                        pltpu.VMEM((SUBLANES, LANES), F32)],
        compiler_params=_cparams(("parallel", "arbitrary")),
        name="mlstm_mix",
    )(proj, proj, proj, proj, gates, gates_t, b_col, b_row, norm_g)


def _layer_norm(z, g, b):
    mu = jnp.mean(z, axis=-1, keepdims=True)
    zc = z - mu
    var = jnp.mean(zc * zc, axis=-1, keepdims=True)
    return zc * lax.rsqrt(var + LN_EPS) * g + b


def _mix_out_router_kernel(y_ref, w_ref, x_ref, g_ref, b_ref, rwt_ref, rb_ref,
                           x1_ref, et_ref, gt_ref, pt_ref, cnt_ref, *, tm, E, K):
    y = jnp.dot(y_ref[...], w_ref[...], preferred_element_type=F32)
    x1 = _layer_norm(DEEPNORM_ALPHA * x_ref[...] + y, g_ref[...], b_ref[...])
    x1_ref[...] = x1

    logits = lax.dot_general(rwt_ref[...], x1, NT_DIMS, preferred_element_type=F32,
                             precision=lax.Precision.HIGHEST) + rb_ref[...]
    sub = lax.broadcasted_iota(I32, logits.shape, 0).astype(F32)
    vals, hits = [], []
    cur = logits
    for _ in range(K):
        mx = jnp.max(cur, axis=0, keepdims=True)
        idx = jnp.min(jnp.where(cur == mx, sub, float(E)), axis=0, keepdims=True)
        hit = sub == idx
        vals.append(mx)
        hits.append(hit)
        cur = jnp.where(hit, NEG_INF, cur)

    chosen = jnp.zeros(logits.shape, F32)
    for hit in hits:
        chosen = jnp.where(hit, 1.0, chosen)
    r_ = lax.broadcasted_iota(I32, (tm, tm), 0)
    c_ = lax.broadcasted_iota(I32, (tm, tm), 1)
    before = (r_ < c_).astype(BF16)
    pos = jnp.dot(chosen.astype(BF16), before, preferred_element_type=F32)
    cnt_ref[...] = jnp.broadcast_to(jnp.sum(chosen, axis=1, keepdims=True), cnt_ref.shape)

    ex = [jnp.exp(v - vals[0]) for v in vals]
    den = ex[0]
    for e_ in ex[1:]:
        den = den + e_
    for r in range(K):
        gt_ref[r:r + 1, :] = ex[r] / den
        et_ref[r:r + 1, :] = jnp.sum(jnp.where(hits[r], sub, 0.0), axis=0, keepdims=True).astype(I32)
        pt_ref[r:r + 1, :] = jnp.sum(jnp.where(hits[r], pos, 0.0), axis=0, keepdims=True).astype(I32)


def _mix_out_router(y, w_bf16, x, ln_g, ln_b, router_w, router_b):
    T, D = x.shape
    Kin = y.shape[1]
    E, K = MOE_EXPERTS, MOE_TOP_K
    tm = MOE_TILE
    kern = functools.partial(_mix_out_router_kernel, tm=tm, E=E, K=K)
    row = lambda i: (i, 0)
    fixed = lambda i: (0, 0)
    tok = lambda i: (0, i)
    return pl.pallas_call(
        kern,
        grid=(T // tm,),
        in_specs=[pl.BlockSpec((tm, Kin), row), pl.BlockSpec((Kin, D), fixed),
                  pl.BlockSpec((tm, D), row), pl.BlockSpec((1, D), fixed), pl.BlockSpec((1, D), fixed),
                  pl.BlockSpec((E, D), fixed), pl.BlockSpec((E, 1), fixed)],
        out_specs=[pl.BlockSpec((tm, D), row), pl.BlockSpec((K, tm), tok), pl.BlockSpec((K, tm), tok),
                   pl.BlockSpec((K, tm), tok), pl.BlockSpec((E, LANES), tok)],
        out_shape=[jax.ShapeDtypeStruct((T, D), F32), jax.ShapeDtypeStruct((K, T), I32),
                   jax.ShapeDtypeStruct((K, T), F32), jax.ShapeDtypeStruct((K, T), I32),
                   jax.ShapeDtypeStruct((E, (T // tm) * LANES), F32)],
        compiler_params=_cparams(("parallel",)),
        name="mix_out_router",
    )(y, w_bf16, x, ln_g.reshape(1, D), ln_b.reshape(1, D), router_w.T, router_b.reshape(E, 1))


def _segment(tables, tile, e, E):
    c8_ref, lo_ref, gd_ref = tables
    idx = tile * E + e
    n = pl.multiple_of(c8_ref[idx], SUBLANES)
    local = pl.multiple_of(lo_ref[idx], SUBLANES)
    glob = pl.multiple_of(gd_ref[idx], SUBLANES)
    return n, local, glob


def _dispatch_kernel(c8_ref, lo_ref, gd_ref, zs_ref, zv_ref, nv_ref, lpos_ref, x_ref, xr_ref,
                     perm_s, sorted_s, zbuf, sem, zsem, *, tm, K, E, RB, n_blocks, LR, CH):
    i = pl.program_id(0)
    tables = (c8_ref, lo_ref, gd_ref)

    def zero_copy(start):
        return pltpu.make_async_copy(zbuf, xr_ref.at[pl.ds(pl.multiple_of(start, RB), RB), :], zsem)

    @pl.when(i == 0)
    def _():
        zbuf[...] = jnp.zeros_like(zbuf)
        for e in range(E):
            @pl.when(zv_ref[e] > 0)
            def _():
                zero_copy(zs_ref[e]).start()

        def tail_start(p, carry):
            zero_copy(p * RB).start()
            return carry

        def tail_wait(p, carry):
            zero_copy(p * RB).wait()
            return carry

        lax.fori_loop(nv_ref[0], n_blocks, tail_start, 0)
        for e in range(E):
            @pl.when(zv_ref[e] > 0)
            def _():
                zero_copy(zs_ref[e]).wait()
        lax.fori_loop(nv_ref[0], n_blocks, tail_wait, 0)

    def seg_copy(tile, e):
        n, local, glob = _segment(tables, tile, e, E)
        return pltpu.make_async_copy(sorted_s.at[pl.ds(local, n), :], xr_ref.at[pl.ds(glob, n), :], sem)

    def for_segments(tile, fn):
        for e in range(E):
            @pl.when(c8_ref[tile * E + e] > 0)
            def _():
                fn(seg_copy(tile, e))

    lpos = lpos_ref[...]
    for c in range(LR // CH):
        rows = lax.broadcasted_iota(I32, (CH, tm), 0) + c * CH
        hit = jnp.zeros((CH, tm), F32)
        for r in range(K):
            hit = hit + jnp.where(rows == lpos[r:r + 1, :], 1.0, 0.0)
        perm_s[c * CH:(c + 1) * CH, :] = hit.astype(BF16)

    @pl.when(i > 0)
    def _():
        for_segments(i - 1, lambda cp: cp.wait())

    xb = x_ref[...].astype(BF16)
    for c in range(LR // CH):
        sorted_s[c * CH:(c + 1) * CH, :] = jnp.dot(perm_s[c * CH:(c + 1) * CH, :], xb,
                                                   preferred_element_type=F32)
    for_segments(i, lambda cp: cp.start())

    @pl.when(i == pl.num_programs(0) - 1)
    def _():
        for_segments(i, lambda cp: cp.wait())


def _dispatch(x1, lpos_t, seg_tables, zero_start, zero_valid, n_valid, n_rows):
    T, D = x1.shape
    K, E, RB = MOE_TOP_K, MOE_EXPERTS, ROW_BLOCK
    tm = MOE_TILE
    LR = _local_rows(tm)
    kern = functools.partial(_dispatch_kernel, tm=tm, K=K, E=E, RB=RB, n_blocks=n_rows // RB, LR=LR, CH=SORT_CHUNK)
    grid_spec = pltpu.PrefetchScalarGridSpec(
        num_scalar_prefetch=6,
        grid=(T // tm,),
        in_specs=[pl.BlockSpec((K, tm), lambda i, *_: (0, i)),
                  pl.BlockSpec((tm, D), lambda i, *_: (i, 0))],
        out_specs=pl.BlockSpec(memory_space=pl.ANY),
        scratch_shapes=[pltpu.VMEM((LR, tm), BF16), pltpu.VMEM((LR, D), F32), pltpu.VMEM((RB, D), F32),
                        pltpu.SemaphoreType.DMA, pltpu.SemaphoreType.DMA],
    )
    return pl.pallas_call(
        kern,
        grid_spec=grid_spec,
        out_shape=jax.ShapeDtypeStruct((n_rows, D), F32),
        compiler_params=pltpu.CompilerParams(dimension_semantics=("arbitrary",),
                                             vmem_limit_bytes=VMEM_LIMIT_BYTES,
                                             has_side_effects=True),
        name="moe_dispatch",
    )(*seg_tables, zero_start, zero_valid, n_valid, lpos_t, x1)


def _expert_kernel(be_ref, nv_ref, nxt_ref, par_ref, x_ref, wgu_hbm, bg_ref, bl_ref, wd_hbm, bd_ref, y_ref,
                   wgu_buf, wd_buf, wg_s, wl_s, wd_s, sem, *, layer):
    p = pl.program_id(0)
    valid = p < nv_ref[0]
    fresh = (p == 0) | (be_ref[p] != be_ref[jnp.maximum(p - 1, 0)])
    half = MXU_DIM // 2

    def weight_copies(e, s):
        return (pltpu.make_async_copy(wgu_hbm.at[layer, e], wgu_buf.at[s], sem.at[0, s]),
                pltpu.make_async_copy(wd_hbm.at[layer, e], wd_buf.at[s], sem.at[1, s]))

    @pl.when(valid & fresh)
    def _():
        s = par_ref[p]

        @pl.when(p == 0)
        def _():
            for cp in weight_copies(be_ref[0], s):
                cp.start()

        @pl.when(nxt_ref[p] >= 0)
        def _():
            for cp in weight_copies(nxt_ref[p], 1 - s):
                cp.start()

        for cp in weight_copies(be_ref[p], s):
            cp.wait()

        r_ = lax.broadcasted_iota(I32, (MXU_DIM, MXU_DIM), 0)
        c_ = lax.broadcasted_iota(I32, (MXU_DIM, MXU_DIM), 1)
        src = jnp.where(c_ < half, 2 * c_, 2 * (c_ - half) + 1)
        perm = jnp.where(r_ == src, 1.0, 0.0).astype(BF16)
        for cb in range(wgu_buf.shape[2] // MXU_DIM):
            blk = wgu_buf[s, :, cb * MXU_DIM:(cb + 1) * MXU_DIM].astype(BF16)
            sep = jnp.dot(blk, perm, preferred_element_type=F32).astype(BF16)
            wg_s[:, cb * half:(cb + 1) * half] = sep[:, :half]
            wl_s[:, cb * half:(cb + 1) * half] = sep[:, half:]
        wd_s[...] = wd_buf[s].astype(BF16)

    @pl.when(valid)
    def _():
        x = x_ref[...].astype(BF16)
        hg = jnp.dot(x, wg_s[...], preferred_element_type=F32) + bg_ref[...]
        hl = jnp.dot(x, wl_s[...], preferred_element_type=F32) + bl_ref[...]
        xg = jnp.minimum(hg, SWIGLU_LIMIT)
        xl = jnp.clip(hl, -SWIGLU_LIMIT, SWIGLU_LIMIT)
        act = xg * jax.nn.sigmoid(SWIGLU_ALPHA * xg) * (xl + 1.0)
        y_ref[...] = jnp.dot(act.astype(BF16), wd_s[...], preferred_element_type=F32) + bd_ref[...]

    @pl.when(jnp.logical_not(valid))
    def _():
        y_ref[...] = jnp.zeros_like(y_ref)


def _experts(x_rows, blk_e, n_valid, blk_next, blk_parity, layer, w_gate_up, b_glu, b_lin, w_down, b_down):
    n_rows, D = x_rows.shape
    _, E, _, F2 = w_gate_up.shape
    F = F2 // 2
    RB = ROW_BLOCK
    n_blocks = n_rows // RB
    xblk = lambda p, be, nv, *_: (jnp.minimum(p, nv[0] - 1), 0)
    yblk = lambda p, *_: (p, 0)
    wsel = lambda p, be, *_: (be[p], 0, 0)
    grid_spec = pltpu.PrefetchScalarGridSpec(
        num_scalar_prefetch=4,
        grid=(n_blocks,),
        in_specs=[pl.BlockSpec((RB, D), xblk),
                  pl.BlockSpec(memory_space=pl.ANY),
                  pl.BlockSpec((None, 1, F), wsel), pl.BlockSpec((None, 1, F), wsel),
                  pl.BlockSpec(memory_space=pl.ANY), pl.BlockSpec((None, 1, D), wsel)],
        out_specs=pl.BlockSpec((RB, D), yblk),
        scratch_shapes=[pltpu.VMEM((2, D, F2), F32), pltpu.VMEM((2, F, D), F32),
                        pltpu.VMEM((D, F), BF16), pltpu.VMEM((D, F), BF16), pltpu.VMEM((F, D), BF16),
                        pltpu.SemaphoreType.DMA((2, 2))],
    )
    return pl.pallas_call(
        functools.partial(_expert_kernel, layer=layer),
        grid_spec=grid_spec,
        out_shape=jax.ShapeDtypeStruct((n_rows, D), F32),
        compiler_params=_cparams(("arbitrary",), EXPERT_VMEM_LIMIT_BYTES),
        name="moe_experts",
    )(blk_e, n_valid, blk_next, blk_parity, x_rows, w_gate_up, b_glu, b_lin, w_down, b_down)


def _combine_kernel(c8_ref, lo_ref, gd_ref, y_ref, lpos_ref, gate_ref, x_ref, g_ref, b_ref, o_ref,
                    ys, sem, *, tm, K, E, LR, CH):
    i = pl.program_id(0)
    n_tiles = pl.num_programs(0)
    slot = i % 2
    tables = (c8_ref, lo_ref, gd_ref)

    def seg_copy(tile, e, s):
        n, local, glob = _segment(tables, tile, e, E)
        return pltpu.make_async_copy(y_ref.at[pl.ds(glob, n), :], ys.at[s, pl.ds(local, n), :], sem.at[s])

    def for_segments(tile, s, fn):
        for e in range(E):
            @pl.when(c8_ref[tile * E + e] > 0)
            def _():
                fn(seg_copy(tile, e, s))

    @pl.when(i == 0)
    def _():
        ys[...] = jnp.zeros_like(ys)
        for_segments(0, 0, lambda cp: cp.start())

    @pl.when(i + 1 < n_tiles)
    def _():
        for_segments(i + 1, 1 - slot, lambda cp: cp.start())

    for_segments(i, slot, lambda cp: cp.wait())

    lpos = lpos_ref[...]
    gate = gate_ref[...]
    f = jnp.zeros((tm, x_ref.shape[1]), F32)
    for c in range(LR // CH):
        cols = lax.broadcasted_iota(I32, (tm, CH), 1) + c * CH
        w = jnp.zeros((tm, CH), F32)
        for r in range(K):
            w = w + jnp.where(cols == lpos[:, r:r + 1], gate[:, r:r + 1], 0.0)
        f = f + jnp.dot(w.astype(BF16), ys[slot, c * CH:(c + 1) * CH, :].astype(BF16),
                        preferred_element_type=F32)
    o_ref[...] = _layer_norm(DEEPNORM_ALPHA * x_ref[...] + f, g_ref[...], b_ref[...])


def _combine(y_rows, lpos_c, seg_tables, x1, gates_c, ln_g, ln_b):
    T, D = x1.shape
    K, E = MOE_TOP_K, MOE_EXPERTS
    tm = MOE_TILE
    LR = _local_rows(tm)
    kern = functools.partial(_combine_kernel, tm=tm, K=K, E=E, LR=LR, CH=SORT_CHUNK)
    grid_spec = pltpu.PrefetchScalarGridSpec(
        num_scalar_prefetch=3,
        grid=(T // tm,),
        in_specs=[pl.BlockSpec(memory_space=pl.ANY),
                  pl.BlockSpec((tm, K), lambda i, *_: (i, 0)),
                  pl.BlockSpec((tm, K), lambda i, *_: (i, 0)),
                  pl.BlockSpec((tm, D), lambda i, *_: (i, 0)),
                  pl.BlockSpec((1, D), lambda i, *_: (0, 0)), pl.BlockSpec((1, D), lambda i, *_: (0, 0))],
        out_specs=pl.BlockSpec((tm, D), lambda i, *_: (i, 0)),
        scratch_shapes=[pltpu.VMEM((2, LR, D), F32), pltpu.SemaphoreType.DMA((2,))],
    )
    return pl.pallas_call(
        kern,
        grid_spec=grid_spec,
        out_shape=jax.ShapeDtypeStruct((T, D), F32),
        compiler_params=_cparams(("arbitrary",)),
        name="moe_combine",
    )(*seg_tables, y_rows, lpos_c, gates_c, x1, ln_g.reshape(1, D), ln_b.reshape(1, D))


def _moe(x1, e_t, g_t, p_t, cnt, layer, w_gate_up, b_gate_up, w_down, b_down, ln_g, ln_b):
    T, D = x1.shape
    E, K, RB = MOE_EXPERTS, MOE_TOP_K, ROW_BLOCK
    tm = MOE_TILE
    n_tiles = T // tm
    n_blocks = -(-(T * K + n_tiles * E * (SUBLANES - 1) + E * (RB - 1)) // RB)
    n_rows = n_blocks * RB

    seg_len = (cnt[:, ::LANES].T.astype(I32) + SUBLANES - 1) // SUBLANES * SUBLANES
    local_off = jnp.cumsum(seg_len, axis=1) - seg_len
    group_rows = jnp.sum(seg_len, axis=0)
    padded = (group_rows + RB - 1) // RB * RB
    pad_end = jnp.cumsum(padded)
    pad_start = pad_end - padded
    global_off = pad_start[None, :] + jnp.cumsum(seg_len, axis=0) - seg_len
    seg_tables = (seg_len.reshape(-1), local_off.reshape(-1).astype(I32), global_off.reshape(-1).astype(I32))

    experts = jnp.arange(E, dtype=I32)
    e_tiles = e_t.reshape(K, n_tiles, tm)
    off_sel = jnp.sum(jnp.where(e_tiles[..., None] == experts, local_off[None, :, None, :], 0), axis=-1)
    lpos_t = (off_sel.reshape(K, T) + p_t).astype(I32)

    n_valid = (pad_end[-1] // RB).astype(I32)
    blk_ids = jnp.minimum(jnp.arange(n_blocks, dtype=I32), n_valid - 1)
    blk_e = jnp.minimum(jnp.sum((pad_end[None, :] <= blk_ids[:, None] * RB).astype(I32), axis=1), E - 1)
    zero_start = jnp.maximum(pad_end - RB, 0).astype(I32)
    nonempty = group_rows > 0
    zero_valid = nonempty.astype(I32)
    n_valid = n_valid.reshape(1)
    later = (experts[None, :] > experts[:, None]) & nonempty[None, :]
    next_e = jnp.min(jnp.where(later, experts[None, :], E), axis=1)
    next_e = jnp.where(next_e < E, next_e, -1).astype(I32)
    parity_e = ((jnp.cumsum(nonempty.astype(I32)) - nonempty.astype(I32)) % 2).astype(I32)
    at_block = blk_e[:, None] == experts[None, :]
    blk_next = jnp.sum(jnp.where(at_block, next_e[None, :], 0), axis=1).astype(I32)
    blk_parity = jnp.sum(jnp.where(at_block, parity_e[None, :], 0), axis=1).astype(I32)

    x_rows = _dispatch(x1, lpos_t, seg_tables, zero_start, zero_valid, n_valid, n_rows)
    y_rows = _experts(x_rows, blk_e, n_valid, blk_next, blk_parity, layer, w_gate_up, b_gate_up[:, None, 0::2],
                      b_gate_up[:, None, 1::2], w_down, b_down[:, None, :])
    return _combine(y_rows, lpos_t.T, seg_tables, x1, g_t.T, ln_g, ln_b)


def kernel(x, positions, attn_w_qkv, attn_w_o, mlstm_w_in, mlstm_b_gates, mlstm_norm_g, mlstm_w_out,
           ln_mix_g, ln_mix_b, ln_ffn_g, ln_ffn_b, router_w, router_b,
           w_gate_up, b_gate_up, w_down, b_down):
    B, S, D = x.shape
    T = B * S
    xt = x.reshape(T, D)
    NH = MLSTM_HEADS

    for layer in range(DEPTH):
        slot = layer // 2
        if layer % 2 == 0:
            hd = attn_w_o.shape[1]
            qkv = _qkv_rope(xt, positions.reshape(T, 1), attn_w_qkv[slot].astype(BF16), hd)
            mixed = _moba_attention(qkv, B, S, hd)
            w_out = attn_w_o[slot].astype(BF16)
        else:
            vw = mlstm_w_out.shape[1]
            qw = vw // 2
            w_in = mlstm_w_in[slot]
            dqk = qw // NH
            col_scale = jnp.concatenate([jnp.ones((qw,), F32), jnp.full((qw,), dqk ** -0.5, F32),
                                         jnp.ones((2 * vw,), F32)])
            w_main = (w_in[:, :2 * qw + 2 * vw] * col_scale).astype(BF16)
            w_gate = jnp.pad(w_in[:, 2 * qw + 2 * vw:], ((0, 0), (0, LANES - 2 * NH))).astype(BF16)
            proj, gates = _mlstm_proj(xt, w_main, w_gate)
            bg = mlstm_b_gates[slot]
            b_col = jnp.pad(bg, (0, LANES - 2 * NH)).reshape(1, LANES)
            b_row = bg.reshape(2 * NH, 1)
            mixed = _mlstm_mix(proj, gates, gates[:, :2 * NH].T, b_col, b_row,
                               mlstm_norm_g[slot].reshape(1, vw), B, S)
            w_out = mlstm_w_out[slot].astype(BF16)
        x1, e_t, g_t, p_t, cnt = _mix_out_router(mixed, w_out, xt, ln_mix_g[layer], ln_mix_b[layer],
                                                 router_w[layer], router_b[layer])
        xt = _moe(x1, e_t, g_t, p_t, cnt, layer, w_gate_up, b_gate_up[layer], w_down, b_down[layer],
                  ln_ffn_g[layer], ln_ffn_b[layer])
    return xt.reshape(B, S, D)
```

```python
import functools

import jax
import jax.numpy as jnp
from jax import lax
from jax.experimental import pallas as pl
from jax.experimental.pallas import tpu as pltpu

F32 = jnp.float32
BF16 = jnp.bfloat16
I32 = jnp.int32

DEPTH = 2
ATTN_HEAD_DIM = 64
ROT_DIM = ATTN_HEAD_DIM // 4
ROPE_THETA = 500000.0
MOBA_BLOCK = 256
MOBA_TOP_K = 3
MLSTM_HEADS = 4
MOE_EXPERTS = 32
MOE_TOP_K = 4
SWIGLU_LIMIT = 7.0
SWIGLU_ALPHA = 1.702
DEEPNORM_ALPHA = (2 * DEPTH) ** 0.25
LN_EPS = 1e-5

LANES = 128
SUBLANES = 8
BF16_SUBLANES = 16
MXU_DIM = 256
VMEM_LIMIT_BYTES = 48 * 1024 * 1024
EXPERT_VMEM_LIMIT_BYTES = 58 * 1024 * 1024

PROJ_TM = 512
PROJ_TN = 512
MOE_TILE = 512
MLSTM_L = 256
ROW_BLOCK = 512
SORT_CHUNK = 256

NT_DIMS = (((1,), (1,)), ((), ()))
TN_DIMS = (((0,), (0,)), ((), ()))
NEG_INF = float("-inf")
MASKED = -1e30
LOG2_E = 1.4426950408889634


def _cparams(sem, vmem=VMEM_LIMIT_BYTES):
    return pltpu.CompilerParams(dimension_semantics=sem, vmem_limit_bytes=vmem)


def _local_rows(tm):
    rows = tm * MOE_TOP_K + MOE_EXPERTS * SUBLANES
    return -(-rows // SORT_CHUNK) * SORT_CHUNK


def _qkv_rope_kernel(pos_ref, invf_ref, x_ref, w_ref, o_ref, *, n_rope_chunks, n_q_chunks, tn):
    ang = pos_ref[...].astype(F32) * invf_ref[...]
    d = lax.broadcasted_iota(I32, ang.shape, 1) & (ATTN_HEAD_DIM - 1)
    cos = jnp.cos(ang)
    sin = jnp.sin(ang)
    half = ROT_DIM // 2
    c_tab = jnp.where(d < ROT_DIM, cos, 1.0)
    s_up = jnp.where(d < half, -sin, 0.0)
    s_dn = jnp.where((d >= half) & (d < ROT_DIM), sin, 0.0)

    xb = x_ref[...].astype(BF16)
    for c in range(w_ref.shape[1] // tn):
        acc = jnp.dot(xb, w_ref[:, c * tn:(c + 1) * tn], preferred_element_type=F32)
        if c >= n_rope_chunks:
            o_ref[:, c * tn:(c + 1) * tn] = acc.astype(o_ref.dtype)
            continue
        scale = ATTN_HEAD_DIM ** -0.5 * LOG2_E if c < n_q_chunks else 1.0
        for s in range(tn // LANES):
            blk = acc[:, s * LANES:(s + 1) * LANES]
            r = (blk * c_tab + pltpu.roll(blk, LANES - half, 1) * s_up + pltpu.roll(blk, half, 1) * s_dn)
            lo = c * tn + s * LANES
            o_ref[:, lo:lo + LANES] = (r * scale).astype(o_ref.dtype)


def _qkv_rope(x, pos, w_bf16, hd):
    T, D = x.shape
    N = w_bf16.shape[1]
    tm, tn = PROJ_TM, PROJ_TN
    inv_freq = ROPE_THETA ** (-jnp.arange(0, ROT_DIM, 2, dtype=F32) / ROT_DIM)
    lane_d = jnp.arange(LANES) % ATTN_HEAD_DIM
    invf = jnp.where(lane_d < ROT_DIM, inv_freq[lane_d % (ROT_DIM // 2)], 0.0).reshape(1, LANES).astype(F32)
    kern = functools.partial(_qkv_rope_kernel, n_rope_chunks=2 * hd // tn, n_q_chunks=hd // tn, tn=tn)
    return pl.pallas_call(
        kern,
        grid=(T // tm,),
        in_specs=[pl.BlockSpec((tm, 1), lambda i: (i, 0)),
                  pl.BlockSpec((1, LANES), lambda i: (0, 0)),
                  pl.BlockSpec((tm, D), lambda i: (i, 0)),
                  pl.BlockSpec((D, N), lambda i: (0, 0))],
        out_specs=pl.BlockSpec((tm, N), lambda i: (i, 0)),
        out_shape=jax.ShapeDtypeStruct((T, N), BF16),
        compiler_params=_cparams(("parallel",)),
        name="qkv_rope",
    )(pos, invf, x, w_bf16)


def _moba_kernel(q_ref, k_ref, v_ref, o_ref, kmh_ref, kml_ref, vt_ref, q2t_ref, bias_ref,
                 m_ref, acc_ref, *score_bufs, nb, blk):
    s_refs, smax_refs = score_bufs[:4], score_bufs[4:]
    i = pl.program_id(2)
    hd = ATTN_HEAD_DIM
    nbp = kmh_ref.shape[0]
    va = vt_ref.shape[2]

    @pl.when(i == 0)
    def _():
        rows = [jnp.mean(k_ref[j * blk:(j + 1) * blk, :].astype(F32), axis=0, keepdims=True)
                for j in range(nb)]
        if nbp > nb:
            rows.append(jnp.zeros((nbp - nb, LANES), F32))
        km = jnp.concatenate(rows, axis=0)
        hi = km.astype(BF16)
        kmh_ref[...] = hi
        kml_ref[...] = (km - hi.astype(F32)).astype(BF16)
        sub_v = lax.broadcasted_iota(I32, (va - hd, blk), 0)
        ones_row = jnp.where(sub_v == 0, 1.0, 0.0).astype(BF16)
        for j in range(nb):
            vt = v_ref[j * blk:(j + 1) * blk, :].astype(F32).T.astype(BF16)
            vt_ref[j, 0] = jnp.concatenate([vt[:hd], ones_row], axis=0)
            vt_ref[j, 1] = jnp.concatenate([vt[hd:], ones_row], axis=0)

    qt = q_ref[...].astype(F32).T
    sub_q = lax.broadcasted_iota(I32, qt.shape, 0)
    zero = jnp.zeros_like(qt)
    q2t = jnp.concatenate([jnp.where(sub_q < hd, qt, zero), jnp.where(sub_q >= hd, qt, zero)],
                          axis=1).astype(BF16)
    q2t_ref[...] = q2t

    gate = (jnp.dot(kmh_ref[...], q2t, preferred_element_type=F32)
            + jnp.dot(kml_ref[...], q2t, preferred_element_type=F32))
    sub = lax.broadcasted_iota(I32, gate.shape, 0)
    sub_f = sub.astype(F32)
    g = jnp.where(sub < i, gate, NEG_INF)
    sel = jnp.zeros(gate.shape, F32)
    for _ in range(MOBA_TOP_K):
        mx = jnp.max(g, axis=0, keepdims=True)
        idx = jnp.min(jnp.where(g == mx, sub_f, float(nbp)), axis=0, keepdims=True)
        hit = (sub_f == idx) & (mx > NEG_INF)
        sel = jnp.where(hit, 1.0, sel)
        g = jnp.where(hit, NEG_INF, g)
    bias_ref[...] = jnp.where(sel > 0.0, 0.0, MASKED)

    def block_bias(pos, j):
        return jnp.where(pos == 0, 0.0, bias_ref[pl.ds(j, 1), :])

    def produce(pos, buf):
        s_ref, smax_ref = buf
        if isinstance(pos, int) and pos == 0:
            ki = k_ref[pl.ds(pl.multiple_of(i * blk, blk), blk), :]
            s = jnp.dot(ki, q2t, preferred_element_type=F32)
            key = lax.broadcasted_iota(I32, s.shape, 0)
            qry = lax.broadcasted_iota(I32, s.shape, 1) & (blk - 1)
            s = jnp.where(key <= qry, s, MASKED)
            smax = jnp.max(s, axis=0, keepdims=True)
        else:
            j = jnp.minimum(pos - 1, i)
            kj = k_ref[pl.ds(pl.multiple_of(j * blk, blk), blk), :]
            s = jnp.dot(kj, q2t_ref[...], preferred_element_type=F32)
            smax = jnp.max(s, axis=0, keepdims=True) + block_bias(pos, j)
        s_ref[...] = s
        smax_ref[...] = smax

    def consume(pos, buf):
        s_ref, smax_ref = buf
        j = jnp.where(pos == 0, i, jnp.minimum(pos - 1, i))
        m_old = m_ref[...]
        m_new = jnp.maximum(m_old, smax_ref[...])
        a = jnp.exp2(m_old - m_new)
        pb = jnp.exp2(s_ref[...] - (m_new - block_bias(pos, j))).astype(BF16)
        upd = jnp.concatenate([jnp.dot(vt_ref[j, 0], pb[:, :blk], preferred_element_type=F32),
                               jnp.dot(vt_ref[j, 1], pb[:, blk:], preferred_element_type=F32)], axis=1)
        acc_ref[...] = a * acc_ref[...] + upd
        m_ref[...] = m_new

    m_ref[...] = jnp.full(m_ref.shape, MASKED, F32)
    acc_ref[...] = jnp.zeros_like(acc_ref)
    buf_a, buf_b, buf_c, buf_d = [(s_refs[n], smax_refs[n]) for n in range(4)]
    produce(0, buf_a)
    produce(1, buf_b)

    def body(t, carry):
        base = 4 * t
        produce(base + 2, buf_c)
        produce(base + 3, buf_d)
        consume(base, buf_a)
        consume(base + 1, buf_b)
        produce(base + 4, buf_a)
        produce(base + 5, buf_b)
        consume(base + 2, buf_c)
        consume(base + 3, buf_d)
        return carry

    full = (i + 1) // 4
    left = (i + 1) % 4
    lax.fori_loop(0, full, body, 0)
    base = 4 * full

    @pl.when(left == 1)
    def _():
        consume(base, buf_a)

    @pl.when(left == 2)
    def _():
        consume(base, buf_a)
        consume(base + 1, buf_b)

    @pl.when(left == 3)
    def _():
        produce(base + 2, buf_c)
        consume(base, buf_a)
        consume(base + 1, buf_b)
        consume(base + 2, buf_c)

    acc = acc_ref[...]
    out_t = acc[:hd] / acc[hd:hd + 1]
    o_ref[...] = jnp.concatenate([out_t[:, :blk], out_t[:, blk:]], axis=0).T.astype(o_ref.dtype)


def _moba_attention(qkv, B, S, hd):
    T = B * S
    blk = MOBA_BLOCK
    nb = S // blk
    nbp = -(-nb // BF16_SUBLANES) * BF16_SUBLANES
    va = ATTN_HEAD_DIM + BF16_SUBLANES
    n_pairs = hd // LANES
    kern = functools.partial(_moba_kernel, nb=nb, blk=blk)
    return pl.pallas_call(
        kern,
        grid=(B, n_pairs, nb),
        in_specs=[pl.BlockSpec((blk, LANES), lambda b, h, i: (b * nb + i, h)),
                  pl.BlockSpec((S, LANES), lambda b, h, i: (b, n_pairs + h)),
                  pl.BlockSpec((S, LANES), lambda b, h, i: (b, 2 * n_pairs + h))],
        out_specs=pl.BlockSpec((blk, LANES), lambda b, h, i: (b * nb + i, h)),
        out_shape=jax.ShapeDtypeStruct((T, hd), BF16),
        scratch_shapes=[pltpu.VMEM((nbp, LANES), BF16), pltpu.VMEM((nbp, LANES), BF16),
                        pltpu.VMEM((nb, 2, va, blk), BF16),
                        pltpu.VMEM((LANES, 2 * blk), BF16), pltpu.VMEM((nbp, 2 * blk), F32),
                        pltpu.VMEM((1, 2 * blk), F32), pltpu.VMEM((va, 2 * blk), F32)]
                       + [pltpu.VMEM((blk, 2 * blk), F32)] * 4 + [pltpu.VMEM((1, 2 * blk), F32)] * 4,
        compiler_params=_cparams(("parallel", "parallel", "arbitrary")),
        name="moba_attention",
    )(qkv, qkv, qkv)


def _mlstm_proj_kernel(x_ref, w_ref, wg_ref, o_ref, g_ref, *, tn):
    xb = x_ref[...].astype(BF16)
    for c in range(w_ref.shape[1] // tn):
        o_ref[:, c * tn:(c + 1) * tn] = jnp.dot(xb, w_ref[:, c * tn:(c + 1) * tn],
                                                preferred_element_type=F32).astype(o_ref.dtype)
    g_ref[...] = jnp.dot(xb, wg_ref[...], preferred_element_type=F32)


def _mlstm_proj(x, w_main, w_gate):
    T, D = x.shape
    N = w_main.shape[1]
    tm = PROJ_TM
    return pl.pallas_call(
        functools.partial(_mlstm_proj_kernel, tn=PROJ_TN),
        grid=(T // tm,),
        in_specs=[pl.BlockSpec((tm, D), lambda i: (i, 0)),
                  pl.BlockSpec((D, N), lambda i: (0, 0)),
                  pl.BlockSpec((D, LANES), lambda i: (0, 0))],
        out_specs=[pl.BlockSpec((tm, N), lambda i: (i, 0)),
                   pl.BlockSpec((tm, LANES), lambda i: (i, 0))],
        out_shape=[jax.ShapeDtypeStruct((T, N), BF16), jax.ShapeDtypeStruct((T, LANES), F32)],
        compiler_params=_cparams(("parallel",)),
        name="mlstm_proj",
    )(x, w_main, w_gate)


def _log_sigmoid(x):
    return jnp.minimum(x, 0.0) - jnp.log(1.0 + jnp.exp(-jnp.abs(x)))


def _mlstm_kernel(q_ref, k_ref, v_ref, o_ref, gc_ref, gr_ref, bc_ref, br_ref, ng_ref, out_ref,
                  C_ref, n_ref, m_ref, *, L, NH, DQK, DV):
    c = pl.program_id(1)

    @pl.when(c == 0)
    def _():
        C_ref[...] = jnp.zeros_like(C_ref)
        n_ref[...] = jnp.zeros_like(n_ref)
        m_ref[...] = jnp.zeros_like(m_ref)

    gcol = gc_ref[...] + bc_ref[...]
    grow = gr_ref[...] + br_ref[...]
    lane = lax.broadcasted_iota(I32, gcol.shape, 1)
    lf_col = jnp.where((lane >= NH) & (lane < 2 * NH), _log_sigmoid(gcol), 0.0)
    lf_row = _log_sigmoid(grow)
    row = lax.broadcasted_iota(I32, (L, L), 0)
    col = lax.broadcasted_iota(I32, (L, L), 1)
    causal = col <= row
    tri = causal.astype(F32)
    hp = lax.Precision.HIGHEST
    b_cols = jnp.dot(tri, lf_col, preferred_element_type=F32, precision=hp)
    b_rows = lax.dot_general(lf_row, tri, NT_DIMS, preferred_element_type=F32, precision=hp)

    for h in range(NH):
        i_row = grow[h:h + 1, :]
        i_col = gcol[:, h:h + 1]
        b_row = b_rows[NH + h:NH + h + 1, :]
        b_col = b_cols[:, NH + h:NH + h + 1]
        m_prev = m_ref[h:h + 1, 0:1]
        qh = q_ref[:, h * DQK:(h + 1) * DQK]
        kh = k_ref[:, h * DQK:(h + 1) * DQK]
        vh = v_ref[:, h * DV:(h + 1) * DV]

        D = jnp.where(causal, b_col - b_row + i_row, NEG_INF)
        g = b_col + m_prev
        m_t = jnp.maximum(g, jnp.max(D, axis=1, keepdims=True))
        w_inter = jnp.exp(g - m_t)
        qk = lax.dot_general(qh, kh, NT_DIMS, preferred_element_type=F32)
        A = jnp.exp(D - m_t) * qk
        num = (w_inter * jnp.dot(qh, C_ref[h].astype(BF16), preferred_element_type=F32)
               + jnp.dot(A.astype(BF16), vh, preferred_element_type=F32))
        qn = jnp.sum(qh.astype(F32) * n_ref[h:h + 1, :], axis=1, keepdims=True)
        den = w_inter * qn + jnp.sum(A, axis=1, keepdims=True)
        hh = num / jnp.maximum(jnp.abs(den), jnp.exp(-m_t))

        m_new = m_t[L - 1:L, :]
        b_last = b_col[L - 1:L, :]
        decay = jnp.exp(b_last + m_prev - m_new)
        w_s = jnp.exp(b_last - b_col + i_col - m_new)
        kw = kh.astype(F32) * w_s
        C_ref[h] = decay * C_ref[h] + lax.dot_general(kw.astype(BF16), vh, TN_DIMS,
                                                      preferred_element_type=F32)
        n_ref[h:h + 1, :] = decay * n_ref[h:h + 1, :] + jnp.sum(kw, axis=0, keepdims=True)
        m_ref[h:h + 1, :] = jnp.broadcast_to(m_new, (1, LANES))

        mu = jnp.mean(hh, axis=1, keepdims=True)
        hc = hh - mu
        var = jnp.mean(hc * hc, axis=1, keepdims=True)
        hn = hc * lax.rsqrt(var + LN_EPS) * ng_ref[:, h * DV:(h + 1) * DV]
        og = jax.nn.sigmoid(o_ref[:, h * DV:(h + 1) * DV].astype(F32))
        out_ref[:, h * DV:(h + 1) * DV] = (og * hn).astype(out_ref.dtype)


def _mlstm_mix(proj, gates, gates_t, b_col, b_row, norm_g, B, S):
    T = B * S
    NH = MLSTM_HEADS
    DV = norm_g.shape[1] // NH
    DQK = DV // 2
    L = MLSTM_L
    nc = S // L
    qw = NH * DQK
    vw = NH * DV
    kern = functools.partial(_mlstm_kernel, L=L, NH=NH, DQK=DQK, DV=DV)
    return pl.pallas_call(
        kern,
        grid=(B, nc),
        in_specs=[pl.BlockSpec((L, qw), lambda b, c: (b * nc + c, 0)),
                  pl.BlockSpec((L, qw), lambda b, c: (b * nc + c, 1)),
                  pl.BlockSpec((L, vw), lambda b, c: (b * nc + c, 2 * qw // vw)),
                  pl.BlockSpec((L, vw), lambda b, c: (b * nc + c, 2 * qw // vw + 1)),
                  pl.BlockSpec((L, LANES), lambda b, c: (b * nc + c, 0)),
                  pl.BlockSpec((SUBLANES, L), lambda b, c: (0, b * nc + c)),
                  pl.BlockSpec((1, LANES), lambda b, c: (0, 0)),
                  pl.BlockSpec((SUBLANES, 1), lambda b, c: (0, 0)),
                  pl.BlockSpec((1, vw), lambda b, c: (0, 0))],
        out_specs=pl.BlockSpec((L, vw), lambda b, c: (b * nc + c, 0)),
        out_shape=jax.ShapeDtypeStruct((T, vw), BF16),
        scratch_shapes=[pltpu.VMEM((NH, DQK, DV), F32), pltpu.VMEM((SUBLANES, DQK), F32),
                        pltpu.VMEM((SUBLANES, LANES), F32)],
        compiler_params=_cparams(("parallel", "arbitrary")),
        name="mlstm_mix",
    )(proj, proj, proj, proj, gates, gates_t, b_col, b_row, norm_g)


def _layer_norm(z, g, b):
    mu = jnp.mean(z, axis=-1, keepdims=True)
    zc = z - mu
    var = jnp.mean(zc * zc, axis=-1, keepdims=True)
    return zc * lax.rsqrt(var + LN_EPS) * g + b


def _mix_out_router_kernel(y_ref, w_ref, x_ref, g_ref, b_ref, rwt_ref, rb_ref,
                           x1_ref, et_ref, gt_ref, pt_ref, cnt_ref, *, tm, E, K):
    y = jnp.dot(y_ref[...], w_ref[...], preferred_element_type=F32)
    x1 = _layer_norm(DEEPNORM_ALPHA * x_ref[...] + y, g_ref[...], b_ref[...])
    x1_ref[...] = x1

    logits = lax.dot_general(rwt_ref[...], x1, NT_DIMS, preferred_element_type=F32,
                             precision=lax.Precision.HIGHEST) + rb_ref[...]
    sub = lax.broadcasted_iota(I32, logits.shape, 0).astype(F32)
    vals, hits = [], []
    cur = logits
    for _ in range(K):
        mx = jnp.max(cur, axis=0, keepdims=True)
        idx = jnp.min(jnp.where(cur == mx, sub, float(E)), axis=0, keepdims=True)
        hit = sub == idx
        vals.append(mx)
        hits.append(hit)
        cur = jnp.where(hit, NEG_INF, cur)

    chosen = jnp.zeros(logits.shape, F32)
    for hit in hits:
        chosen = jnp.where(hit, 1.0, chosen)
    r_ = lax.broadcasted_iota(I32, (tm, tm), 0)
    c_ = lax.broadcasted_iota(I32, (tm, tm), 1)
    before = (r_ < c_).astype(BF16)
    pos = jnp.dot(chosen.astype(BF16), before, preferred_element_type=F32)
    cnt_ref[...] = jnp.broadcast_to(jnp.sum(chosen, axis=1, keepdims=True), cnt_ref.shape)

    ex = [jnp.exp(v - vals[0]) for v in vals]
    den = ex[0]
    for e_ in ex[1:]:
        den = den + e_
    for r in range(K):
        gt_ref[r:r + 1, :] = ex[r] / den
        et_ref[r:r + 1, :] = jnp.sum(jnp.where(hits[r], sub, 0.0), axis=0, keepdims=True).astype(I32)
        pt_ref[r:r + 1, :] = jnp.sum(jnp.where(hits[r], pos, 0.0), axis=0, keepdims=True).astype(I32)


def _mix_out_router(y, w_bf16, x, ln_g, ln_b, router_w, router_b):
    T, D = x.shape
    Kin = y.shape[1]
    E, K = MOE_EXPERTS, MOE_TOP_K
    tm = MOE_TILE
    kern = functools.partial(_mix_out_router_kernel, tm=tm, E=E, K=K)
    row = lambda i: (i, 0)
    fixed = lambda i: (0, 0)
    tok = lambda i: (0, i)
    return pl.pallas_call(
        kern,
        grid=(T // tm,),
        in_specs=[pl.BlockSpec((tm, Kin), row), pl.BlockSpec((Kin, D), fixed),
                  pl.BlockSpec((tm, D), row), pl.BlockSpec((1, D), fixed), pl.BlockSpec((1, D), fixed),
                  pl.BlockSpec((E, D), fixed), pl.BlockSpec((E, 1), fixed)],
        out_specs=[pl.BlockSpec((tm, D), row), pl.BlockSpec((K, tm), tok), pl.BlockSpec((K, tm), tok),
                   pl.BlockSpec((K, tm), tok), pl.BlockSpec((E, LANES), tok)],
        out_shape=[jax.ShapeDtypeStruct((T, D), F32), jax.ShapeDtypeStruct((K, T), I32),
                   jax.ShapeDtypeStruct((K, T), F32), jax.ShapeDtypeStruct((K, T), I32),
                   jax.ShapeDtypeStruct((E, (T // tm) * LANES), F32)],
        compiler_params=_cparams(("parallel",)),
        name="mix_out_router",
    )(y, w_bf16, x, ln_g.reshape(1, D), ln_b.reshape(1, D), router_w.T, router_b.reshape(E, 1))


def _segment(tables, tile, e, E):
    c8_ref, lo_ref, gd_ref = tables
    idx = tile * E + e
    n = pl.multiple_of(c8_ref[idx], SUBLANES)
    local = pl.multiple_of(lo_ref[idx], SUBLANES)
    glob = pl.multiple_of(gd_ref[idx], SUBLANES)
    return n, local, glob


def _dispatch_kernel(c8_ref, lo_ref, gd_ref, zs_ref, zv_ref, nv_ref, lpos_ref, x_ref, xr_ref,
                     perm_s, sorted_s, zbuf, sem, zsem, *, tm, K, E, RB, n_blocks, LR, CH):
    i = pl.program_id(0)
    tables = (c8_ref, lo_ref, gd_ref)

    def zero_copy(start):
        return pltpu.make_async_copy(zbuf, xr_ref.at[pl.ds(pl.multiple_of(start, RB), RB), :], zsem)

    @pl.when(i == 0)
    def _():
        zbuf[...] = jnp.zeros_like(zbuf)
        for e in range(E):
            @pl.when(zv_ref[e] > 0)
            def _():
                zero_copy(zs_ref[e]).start()

        def tail_start(p, carry):
            zero_copy(p * RB).start()
            return carry

        def tail_wait(p, carry):
            zero_copy(p * RB).wait()
            return carry

        lax.fori_loop(nv_ref[0], n_blocks, tail_start, 0)
        for e in range(E):
            @pl.when(zv_ref[e] > 0)
            def _():
                zero_copy(zs_ref[e]).wait()
        lax.fori_loop(nv_ref[0], n_blocks, tail_wait, 0)

    def seg_copy(tile, e):
        n, local, glob = _segment(tables, tile, e, E)
        return pltpu.make_async_copy(sorted_s.at[pl.ds(local, n), :], xr_ref.at[pl.ds(glob, n), :], sem)

    def for_segments(tile, fn):
        for e in range(E):
            @pl.when(c8_ref[tile * E + e] > 0)
            def _():
                fn(seg_copy(tile, e))

    lpos = lpos_ref[...]
    for c in range(LR // CH):
        rows = lax.broadcasted_iota(I32, (CH, tm), 0) + c * CH
        hit = jnp.zeros((CH, tm), F32)
        for r in range(K):
            hit = hit + jnp.where(rows == lpos[r:r + 1, :], 1.0, 0.0)
        perm_s[c * CH:(c + 1) * CH, :] = hit.astype(BF16)

    @pl.when(i > 0)
    def _():
        for_segments(i - 1, lambda cp: cp.wait())

    xb = x_ref[...].astype(BF16)
    for c in range(LR // CH):
        sorted_s[c * CH:(c + 1) * CH, :] = jnp.dot(perm_s[c * CH:(c + 1) * CH, :], xb,
                                                   preferred_element_type=F32)
    for_segments(i, lambda cp: cp.start())

    @pl.when(i == pl.num_programs(0) - 1)
    def _():
        for_segments(i, lambda cp: cp.wait())


def _dispatch(x1, lpos_t, seg_tables, zero_start, zero_valid, n_valid, n_rows):
    T, D = x1.shape
    K, E, RB = MOE_TOP_K, MOE_EXPERTS, ROW_BLOCK
    tm = MOE_TILE
    LR = _local_rows(tm)
    kern = functools.partial(_dispatch_kernel, tm=tm, K=K, E=E, RB=RB, n_blocks=n_rows // RB, LR=LR, CH=SORT_CHUNK)
    grid_spec = pltpu.PrefetchScalarGridSpec(
        num_scalar_prefetch=6,
        grid=(T // tm,),
        in_specs=[pl.BlockSpec((K, tm), lambda i, *_: (0, i)),
                  pl.BlockSpec((tm, D), lambda i, *_: (i, 0))],
        out_specs=pl.BlockSpec(memory_space=pl.ANY),
        scratch_shapes=[pltpu.VMEM((LR, tm), BF16), pltpu.VMEM((LR, D), F32), pltpu.VMEM((RB, D), F32),
                        pltpu.SemaphoreType.DMA, pltpu.SemaphoreType.DMA],
    )
    return pl.pallas_call(
        kern,
        grid_spec=grid_spec,
        out_shape=jax.ShapeDtypeStruct((n_rows, D), F32),
        compiler_params=pltpu.CompilerParams(dimension_semantics=("arbitrary",),
                                             vmem_limit_bytes=VMEM_LIMIT_BYTES,
                                             has_side_effects=True),
        name="moe_dispatch",
    )(*seg_tables, zero_start, zero_valid, n_valid, lpos_t, x1)


def _expert_kernel(be_ref, nv_ref, nxt_ref, par_ref, x_ref, wgu_hbm, bg_ref, bl_ref, wd_hbm, bd_ref, y_ref,
                   wgu_buf, wd_buf, wg_s, wl_s, wd_s, sem, *, layer):
    p = pl.program_id(0)
    valid = p < nv_ref[0]
    fresh = (p == 0) | (be_ref[p] != be_ref[jnp.maximum(p - 1, 0)])
    half = MXU_DIM // 2

    def weight_copies(e, s):
        return (pltpu.make_async_copy(wgu_hbm.at[layer, e], wgu_buf.at[s], sem.at[0, s]),
                pltpu.make_async_copy(wd_hbm.at[layer, e], wd_buf.at[s], sem.at[1, s]))

    @pl.when(valid & fresh)
    def _():
        s = par_ref[p]

        @pl.when(p == 0)
        def _():
            for cp in weight_copies(be_ref[0], s):
                cp.start()

        @pl.when(nxt_ref[p] >= 0)
        def _():
            for cp in weight_copies(nxt_ref[p], 1 - s):
                cp.start()

        for cp in weight_copies(be_ref[p], s):
            cp.wait()

        r_ = lax.broadcasted_iota(I32, (MXU_DIM, MXU_DIM), 0)
        c_ = lax.broadcasted_iota(I32, (MXU_DIM, MXU_DIM), 1)
        src = jnp.where(c_ < half, 2 * c_, 2 * (c_ - half) + 1)
        perm = jnp.where(r_ == src, 1.0, 0.0).astype(BF16)
        for cb in range(wgu_buf.shape[2] // MXU_DIM):
            blk = wgu_buf[s, :, cb * MXU_DIM:(cb + 1) * MXU_DIM].astype(BF16)
            sep = jnp.dot(blk, perm, preferred_element_type=F32).astype(BF16)
            wg_s[:, cb * half:(cb + 1) * half] = sep[:, :half]
            wl_s[:, cb * half:(cb + 1) * half] = sep[:, half:]
        wd_s[...] = wd_buf[s].astype(BF16)

    @pl.when(valid)
    def _():
        x = x_ref[...].astype(BF16)
        hg = jnp.dot(x, wg_s[...], preferred_element_type=F32) + bg_ref[...]
        hl = jnp.dot(x, wl_s[...], preferred_element_type=F32) + bl_ref[...]
        xg = jnp.minimum(hg, SWIGLU_LIMIT)
        xl = jnp.clip(hl, -SWIGLU_LIMIT, SWIGLU_LIMIT)
        act = xg * jax.nn.sigmoid(SWIGLU_ALPHA * xg) * (xl + 1.0)
        y_ref[...] = jnp.dot(act.astype(BF16), wd_s[...], preferred_element_type=F32) + bd_ref[...]

    @pl.when(jnp.logical_not(valid))
    def _():
        y_ref[...] = jnp.zeros_like(y_ref)


def _experts(x_rows, blk_e, n_valid, blk_next, blk_parity, layer, w_gate_up, b_glu, b_lin, w_down, b_down):
    n_rows, D = x_rows.shape
    _, E, _, F2 = w_gate_up.shape
    F = F2 // 2
    RB = ROW_BLOCK
    n_blocks = n_rows // RB
    xblk = lambda p, be, nv, *_: (jnp.minimum(p, nv[0] - 1), 0)
    yblk = lambda p, *_: (p, 0)
    wsel = lambda p, be, *_: (be[p], 0, 0)
    grid_spec = pltpu.PrefetchScalarGridSpec(
        num_scalar_prefetch=4,
        grid=(n_blocks,),
        in_specs=[pl.BlockSpec((RB, D), xblk),
                  pl.BlockSpec(memory_space=pl.ANY),
                  pl.BlockSpec((None, 1, F), wsel), pl.BlockSpec((None, 1, F), wsel),
                  pl.BlockSpec(memory_space=pl.ANY), pl.BlockSpec((None, 1, D), wsel)],
        out_specs=pl.BlockSpec((RB, D), yblk),
        scratch_shapes=[pltpu.VMEM((2, D, F2), F32), pltpu.VMEM((2, F, D), F32),
                        pltpu.VMEM((D, F), BF16), pltpu.VMEM((D, F), BF16), pltpu.VMEM((F, D), BF16),
                        pltpu.SemaphoreType.DMA((2, 2))],
    )
    return pl.pallas_call(
        functools.partial(_expert_kernel, layer=layer),
        grid_spec=grid_spec,
        out_shape=jax.ShapeDtypeStruct((n_rows, D), F32),
        compiler_params=_cparams(("arbitrary",), EXPERT_VMEM_LIMIT_BYTES),
        name="moe_experts",
    )(blk_e, n_valid, blk_next, blk_parity, x_rows, w_gate_up, b_glu, b_lin, w_down, b_down)


def _combine_kernel(c8_ref, lo_ref, gd_ref, y_ref, lpos_ref, gate_ref, x_ref, g_ref, b_ref, o_ref,
                    ys, sem, *, tm, K, E, LR, CH):
    i = pl.program_id(0)
    n_tiles = pl.num_programs(0)
    slot = i % 2
    tables = (c8_ref, lo_ref, gd_ref)

    def seg_copy(tile, e, s):
        n, local, glob = _segment(tables, tile, e, E)
        return pltpu.make_async_copy(y_ref.at[pl.ds(glob, n), :], ys.at[s, pl.ds(local, n), :], sem.at[s])

    def for_segments(tile, s, fn):
        for e in range(E):
            @pl.when(c8_ref[tile * E + e] > 0)
            def _():
                fn(seg_copy(tile, e, s))

    @pl.when(i == 0)
    def _():
        ys[...] = jnp.zeros_like(ys)
        for_segments(0, 0, lambda cp: cp.start())

    @pl.when(i + 1 < n_tiles)
    def _():
        for_segments(i + 1, 1 - slot, lambda cp: cp.start())

    for_segments(i, slot, lambda cp: cp.wait())

    lpos = lpos_ref[...]
    gate = gate_ref[...]
    f = jnp.zeros((tm, x_ref.shape[1]), F32)
    for c in range(LR // CH):
        cols = lax.broadcasted_iota(I32, (tm, CH), 1) + c * CH
        w = jnp.zeros((tm, CH), F32)
        for r in range(K):
            w = w + jnp.where(cols == lpos[:, r:r + 1], gate[:, r:r + 1], 0.0)
        f = f + jnp.dot(w.astype(BF16), ys[slot, c * CH:(c + 1) * CH, :].astype(BF16),
                        preferred_element_type=F32)
    o_ref[...] = _layer_norm(DEEPNORM_ALPHA * x_ref[...] + f, g_ref[...], b_ref[...])


def _combine(y_rows, lpos_c, seg_tables, x1, gates_c, ln_g, ln_b):
    T, D = x1.shape
    K, E = MOE_TOP_K, MOE_EXPERTS
    tm = MOE_TILE
    LR = _local_rows(tm)
    kern = functools.partial(_combine_kernel, tm=tm, K=K, E=E, LR=LR, CH=SORT_CHUNK)
    grid_spec = pltpu.PrefetchScalarGridSpec(
        num_scalar_prefetch=3,
        grid=(T // tm,),
        in_specs=[pl.BlockSpec(memory_space=pl.ANY),
                  pl.BlockSpec((tm, K), lambda i, *_: (i, 0)),
                  pl.BlockSpec((tm, K), lambda i, *_: (i, 0)),
                  pl.BlockSpec((tm, D), lambda i, *_: (i, 0)),
                  pl.BlockSpec((1, D), lambda i, *_: (0, 0)), pl.BlockSpec((1, D), lambda i, *_: (0, 0))],
        out_specs=pl.BlockSpec((tm, D), lambda i, *_: (i, 0)),
        scratch_shapes=[pltpu.VMEM((2, LR, D), F32), pltpu.SemaphoreType.DMA((2,))],
    )
    return pl.pallas_call(
        kern,
        grid_spec=grid_spec,
        out_shape=jax.ShapeDtypeStruct((T, D), F32),
        compiler_params=_cparams(("arbitrary",)),
        name="moe_combine",
    )(*seg_tables, y_rows, lpos_c, gates_c, x1, ln_g.reshape(1, D), ln_b.reshape(1, D))


def _moe(x1, e_t, g_t, p_t, cnt, layer, w_gate_up, b_gate_up, w_down, b_down, ln_g, ln_b):
    T, D = x1.shape
    E, K, RB = MOE_EXPERTS, MOE_TOP_K, ROW_BLOCK
    tm = MOE_TILE
    n_tiles = T // tm
    n_blocks = -(-(T * K + n_tiles * E * (SUBLANES - 1) + E * (RB - 1)) // RB)
    n_rows = n_blocks * RB

    seg_len = (cnt[:, ::LANES].T.astype(I32) + SUBLANES - 1) // SUBLANES * SUBLANES
    local_off = jnp.cumsum(seg_len, axis=1) - seg_len
    group_rows = jnp.sum(seg_len, axis=0)
    padded = (group_rows + RB - 1) // RB * RB
    pad_end = jnp.cumsum(padded)
    pad_start = pad_end - padded
    global_off = pad_start[None, :] + jnp.cumsum(seg_len, axis=0) - seg_len
    seg_tables = (seg_len.reshape(-1), local_off.reshape(-1).astype(I32), global_off.reshape(-1).astype(I32))

    experts = jnp.arange(E, dtype=I32)
    e_tiles = e_t.reshape(K, n_tiles, tm)
    off_sel = jnp.sum(jnp.where(e_tiles[..., None] == experts, local_off[None, :, None, :], 0), axis=-1)
    lpos_t = (off_sel.reshape(K, T) + p_t).astype(I32)

    n_valid = (pad_end[-1] // RB).astype(I32)
    blk_ids = jnp.minimum(jnp.arange(n_blocks, dtype=I32), n_valid - 1)
    blk_e = jnp.minimum(jnp.sum((pad_end[None, :] <= blk_ids[:, None] * RB).astype(I32), axis=1), E - 1)
    zero_start = jnp.maximum(pad_end - RB, 0).astype(I32)
    nonempty = group_rows > 0
    zero_valid = nonempty.astype(I32)
    n_valid = n_valid.reshape(1)
    later = (experts[None, :] > experts[:, None]) & nonempty[None, :]
    next_e = jnp.min(jnp.where(later, experts[None, :], E), axis=1)
    next_e = jnp.where(next_e < E, next_e, -1).astype(I32)
    parity_e = ((jnp.cumsum(nonempty.astype(I32)) - nonempty.astype(I32)) % 2).astype(I32)
    at_block = blk_e[:, None] == experts[None, :]
    blk_next = jnp.sum(jnp.where(at_block, next_e[None, :], 0), axis=1).astype(I32)
    blk_parity = jnp.sum(jnp.where(at_block, parity_e[None, :], 0), axis=1).astype(I32)

    x_rows = _dispatch(x1, lpos_t, seg_tables, zero_start, zero_valid, n_valid, n_rows)
    y_rows = _experts(x_rows, blk_e, n_valid, blk_next, blk_parity, layer, w_gate_up, b_gate_up[:, None, 0::2],
                      b_gate_up[:, None, 1::2], w_down, b_down[:, None, :])
    return _combine(y_rows, lpos_t.T, seg_tables, x1, g_t.T, ln_g, ln_b)


def kernel(x, positions, attn_w_qkv, attn_w_o, mlstm_w_in, mlstm_b_gates, mlstm_norm_g, mlstm_w_out,
           ln_mix_g, ln_mix_b, ln_ffn_g, ln_ffn_b, router_w, router_b,
           w_gate_up, b_gate_up, w_down, b_down):
    B, S, D = x.shape
    T = B * S
    xt = x.reshape(T, D)
    NH = MLSTM_HEADS

    for layer in range(DEPTH):
        slot = layer // 2
        if layer % 2 == 0:
            hd = attn_w_o.shape[1]
            qkv = _qkv_rope(xt, positions.reshape(T, 1), attn_w_qkv[slot].astype(BF16), hd)
            mixed = _moba_attention(qkv, B, S, hd)
            w_out = attn_w_o[slot].astype(BF16)
        else:
            vw = mlstm_w_out.shape[1]
            qw = vw // 2
            w_in = mlstm_w_in[slot]
            dqk = qw // NH
            col_scale = jnp.concatenate([jnp.ones((qw,), F32), jnp.full((qw,), dqk ** -0.5, F32),
                                         jnp.ones((2 * vw,), F32)])
            w_main = (w_in[:, :2 * qw + 2 * vw] * col_scale).astype(BF16)
            w_gate = jnp.pad(w_in[:, 2 * qw + 2 * vw:], ((0, 0), (0, LANES - 2 * NH))).astype(BF16)
            proj, gates = _mlstm_proj(xt, w_main, w_gate)
            bg = mlstm_b_gates[slot]
            b_col = jnp.pad(bg, (0, LANES - 2 * NH)).reshape(1, LANES)
            b_row = bg.reshape(2 * NH, 1)
            mixed = _mlstm_mix(proj, gates, gates[:, :2 * NH].T, b_col, b_row,
                               mlstm_norm_g[slot].reshape(1, vw), B, S)
            w_out = mlstm_w_out[slot].astype(BF16)
        x1, e_t, g_t, p_t, cnt = _mix_out_router(mixed, w_out, xt, ln_mix_g[layer], ln_mix_b[layer],
                                                 router_w[layer], router_b[layer])
        xt = _moe(x1, e_t, g_t, p_t, cnt, layer, w_gate_up, b_gate_up[layer], w_down, b_down[layer],
                  ln_ffn_g[layer], ln_ffn_b[layer])
    return xt.reshape(B, S, D)
```

```python
import functools

import jax
import jax.numpy as jnp
from jax import lax
from jax.experimental import pallas as pl
from jax.experimental.pallas import tpu as pltpu

F32 = jnp.float32
BF16 = jnp.bfloat16
I32 = jnp.int32

DEPTH = 2
ATTN_HEAD_DIM = 64
ROT_DIM = ATTN_HEAD_DIM // 4
ROPE_THETA = 500000.0
MOBA_BLOCK = 256
MOBA_TOP_K = 3
MLSTM_HEADS = 4
MOE_EXPERTS = 32
MOE_TOP_K = 4
SWIGLU_LIMIT = 7.0
SWIGLU_ALPHA = 1.702
DEEPNORM_ALPHA = (2 * DEPTH) ** 0.25
LN_EPS = 1e-5

LANES = 128
SUBLANES = 8
BF16_SUBLANES = 16
MXU_DIM = 256
VMEM_LIMIT_BYTES = 48 * 1024 * 1024
EXPERT_VMEM_LIMIT_BYTES = 58 * 1024 * 1024

PROJ_TM = 512
PROJ_TN = 512
MOE_TILE = 512
MLSTM_L = 256
ROW_BLOCK = 512
SORT_CHUNK = 256

NT_DIMS = (((1,), (1,)), ((), ()))
TN_DIMS = (((0,), (0,)), ((), ()))
NEG_INF = float("-inf")
MASKED = -1e30
LOG2_E = 1.4426950408889634


def _cparams(sem, vmem=VMEM_LIMIT_BYTES):
    return pltpu.CompilerParams(dimension_semantics=sem, vmem_limit_bytes=vmem)


def _local_rows(tm):
    rows = tm * MOE_TOP_K + MOE_EXPERTS * SUBLANES
    return -(-rows // SORT_CHUNK) * SORT_CHUNK


def _qkv_rope_kernel(pos_ref, invf_ref, x_ref, w_ref, o_ref, *, n_rope_chunks, n_q_chunks, tn):
    ang = pos_ref[...].astype(F32) * invf_ref[...]
    d = lax.broadcasted_iota(I32, ang.shape, 1) & (ATTN_HEAD_DIM - 1)
    cos = jnp.cos(ang)
    sin = jnp.sin(ang)
    half = ROT_DIM // 2
    c_tab = jnp.where(d < ROT_DIM, cos, 1.0)
    s_up = jnp.where(d < half, -sin, 0.0)
    s_dn = jnp.where((d >= half) & (d < ROT_DIM), sin, 0.0)

    xb = x_ref[...].astype(BF16)
    for c in range(w_ref.shape[1] // tn):
        acc = jnp.dot(xb, w_ref[:, c * tn:(c + 1) * tn], preferred_element_type=F32)
        if c >= n_rope_chunks:
            o_ref[:, c * tn:(c + 1) * tn] = acc.astype(o_ref.dtype)
            continue
        scale = ATTN_HEAD_DIM ** -0.5 * LOG2_E if c < n_q_chunks else 1.0
        for s in range(tn // LANES):
            blk = acc[:, s * LANES:(s + 1) * LANES]
            r = (blk * c_tab + pltpu.roll(blk, LANES - half, 1) * s_up + pltpu.roll(blk, half, 1) * s_dn)
            lo = c * tn + s * LANES
            o_ref[:, lo:lo + LANES] = (r * scale).astype(o_ref.dtype)


def _qkv_rope(x, pos, w_bf16, hd):
    T, D = x.shape
    N = w_bf16.shape[1]
    tm, tn = PROJ_TM, PROJ_TN
    inv_freq = ROPE_THETA ** (-jnp.arange(0, ROT_DIM, 2, dtype=F32) / ROT_DIM)
    lane_d = jnp.arange(LANES) % ATTN_HEAD_DIM
    invf = jnp.where(lane_d < ROT_DIM, inv_freq[lane_d % (ROT_DIM // 2)], 0.0).reshape(1, LANES).astype(F32)
    kern = functools.partial(_qkv_rope_kernel, n_rope_chunks=2 * hd // tn, n_q_chunks=hd // tn, tn=tn)
    return pl.pallas_call(
        kern,
        grid=(T // tm,),
        in_specs=[pl.BlockSpec((tm, 1), lambda i: (i, 0)),
                  pl.BlockSpec((1, LANES), lambda i: (0, 0)),
                  pl.BlockSpec((tm, D), lambda i: (i, 0)),
                  pl.BlockSpec((D, N), lambda i: (0, 0))],
        out_specs=pl.BlockSpec((tm, N), lambda i: (i, 0)),
        out_shape=jax.ShapeDtypeStruct((T, N), BF16),
        compiler_params=_cparams(("parallel",)),
        name="qkv_rope",
    )(pos, invf, x, w_bf16)


def _moba_kernel(q_ref, k_ref, v_ref, o_ref, kmh_ref, kml_ref, vt_ref, q2t_ref, bias_ref,
                 m_ref, acc_ref, *score_bufs, nb, blk):
    s_refs, smax_refs = score_bufs[:4], score_bufs[4:]
    i = pl.program_id(2)
    hd = ATTN_HEAD_DIM
    nbp = kmh_ref.shape[0]
    va = vt_ref.shape[2]

    @pl.when(i == 0)
    def _():
        rows = [jnp.mean(k_ref[j * blk:(j + 1) * blk, :].astype(F32), axis=0, keepdims=True)
                for j in range(nb)]
        if nbp > nb:
            rows.append(jnp.zeros((nbp - nb, LANES), F32))
        km = jnp.concatenate(rows, axis=0)
        hi = km.astype(BF16)
        kmh_ref[...] = hi
        kml_ref[...] = (km - hi.astype(F32)).astype(BF16)
        sub_v = lax.broadcasted_iota(I32, (va - hd, blk), 0)
        ones_row = jnp.where(sub_v == 0, 1.0, 0.0).astype(BF16)
        for j in range(nb):
            vt = v_ref[j * blk:(j + 1) * blk, :].astype(F32).T.astype(BF16)
            vt_ref[j, 0] = jnp.concatenate([vt[:hd], ones_row], axis=0)
            vt_ref[j, 1] = jnp.concatenate([vt[hd:], ones_row], axis=0)

    qt = q_ref[...].astype(F32).T
    sub_q = lax.broadcasted_iota(I32, qt.shape, 0)
    zero = jnp.zeros_like(qt)
    q2t = jnp.concatenate([jnp.where(sub_q < hd, qt, zero), jnp.where(sub_q >= hd, qt, zero)],
                          axis=1).astype(BF16)
    q2t_ref[...] = q2t

    gate = (jnp.dot(kmh_ref[...], q2t, preferred_element_type=F32)
            + jnp.dot(kml_ref[...], q2t, preferred_element_type=F32))
    sub = lax.broadcasted_iota(I32, gate.shape, 0)
    sub_f = sub.astype(F32)
    g = jnp.where(sub < i, gate, NEG_INF)
    sel = jnp.zeros(gate.shape, F32)
    for _ in range(MOBA_TOP_K):
        mx = jnp.max(g, axis=0, keepdims=True)
        idx = jnp.min(jnp.where(g == mx, sub_f, float(nbp)), axis=0, keepdims=True)
        hit = (sub_f == idx) & (mx > NEG_INF)
        sel = jnp.where(hit, 1.0, sel)
        g = jnp.where(hit, NEG_INF, g)
    bias_ref[...] = jnp.where(sel > 0.0, 0.0, MASKED)

    def block_bias(pos, j):
        return jnp.where(pos == 0, 0.0, bias_ref[pl.ds(j, 1), :])

    def produce(pos, buf):
        s_ref, smax_ref = buf
        if isinstance(pos, int) and pos == 0:
            ki = k_ref[pl.ds(pl.multiple_of(i * blk, blk), blk), :]
            s = jnp.dot(ki, q2t, preferred_element_type=F32)
            key = lax.broadcasted_iota(I32, s.shape, 0)
            qry = lax.broadcasted_iota(I32, s.shape, 1) & (blk - 1)
            s = jnp.where(key <= qry, s, MASKED)
            smax = jnp.max(s, axis=0, keepdims=True)
        else:
            j = jnp.minimum(pos - 1, i)
            kj = k_ref[pl.ds(pl.multiple_of(j * blk, blk), blk), :]
            s = jnp.dot(kj, q2t_ref[...], preferred_element_type=F32)
            smax = jnp.max(s, axis=0, keepdims=True) + block_bias(pos, j)
        s_ref[...] = s
        smax_ref[...] = smax

    def consume(pos, buf):
        s_ref, smax_ref = buf
        j = jnp.where(pos == 0, i, jnp.minimum(pos - 1, i))
        m_old = m_ref[...]
        m_new = jnp.maximum(m_old, smax_ref[...])
        a = jnp.exp2(m_old - m_new)
        pb = jnp.exp2(s_ref[...] - (m_new - block_bias(pos, j))).astype(BF16)
        upd = jnp.concatenate([jnp.dot(vt_ref[j, 0], pb[:, :blk], preferred_element_type=F32),
                               jnp.dot(vt_ref[j, 1], pb[:, blk:], preferred_element_type=F32)], axis=1)
        acc_ref[...] = a * acc_ref[...] + upd
        m_ref[...] = m_new

    m_ref[...] = jnp.full(m_ref.shape, MASKED, F32)
    acc_ref[...] = jnp.zeros_like(acc_ref)
    buf_a, buf_b, buf_c, buf_d = [(s_refs[n], smax_refs[n]) for n in range(4)]
    produce(0, buf_a)
    produce(1, buf_b)

    def body(t, carry):
        base = 4 * t
        produce(base + 2, buf_c)
        produce(base + 3, buf_d)
        consume(base, buf_a)
        consume(base + 1, buf_b)
        produce(base + 4, buf_a)
        produce(base + 5, buf_b)
        consume(base + 2, buf_c)
        consume(base + 3, buf_d)
        return carry

    full = (i + 1) // 4
    left = (i + 1) % 4
    lax.fori_loop(0, full, body, 0)
    base = 4 * full

    @pl.when(left == 1)
    def _():
        consume(base, buf_a)

    @pl.when(left == 2)
    def _():
        consume(base, buf_a)
        consume(base + 1, buf_b)

    @pl.when(left == 3)
    def _():
        produce(base + 2, buf_c)
        consume(base, buf_a)
        consume(base + 1, buf_b)
        consume(base + 2, buf_c)

    acc = acc_ref[...]
    out_t = acc[:hd] / acc[hd:hd + 1]
    o_ref[...] = jnp.concatenate([out_t[:, :blk], out_t[:, blk:]], axis=0).T.astype(o_ref.dtype)


def _moba_attention(qkv, B, S, hd):
    T = B * S
    blk = MOBA_BLOCK
    nb = S // blk
    nbp = -(-nb // BF16_SUBLANES) * BF16_SUBLANES
    va = ATTN_HEAD_DIM + BF16_SUBLANES
    n_pairs = hd // LANES
    kern = functools.partial(_moba_kernel, nb=nb, blk=blk)
    return pl.pallas_call(
        kern,
        grid=(B, n_pairs, nb),
        in_specs=[pl.BlockSpec((blk, LANES), lambda b, h, i: (b * nb + i, h)),
                  pl.BlockSpec((S, LANES), lambda b, h, i: (b, n_pairs + h)),
                  pl.BlockSpec((S, LANES), lambda b, h, i: (b, 2 * n_pairs + h))],
        out_specs=pl.BlockSpec((blk, LANES), lambda b, h, i: (b * nb + i, h)),
        out_shape=jax.ShapeDtypeStruct((T, hd), BF16),
        scratch_shapes=[pltpu.VMEM((nbp, LANES), BF16), pltpu.VMEM((nbp, LANES), BF16),
                        pltpu.VMEM((nb, 2, va, blk), BF16),
                        pltpu.VMEM((LANES, 2 * blk), BF16), pltpu.VMEM((nbp, 2 * blk), F32),
                        pltpu.VMEM((1, 2 * blk), F32), pltpu.VMEM((va, 2 * blk), F32)]
                       + [pltpu.VMEM((blk, 2 * blk), F32)] * 4 + [pltpu.VMEM((1, 2 * blk), F32)] * 4,
        compiler_params=_cparams(("parallel", "parallel", "arbitrary")),
        name="moba_attention",
    )(qkv, qkv, qkv)


def _mlstm_proj_kernel(x_ref, w_ref, wg_ref, o_ref, g_ref, *, tn):
    xb = x_ref[...].astype(BF16)
    for c in range(w_ref.shape[1] // tn):
        o_ref[:, c * tn:(c + 1) * tn] = jnp.dot(xb, w_ref[:, c * tn:(c + 1) * tn],
                                                preferred_element_type=F32).astype(o_ref.dtype)
    g_ref[...] = jnp.dot(xb, wg_ref[...], preferred_element_type=F32)


def _mlstm_proj(x, w_main, w_gate):
    T, D = x.shape
    N = w_main.shape[1]
    tm = PROJ_TM
    return pl.pallas_call(
        functools.partial(_mlstm_proj_kernel, tn=PROJ_TN),
        grid=(T // tm,),
        in_specs=[pl.BlockSpec((tm, D), lambda i: (i, 0)),
                  pl.BlockSpec((D, N), lambda i: (0, 0)),
                  pl.BlockSpec((D, LANES), lambda i: (0, 0))],
        out_specs=[pl.BlockSpec((tm, N), lambda i: (i, 0)),
                   pl.BlockSpec((tm, LANES), lambda i: (i, 0))],
        out_shape=[jax.ShapeDtypeStruct((T, N), BF16), jax.ShapeDtypeStruct((T, LANES), F32)],
        compiler_params=_cparams(("parallel",)),
        name="mlstm_proj",
    )(x, w_main, w_gate)


def _log_sigmoid(x):
    return jnp.minimum(x, 0.0) - jnp.log(1.0 + jnp.exp(-jnp.abs(x)))


def _mlstm_kernel(q_ref, k_ref, v_ref, o_ref, gc_ref, gr_ref, bc_ref, br_ref, ng_ref, out_ref,
                  C_ref, n_ref, m_ref, *, L, NH, DQK, DV):
    c = pl.program_id(1)

    @pl.when(c == 0)
    def _():
        C_ref[...] = jnp.zeros_like(C_ref)
        n_ref[...] = jnp.zeros_like(n_ref)
        m_ref[...] = jnp.zeros_like(m_ref)

    gcol = gc_ref[...] + bc_ref[...]
    grow = gr_ref[...] + br_ref[...]
    lane = lax.broadcasted_iota(I32, gcol.shape, 1)
    lf_col = jnp.where((lane >= NH) & (lane < 2 * NH), _log_sigmoid(gcol), 0.0)
    lf_row = _log_sigmoid(grow)
    row = lax.broadcasted_iota(I32, (L, L), 0)
    col = lax.broadcasted_iota(I32, (L, L), 1)
    causal = col <= row
    tri = jnp.where(causal, 1.0, 0.0).astype(BF16)

    def pieces(a):
        out = []
        for _ in range(3):
            p = a.astype(BF16)
            out.append(p)
            a = a - p.astype(F32)
        return out

    b_cols = sum(jnp.dot(tri, p, preferred_element_type=F32) for p in pieces(lf_col))
    b_rows = sum(lax.dot_general(p, tri, NT_DIMS, preferred_element_type=F32) for p in pieces(lf_row))

    for h in range(NH):
        i_row = grow[h:h + 1, :]
        i_col = gcol[:, h:h + 1]
        b_row = b_rows[NH + h:NH + h + 1, :]
        b_col = b_cols[:, NH + h:NH + h + 1]
        m_prev = m_ref[h:h + 1, 0:1]
        qh = q_ref[:, h * DQK:(h + 1) * DQK]
        kh = k_ref[:, h * DQK:(h + 1) * DQK]
        vh = v_ref[:, h * DV:(h + 1) * DV]

        D = jnp.where(causal, b_col - b_row + i_row, NEG_INF)
        g = b_col + m_prev
        m_t = jnp.maximum(g, jnp.max(D, axis=1, keepdims=True))
        w_inter = jnp.exp(g - m_t)
        qk = lax.dot_general(qh, kh, NT_DIMS, preferred_element_type=F32)
        A = jnp.exp(D - m_t) * qk
        num = (w_inter * jnp.dot(qh, C_ref[h].astype(BF16), preferred_element_type=F32)
               + jnp.dot(A.astype(BF16), vh, preferred_element_type=F32))
        qn = jnp.sum(qh.astype(F32) * n_ref[h:h + 1, :], axis=1, keepdims=True)
        den = w_inter * qn + jnp.sum(A, axis=1, keepdims=True)
        hh = num / jnp.maximum(jnp.abs(den), jnp.exp(-m_t))

        m_new = m_t[L - 1:L, :]
        b_last = b_col[L - 1:L, :]
        decay = jnp.exp(b_last + m_prev - m_new)
        w_s = jnp.exp(b_last - b_col + i_col - m_new)
        kw = kh.astype(F32) * w_s
        C_ref[h] = decay * C_ref[h] + lax.dot_general(kw.astype(BF16), vh, TN_DIMS,
                                                      preferred_element_type=F32)
        n_ref[h:h + 1, :] = decay * n_ref[h:h + 1, :] + jnp.sum(kw, axis=0, keepdims=True)
        m_ref[h:h + 1, :] = jnp.broadcast_to(m_new, (1, LANES))

        mu = jnp.mean(hh, axis=1, keepdims=True)
        hc = hh - mu
        var = jnp.mean(hc * hc, axis=1, keepdims=True)
        hn = hc * lax.rsqrt(var + LN_EPS) * ng_ref[:, h * DV:(h + 1) * DV]
        og = jax.nn.sigmoid(o_ref[:, h * DV:(h + 1) * DV].astype(F32))
        out_ref[:, h * DV:(h + 1) * DV] = (og * hn).astype(out_ref.dtype)


def _mlstm_mix(proj, gates, gates_t, b_col, b_row, norm_g, B, S):
    T = B * S
    NH = MLSTM_HEADS
    DV = norm_g.shape[1] // NH
    DQK = DV // 2
    L = MLSTM_L
    nc = S // L
    qw = NH * DQK
    vw = NH * DV
    kern = functools.partial(_mlstm_kernel, L=L, NH=NH, DQK=DQK, DV=DV)
    return pl.pallas_call(
        kern,
        grid=(B, nc),
        in_specs=[pl.BlockSpec((L, qw), lambda b, c: (b * nc + c, 0)),
                  pl.BlockSpec((L, qw), lambda b, c: (b * nc + c, 1)),
                  pl.BlockSpec((L, vw), lambda b, c: (b * nc + c, 2 * qw // vw)),
                  pl.BlockSpec((L, vw), lambda b, c: (b * nc + c, 2 * qw // vw + 1)),
                  pl.BlockSpec((L, LANES), lambda b, c: (b * nc + c, 0)),
                  pl.BlockSpec((SUBLANES, L), lambda b, c: (0, b * nc + c)),
                  pl.BlockSpec((1, LANES), lambda b, c: (0, 0)),
                  pl.BlockSpec((SUBLANES, 1), lambda b, c: (0, 0)),
                  pl.BlockSpec((1, vw), lambda b, c: (0, 0))],
        out_specs=pl.BlockSpec((L, vw), lambda b, c: (b * nc + c, 0)),
        out_shape=jax.ShapeDtypeStruct((T, vw), BF16),
        scratch_shapes=[pltpu.VMEM((NH, DQK, DV), F32), pltpu.VMEM((SUBLANES, DQK), F32),
                        pltpu.VMEM((SUBLANES, LANES), F32)],
        compiler_params=_cparams(("parallel", "arbitrary")),
        name="mlstm_mix",
    )(proj, proj, proj, proj, gates, gates_t, b_col, b_row, norm_g)


def _layer_norm(z, g, b):
    mu = jnp.mean(z, axis=-1, keepdims=True)
    zc = z - mu
    var = jnp.mean(zc * zc, axis=-1, keepdims=True)
    return zc * lax.rsqrt(var + LN_EPS) * g + b


def _mix_out_router_kernel(y_ref, w_ref, x_ref, g_ref, b_ref, rwt_ref, rb_ref,
                           x1_ref, et_ref, gt_ref, pt_ref, cnt_ref, *, tm, E, K):
    y = jnp.dot(y_ref[...], w_ref[...], preferred_element_type=F32)
    x1 = _layer_norm(DEEPNORM_ALPHA * x_ref[...] + y, g_ref[...], b_ref[...])
    x1_ref[...] = x1

    def split(a):
        hi = a.astype(BF16)
        return hi, (a - hi.astype(F32)).astype(BF16)

    x_hi, x_lo = split(x1)
    w_hi, w_lo = split(rwt_ref[...])
    nt = lambda a, b: lax.dot_general(a, b, NT_DIMS, preferred_element_type=F32)
    logits = nt(w_hi, x_hi) + nt(w_lo, x_hi) + nt(w_hi, x_lo) + rb_ref[...]
    sub = lax.broadcasted_iota(I32, logits.shape, 0).astype(F32)
    vals, hits = [], []
    cur = logits
    for _ in range(K):
        mx = jnp.max(cur, axis=0, keepdims=True)
        idx = jnp.min(jnp.where(cur == mx, sub, float(E)), axis=0, keepdims=True)
        hit = sub == idx
        vals.append(mx)
        hits.append(hit)
        cur = jnp.where(hit, NEG_INF, cur)

    chosen = jnp.zeros(logits.shape, F32)
    for hit in hits:
        chosen = jnp.where(hit, 1.0, chosen)
    r_ = lax.broadcasted_iota(I32, (tm, tm), 0)
    c_ = lax.broadcasted_iota(I32, (tm, tm), 1)
    before = (r_ < c_).astype(BF16)
    pos = jnp.dot(chosen.astype(BF16), before, preferred_element_type=F32)
    cnt_ref[...] = jnp.broadcast_to(jnp.sum(chosen, axis=1, keepdims=True), cnt_ref.shape)

    ex = [jnp.exp(v - vals[0]) for v in vals]
    den = ex[0]
    for e_ in ex[1:]:
        den = den + e_
    for r in range(K):
        gt_ref[r:r + 1, :] = ex[r] / den
        et_ref[r:r + 1, :] = jnp.sum(jnp.where(hits[r], sub, 0.0), axis=0, keepdims=True).astype(I32)
        pt_ref[r:r + 1, :] = jnp.sum(jnp.where(hits[r], pos, 0.0), axis=0, keepdims=True).astype(I32)


def _mix_out_router(y, w_bf16, x, ln_g, ln_b, router_w, router_b):
    T, D = x.shape
    Kin = y.shape[1]
    E, K = MOE_EXPERTS, MOE_TOP_K
    tm = MOE_TILE
    kern = functools.partial(_mix_out_router_kernel, tm=tm, E=E, K=K)
    n_tiles = T // tm
    row = lambda i: (i, 0)
    fixed = lambda i: (0, 0)
    tok = lambda i: (0, i)
    return pl.pallas_call(
        kern,
        grid=(n_tiles,),
        in_specs=[pl.BlockSpec((tm, Kin), row), pl.BlockSpec((Kin, D), fixed),
                  pl.BlockSpec((tm, D), row), pl.BlockSpec((1, D), fixed), pl.BlockSpec((1, D), fixed),
                  pl.BlockSpec((E, D), fixed), pl.BlockSpec((E, 1), fixed)],
        out_specs=[pl.BlockSpec((tm, D), row), pl.BlockSpec((K, tm), tok), pl.BlockSpec((K, tm), tok),
                   pl.BlockSpec((K, tm), tok), pl.BlockSpec((E, LANES), tok)],
        out_shape=[jax.ShapeDtypeStruct((T, D), F32), jax.ShapeDtypeStruct((K, T), I32),
                   jax.ShapeDtypeStruct((K, T), F32), jax.ShapeDtypeStruct((K, T), I32),
                   jax.ShapeDtypeStruct((E, n_tiles * LANES), F32)],
        compiler_params=_cparams(("parallel",)),
        name="mix_out_router",
    )(y, w_bf16, x, ln_g.reshape(1, D), ln_b.reshape(1, D), router_w.T, router_b.reshape(E, 1))


def _segment(tables, tile, e, E):
    c8_ref, lo_ref, gd_ref = tables
    idx = tile * E + e
    n = pl.multiple_of(c8_ref[idx], SUBLANES)
    local = pl.multiple_of(lo_ref[idx], SUBLANES)
    glob = pl.multiple_of(gd_ref[idx], SUBLANES)
    return n, local, glob


def _dispatch_kernel(c8_ref, lo_ref, gd_ref, zs_ref, zv_ref, nv_ref, lpos_ref, x_ref, xr_ref,
                     perm_s, sorted_s, zbuf, sem, zsem, *, tm, K, E, RB, n_blocks, LR, CH):
    i = pl.program_id(0)
    tables = (c8_ref, lo_ref, gd_ref)

    def zero_copy(start):
        return pltpu.make_async_copy(zbuf, xr_ref.at[pl.ds(pl.multiple_of(start, RB), RB), :], zsem)

    @pl.when(i == 0)
    def _():
        zbuf[...] = jnp.zeros_like(zbuf)
        for e in range(E):
            @pl.when(zv_ref[e] > 0)
            def _():
                zero_copy(zs_ref[e]).start()

        def tail_start(p, carry):
            zero_copy(p * RB).start()
            return carry

        def tail_wait(p, carry):
            zero_copy(p * RB).wait()
            return carry

        lax.fori_loop(nv_ref[0], n_blocks, tail_start, 0)
        for e in range(E):
            @pl.when(zv_ref[e] > 0)
            def _():
                zero_copy(zs_ref[e]).wait()
        lax.fori_loop(nv_ref[0], n_blocks, tail_wait, 0)

    slot = i % 2

    def seg_copy(tile, e, s):
        n, local, glob = _segment(tables, tile, e, E)
        return pltpu.make_async_copy(sorted_s.at[s, pl.ds(local, n), :], xr_ref.at[pl.ds(glob, n), :], sem.at[s])

    def for_segments(tile, s, fn):
        for e in range(E):
            @pl.when(c8_ref[tile * E + e] > 0)
            def _():
                fn(seg_copy(tile, e, s))

    lpos = lpos_ref[...]
    for c in range(LR // CH):
        rows = lax.broadcasted_iota(I32, (CH, tm), 0) + c * CH
        hit = jnp.zeros((CH, tm), F32)
        for r in range(K):
            hit = hit + jnp.where(rows == lpos[r:r + 1, :], 1.0, 0.0)
        perm_s[c * CH:(c + 1) * CH, :] = hit.astype(BF16)

    @pl.when(i >= 2)
    def _():
        for_segments(i - 2, slot, lambda cp: cp.wait())

    xb = x_ref[...].astype(BF16)
    for c in range(LR // CH):
        sorted_s[slot, c * CH:(c + 1) * CH, :] = jnp.dot(perm_s[c * CH:(c + 1) * CH, :], xb,
                                                         preferred_element_type=F32)
    for_segments(i, slot, lambda cp: cp.start())

    @pl.when(i == pl.num_programs(0) - 1)
    def _():
        @pl.when(i >= 1)
        def _():
            for_segments(i - 1, 1 - slot, lambda cp: cp.wait())
        for_segments(i, slot, lambda cp: cp.wait())


def _dispatch(x1, lpos_t, seg_tables, zero_start, zero_valid, n_valid, n_rows):
    T, D = x1.shape
    K, E, RB = MOE_TOP_K, MOE_EXPERTS, ROW_BLOCK
    tm = MOE_TILE
    LR = _local_rows(tm)
    kern = functools.partial(_dispatch_kernel, tm=tm, K=K, E=E, RB=RB, n_blocks=n_rows // RB, LR=LR, CH=SORT_CHUNK)
    grid_spec = pltpu.PrefetchScalarGridSpec(
        num_scalar_prefetch=6,
        grid=(T // tm,),
        in_specs=[pl.BlockSpec((K, tm), lambda i, *_: (0, i)),
                  pl.BlockSpec((tm, D), lambda i, *_: (i, 0))],
        out_specs=pl.BlockSpec(memory_space=pl.ANY),
        scratch_shapes=[pltpu.VMEM((LR, tm), BF16), pltpu.VMEM((2, LR, D), F32), pltpu.VMEM((RB, D), F32),
                        pltpu.SemaphoreType.DMA((2,)), pltpu.SemaphoreType.DMA],
    )
    return pl.pallas_call(
        kern,
        grid_spec=grid_spec,
        out_shape=jax.ShapeDtypeStruct((n_rows, D), F32),
        compiler_params=pltpu.CompilerParams(dimension_semantics=("arbitrary",),
                                             vmem_limit_bytes=VMEM_LIMIT_BYTES,
                                             has_side_effects=True),
        name="moe_dispatch",
    )(*seg_tables, zero_start, zero_valid, n_valid, lpos_t, x1)


def _expert_kernel(be_ref, nv_ref, nxt_ref, par_ref, x_ref, wgu_hbm, bg_ref, bl_ref, wd_hbm, bd_ref, y_ref,
                   wgu_buf, wd_buf, wg_s, wl_s, wd_s, sem, *, layer):
    p = pl.program_id(0)
    valid = p < nv_ref[0]
    fresh = (p == 0) | (be_ref[p] != be_ref[jnp.maximum(p - 1, 0)])
    half = MXU_DIM // 2

    def weight_copies(e, s):
        return (pltpu.make_async_copy(wgu_hbm.at[layer, e], wgu_buf.at[s], sem.at[0, s]),
                pltpu.make_async_copy(wd_hbm.at[layer, e], wd_buf.at[s], sem.at[1, s]))

    @pl.when(valid & fresh)
    def _():
        s = par_ref[p]

        @pl.when(p == 0)
        def _():
            for cp in weight_copies(be_ref[0], s):
                cp.start()

        @pl.when(nxt_ref[p] >= 0)
        def _():
            for cp in weight_copies(nxt_ref[p], 1 - s):
                cp.start()

        for cp in weight_copies(be_ref[p], s):
            cp.wait()

        r_ = lax.broadcasted_iota(I32, (MXU_DIM, MXU_DIM), 0)
        c_ = lax.broadcasted_iota(I32, (MXU_DIM, MXU_DIM), 1)
        src = jnp.where(c_ < half, 2 * c_, 2 * (c_ - half) + 1)
        perm = jnp.where(r_ == src, 1.0, 0.0).astype(BF16)
        for cb in range(wgu_buf.shape[2] // MXU_DIM):
            blk = wgu_buf[s, :, cb * MXU_DIM:(cb + 1) * MXU_DIM].astype(BF16)
            sep = jnp.dot(blk, perm, preferred_element_type=F32).astype(BF16)
            wg_s[:, cb * half:(cb + 1) * half] = sep[:, :half]
            wl_s[:, cb * half:(cb + 1) * half] = sep[:, half:]
        wd_s[...] = wd_buf[s].astype(BF16)

    @pl.when(valid)
    def _():
        x = x_ref[...].astype(BF16)
        hg = jnp.dot(x, wg_s[...], preferred_element_type=F32) + bg_ref[...]
        hl = jnp.dot(x, wl_s[...], preferred_element_type=F32) + bl_ref[...]
        xg = jnp.minimum(hg, SWIGLU_LIMIT)
        xl = jnp.clip(hl, -SWIGLU_LIMIT, SWIGLU_LIMIT)
        act = xg * jax.nn.sigmoid(SWIGLU_ALPHA * xg) * (xl + 1.0)
        y_ref[...] = jnp.dot(act.astype(BF16), wd_s[...], preferred_element_type=F32) + bd_ref[...]

    @pl.when(jnp.logical_not(valid))
    def _():
        y_ref[...] = jnp.zeros_like(y_ref)


def _experts(x_rows, blk_e, n_valid, blk_next, blk_parity, layer, w_gate_up, b_glu, b_lin, w_down, b_down):
    n_rows, D = x_rows.shape
    _, E, _, F2 = w_gate_up.shape
    F = F2 // 2
    RB = ROW_BLOCK
    n_blocks = n_rows // RB
    xblk = lambda p, be, nv, *_: (jnp.minimum(p, nv[0] - 1), 0)
    yblk = lambda p, *_: (p, 0)
    wsel = lambda p, be, *_: (be[p], 0, 0)
    grid_spec = pltpu.PrefetchScalarGridSpec(
        num_scalar_prefetch=4,
        grid=(n_blocks,),
        in_specs=[pl.BlockSpec((RB, D), xblk),
                  pl.BlockSpec(memory_space=pl.ANY),
                  pl.BlockSpec((None, 1, F), wsel), pl.BlockSpec((None, 1, F), wsel),
                  pl.BlockSpec(memory_space=pl.ANY), pl.BlockSpec((None, 1, D), wsel)],
        out_specs=pl.BlockSpec((RB, D), yblk),
        scratch_shapes=[pltpu.VMEM((2, D, F2), F32), pltpu.VMEM((2, F, D), F32),
                        pltpu.VMEM((D, F), BF16), pltpu.VMEM((D, F), BF16), pltpu.VMEM((F, D), BF16),
                        pltpu.SemaphoreType.DMA((2, 2))],
    )
    return pl.pallas_call(
        functools.partial(_expert_kernel, layer=layer),
        grid_spec=grid_spec,
        out_shape=jax.ShapeDtypeStruct((n_rows, D), F32),
        compiler_params=_cparams(("arbitrary",), EXPERT_VMEM_LIMIT_BYTES),
        name="moe_experts",
    )(blk_e, n_valid, blk_next, blk_parity, x_rows, w_gate_up, b_glu, b_lin, w_down, b_down)


def _combine_kernel(c8_ref, lo_ref, gd_ref, y_ref, lpos_ref, gate_ref, x_ref, g_ref, b_ref, o_ref,
                    ys, sem, *, tm, K, E, LR, CH):
    i = pl.program_id(0)
    n_tiles = pl.num_programs(0)
    slot = i % 2
    tables = (c8_ref, lo_ref, gd_ref)

    def seg_copy(tile, e, s):
        n, local, glob = _segment(tables, tile, e, E)
        return pltpu.make_async_copy(y_ref.at[pl.ds(glob, n), :], ys.at[s, pl.ds(local, n), :], sem.at[s])

    def for_segments(tile, s, fn):
        for e in range(E):
            @pl.when(c8_ref[tile * E + e] > 0)
            def _():
                fn(seg_copy(tile, e, s))

    @pl.when(i == 0)
    def _():
        ys[...] = jnp.zeros_like(ys)
        for_segments(0, 0, lambda cp: cp.start())

    @pl.when(i + 1 < n_tiles)
    def _():
        for_segments(i + 1, 1 - slot, lambda cp: cp.start())

    for_segments(i, slot, lambda cp: cp.wait())

    lpos = lpos_ref[...]
    gate = gate_ref[...]
    f = jnp.zeros((tm, x_ref.shape[1]), F32)
    for c in range(LR // CH):
        cols = lax.broadcasted_iota(I32, (tm, CH), 1) + c * CH
        w = jnp.zeros((tm, CH), F32)
        for r in range(K):
            w = w + jnp.where(cols == lpos[:, r:r + 1], gate[:, r:r + 1], 0.0)
        f = f + jnp.dot(w.astype(BF16), ys[slot, c * CH:(c + 1) * CH, :].astype(BF16),
                        preferred_element_type=F32)
    o_ref[...] = _layer_norm(DEEPNORM_ALPHA * x_ref[...] + f, g_ref[...], b_ref[...])


def _combine(y_rows, lpos_c, seg_tables, x1, gates_c, ln_g, ln_b):
    T, D = x1.shape
    K, E = MOE_TOP_K, MOE_EXPERTS
    tm = MOE_TILE
    LR = _local_rows(tm)
    kern = functools.partial(_combine_kernel, tm=tm, K=K, E=E, LR=LR, CH=SORT_CHUNK)
    grid_spec = pltpu.PrefetchScalarGridSpec(
        num_scalar_prefetch=3,
        grid=(T // tm,),
        in_specs=[pl.BlockSpec(memory_space=pl.ANY),
                  pl.BlockSpec((tm, K), lambda i, *_: (i, 0)),
                  pl.BlockSpec((tm, K), lambda i, *_: (i, 0)),
                  pl.BlockSpec((tm, D), lambda i, *_: (i, 0)),
                  pl.BlockSpec((1, D), lambda i, *_: (0, 0)), pl.BlockSpec((1, D), lambda i, *_: (0, 0))],
        out_specs=pl.BlockSpec((tm, D), lambda i, *_: (i, 0)),
        scratch_shapes=[pltpu.VMEM((2, LR, D), F32), pltpu.SemaphoreType.DMA((2,))],
    )
    return pl.pallas_call(
        kern,
        grid_spec=grid_spec,
        out_shape=jax.ShapeDtypeStruct((T, D), F32),
        compiler_params=_cparams(("arbitrary",)),
        name="moe_combine",
    )(*seg_tables, y_rows, lpos_c, gates_c, x1, ln_g.reshape(1, D), ln_b.reshape(1, D))


def _moe(x1, e_t, g_t, p_t, cnt, layer, w_gate_up, b_gate_up, w_down, b_down, ln_g, ln_b):
    T, D = x1.shape
    E, K, RB = MOE_EXPERTS, MOE_TOP_K, ROW_BLOCK
    tm = MOE_TILE
    n_tiles = T // tm
    n_blocks = -(-(T * K + n_tiles * E * (SUBLANES - 1) + E * (RB - 1)) // RB)
    n_rows = n_blocks * RB

    seg_len = (cnt[:, ::LANES].T.astype(I32) + SUBLANES - 1) // SUBLANES * SUBLANES
    local_off = jnp.cumsum(seg_len, axis=1) - seg_len
    group_rows = jnp.sum(seg_len, axis=0)
    padded = (group_rows + RB - 1) // RB * RB
    pad_end = jnp.cumsum(padded)
    pad_start = pad_end - padded
    global_off = pad_start[None, :] + jnp.cumsum(seg_len, axis=0) - seg_len
    seg_tables = (seg_len.reshape(-1), local_off.reshape(-1).astype(I32), global_off.reshape(-1).astype(I32))

    experts = jnp.arange(E, dtype=I32)
    e_tiles = e_t.reshape(K, n_tiles, tm)
    off_sel = jnp.sum(jnp.where(e_tiles[..., None] == experts, local_off[None, :, None, :], 0), axis=-1)
    lpos_t = (off_sel.reshape(K, T) + p_t).astype(I32)

    n_valid = (pad_end[-1] // RB).astype(I32)
    blk_ids = jnp.minimum(jnp.arange(n_blocks, dtype=I32), n_valid - 1)
    blk_e = jnp.minimum(jnp.sum((pad_end[None, :] <= blk_ids[:, None] * RB).astype(I32), axis=1), E - 1)
    zero_start = jnp.maximum(pad_end - RB, 0).astype(I32)
    nonempty = group_rows > 0
    zero_valid = nonempty.astype(I32)
    n_valid = n_valid.reshape(1)
    later = (experts[None, :] > experts[:, None]) & nonempty[None, :]
    next_e = jnp.min(jnp.where(later, experts[None, :], E), axis=1)
    next_e = jnp.where(next_e < E, next_e, -1).astype(I32)
    parity_e = ((jnp.cumsum(nonempty.astype(I32)) - nonempty.astype(I32)) % 2).astype(I32)
    at_block = blk_e[:, None] == experts[None, :]
    blk_next = jnp.sum(jnp.where(at_block, next_e[None, :], 0), axis=1).astype(I32)
    blk_parity = jnp.sum(jnp.where(at_block, parity_e[None, :], 0), axis=1).astype(I32)

    x_rows = _dispatch(x1, lpos_t, seg_tables, zero_start, zero_valid, n_valid, n_rows)
    y_rows = _experts(x_rows, blk_e, n_valid, blk_next, blk_parity, layer, w_gate_up, b_gate_up[:, None, 0::2],
                      b_gate_up[:, None, 1::2], w_down, b_down[:, None, :])
    return _combine(y_rows, lpos_t.T, seg_tables, x1, g_t.T, ln_g, ln_b)


def kernel(x, positions, attn_w_qkv, attn_w_o, mlstm_w_in, mlstm_b_gates, mlstm_norm_g, mlstm_w_out,
           ln_mix_g, ln_mix_b, ln_ffn_g, ln_ffn_b, router_w, router_b,
           w_gate_up, b_gate_up, w_down, b_down):
    B, S, D = x.shape
    T = B * S
    xt = x.reshape(T, D)
    NH = MLSTM_HEADS

    for layer in range(DEPTH):
        slot = layer // 2
        if layer % 2 == 0:
            hd = attn_w_o.shape[1]
            qkv = _qkv_rope(xt, positions.reshape(T, 1), attn_w_qkv[slot].astype(BF16), hd)
            mixed = _moba_attention(qkv, B, S, hd)
            w_out = attn_w_o[slot].astype(BF16)
        else:
            vw = mlstm_w_out.shape[1]
            qw = vw // 2
            w_in = mlstm_w_in[slot]
            dqk = qw // NH
            col_scale = jnp.concatenate([jnp.ones((qw,), F32), jnp.full((qw,), dqk ** -0.5, F32),
                                         jnp.ones((2 * vw,), F32)])
            w_main = (w_in[:, :2 * qw + 2 * vw] * col_scale).astype(BF16)
            w_gate = jnp.pad(w_in[:, 2 * qw + 2 * vw:], ((0, 0), (0, LANES - 2 * NH))).astype(BF16)
            proj, gates = _mlstm_proj(xt, w_main, w_gate)
            bg = mlstm_b_gates[slot]
            b_col = jnp.pad(bg, (0, LANES - 2 * NH)).reshape(1, LANES)
            b_row = bg.reshape(2 * NH, 1)
            mixed = _mlstm_mix(proj, gates, gates[:, :2 * NH].T, b_col, b_row,
                               mlstm_norm_g[slot].reshape(1, vw), B, S)
            w_out = mlstm_w_out[slot].astype(BF16)
        x1, e_t, g_t, p_t, cnt = _mix_out_router(mixed, w_out, xt, ln_mix_g[layer], ln_mix_b[layer],
                                                 router_w[layer], router_b[layer])
        xt = _moe(x1, e_t, g_t, p_t, cnt, layer, w_gate_up, b_gate_up[layer], w_down, b_down[layer],
                  ln_ffn_g[layer], ln_ffn_b[layer])
    return xt.reshape(B, S, D)
```

```python
import functools

import jax
import jax.numpy as jnp
from jax import lax
from jax.experimental import pallas as pl
from jax.experimental.pallas import tpu as pltpu

F32 = jnp.float32
BF16 = jnp.bfloat16
I32 = jnp.int32

DEPTH = 2
ATTN_HEAD_DIM = 64
ROT_DIM = ATTN_HEAD_DIM // 4
ROPE_THETA = 500000.0
MOBA_BLOCK = 256
MOBA_TOP_K = 3
MLSTM_HEADS = 4
MOE_EXPERTS = 32
MOE_TOP_K = 4
SWIGLU_LIMIT = 7.0
SWIGLU_ALPHA = 1.702
DEEPNORM_ALPHA = (2 * DEPTH) ** 0.25
LN_EPS = 1e-5

LANES = 128
SUBLANES = 8
BF16_SUBLANES = 16
MXU_DIM = 256
VMEM_LIMIT_BYTES = 48 * 1024 * 1024
EXPERT_VMEM_LIMIT_BYTES = 58 * 1024 * 1024

PROJ_TM = 512
PROJ_TN = 512
MOE_TILE = 512
MLSTM_L = 256
ROW_BLOCK = 512
SORT_CHUNK = 256

NT_DIMS = (((1,), (1,)), ((), ()))
TN_DIMS = (((0,), (0,)), ((), ()))
NEG_INF = float("-inf")
MASKED = -1e30
LOG2_E = 1.4426950408889634


def _cparams(sem, vmem=VMEM_LIMIT_BYTES):
    return pltpu.CompilerParams(dimension_semantics=sem, vmem_limit_bytes=vmem)


def _local_rows(tm):
    rows = tm * MOE_TOP_K + MOE_EXPERTS * SUBLANES
    return -(-rows // SORT_CHUNK) * SORT_CHUNK


def _qkv_rope_kernel(pos_ref, invf_ref, x_ref, w_ref, o_ref, *, n_rope_chunks, n_q_chunks, tn):
    ang = pos_ref[...].astype(F32) * invf_ref[...]
    d = lax.broadcasted_iota(I32, ang.shape, 1) & (ATTN_HEAD_DIM - 1)
    cos = jnp.cos(ang)
    sin = jnp.sin(ang)
    half = ROT_DIM // 2
    c_tab = jnp.where(d < ROT_DIM, cos, 1.0)
    s_up = jnp.where(d < half, -sin, 0.0)
    s_dn = jnp.where((d >= half) & (d < ROT_DIM), sin, 0.0)

    xb = x_ref[...].astype(BF16)
    for c in range(w_ref.shape[1] // tn):
        acc = jnp.dot(xb, w_ref[:, c * tn:(c + 1) * tn], preferred_element_type=F32)
        if c >= n_rope_chunks:
            o_ref[:, c * tn:(c + 1) * tn] = acc.astype(o_ref.dtype)
            continue
        scale = ATTN_HEAD_DIM ** -0.5 * LOG2_E if c < n_q_chunks else 1.0
        for s in range(tn // LANES):
            blk = acc[:, s * LANES:(s + 1) * LANES]
            r = (blk * c_tab + pltpu.roll(blk, LANES - half, 1) * s_up + pltpu.roll(blk, half, 1) * s_dn)
            lo = c * tn + s * LANES
            o_ref[:, lo:lo + LANES] = (r * scale).astype(o_ref.dtype)


def _qkv_rope(x, pos, w_bf16, hd):
    T, D = x.shape
    N = w_bf16.shape[1]
    tm, tn = PROJ_TM, PROJ_TN
    inv_freq = ROPE_THETA ** (-jnp.arange(0, ROT_DIM, 2, dtype=F32) / ROT_DIM)
    lane_d = jnp.arange(LANES) % ATTN_HEAD_DIM
    invf = jnp.where(lane_d < ROT_DIM, inv_freq[lane_d % (ROT_DIM // 2)], 0.0).reshape(1, LANES).astype(F32)
    kern = functools.partial(_qkv_rope_kernel, n_rope_chunks=2 * hd // tn, n_q_chunks=hd // tn, tn=tn)
    return pl.pallas_call(
        kern,
        grid=(T // tm,),
        in_specs=[pl.BlockSpec((tm, 1), lambda i: (i, 0)),
                  pl.BlockSpec((1, LANES), lambda i: (0, 0)),
                  pl.BlockSpec((tm, D), lambda i: (i, 0)),
                  pl.BlockSpec((D, N), lambda i: (0, 0))],
        out_specs=pl.BlockSpec((tm, N), lambda i: (i, 0)),
        out_shape=jax.ShapeDtypeStruct((T, N), BF16),
        compiler_params=_cparams(("parallel",)),
        name="qkv_rope",
    )(pos, invf, x, w_bf16)


def _moba_kernel(q_ref, k_ref, v_ref, o_ref, kmh_ref, kml_ref, vt_ref, q2t_ref, bias_ref,
                 m_ref, acc_ref, *score_bufs, nb, blk):
    s_refs, smax_refs = score_bufs[:4], score_bufs[4:]
    i = pl.program_id(2)
    hd = ATTN_HEAD_DIM
    nbp = kmh_ref.shape[0]
    va = vt_ref.shape[2]

    @pl.when(i == 0)
    def _():
        rows = [jnp.mean(k_ref[j * blk:(j + 1) * blk, :].astype(F32), axis=0, keepdims=True)
                for j in range(nb)]
        if nbp > nb:
            rows.append(jnp.zeros((nbp - nb, LANES), F32))
        km = jnp.concatenate(rows, axis=0)
        hi = km.astype(BF16)
        kmh_ref[...] = hi
        kml_ref[...] = (km - hi.astype(F32)).astype(BF16)
        sub_v = lax.broadcasted_iota(I32, (va - hd, blk), 0)
        ones_row = jnp.where(sub_v == 0, 1.0, 0.0).astype(BF16)
        for j in range(nb):
            vt = v_ref[j * blk:(j + 1) * blk, :].astype(F32).T.astype(BF16)
            vt_ref[j, 0] = jnp.concatenate([vt[:hd], ones_row], axis=0)
            vt_ref[j, 1] = jnp.concatenate([vt[hd:], ones_row], axis=0)

    qt = q_ref[...].astype(F32).T
    sub_q = lax.broadcasted_iota(I32, qt.shape, 0)
    zero = jnp.zeros_like(qt)
    q2t = jnp.concatenate([jnp.where(sub_q < hd, qt, zero), jnp.where(sub_q >= hd, qt, zero)],
                          axis=1).astype(BF16)
    q2t_ref[...] = q2t

    gate = (jnp.dot(kmh_ref[...], q2t, preferred_element_type=F32)
            + jnp.dot(kml_ref[...], q2t, preferred_element_type=F32))
    sub = lax.broadcasted_iota(I32, gate.shape, 0)
    sub_f = sub.astype(F32)
    g = jnp.where(sub < i, gate, NEG_INF)
    sel = jnp.zeros(gate.shape, F32)
    for _ in range(MOBA_TOP_K):
        mx = jnp.max(g, axis=0, keepdims=True)
        idx = jnp.min(jnp.where(g == mx, sub_f, float(nbp)), axis=0, keepdims=True)
        hit = (sub_f == idx) & (mx > NEG_INF)
        sel = jnp.where(hit, 1.0, sel)
        g = jnp.where(hit, NEG_INF, g)
    bias_ref[...] = jnp.where(sel > 0.0, 0.0, MASKED)

    def block_bias(pos, j):
        return jnp.where(pos == 0, 0.0, bias_ref[pl.ds(j, 1), :])

    def produce(pos, buf):
        s_ref, smax_ref = buf
        if isinstance(pos, int) and pos == 0:
            ki = k_ref[pl.ds(pl.multiple_of(i * blk, blk), blk), :]
            s = jnp.dot(ki, q2t, preferred_element_type=F32)
            key = lax.broadcasted_iota(I32, s.shape, 0)
            qry = lax.broadcasted_iota(I32, s.shape, 1) & (blk - 1)
            s = jnp.where(key <= qry, s, MASKED)
            smax = jnp.max(s, axis=0, keepdims=True)
        else:
            j = jnp.minimum(pos - 1, i)
            kj = k_ref[pl.ds(pl.multiple_of(j * blk, blk), blk), :]
            s = jnp.dot(kj, q2t_ref[...], preferred_element_type=F32)
            smax = jnp.max(s, axis=0, keepdims=True) + block_bias(pos, j)
        s_ref[...] = s
        smax_ref[...] = smax

    def consume(pos, buf):
        s_ref, smax_ref = buf
        j = jnp.where(pos == 0, i, jnp.minimum(pos - 1, i))
        m_old = m_ref[...]
        m_new = jnp.maximum(m_old, smax_ref[...])
        a = jnp.exp2(m_old - m_new)
        pb = jnp.exp2(s_ref[...] - (m_new - block_bias(pos, j))).astype(BF16)
        upd = jnp.concatenate([jnp.dot(vt_ref[j, 0], pb[:, :blk], preferred_element_type=F32),
                               jnp.dot(vt_ref[j, 1], pb[:, blk:], preferred_element_type=F32)], axis=1)
        acc_ref[...] = a * acc_ref[...] + upd
        m_ref[...] = m_new

    m_ref[...] = jnp.full(m_ref.shape, MASKED, F32)
    acc_ref[...] = jnp.zeros_like(acc_ref)
    buf_a, buf_b, buf_c, buf_d = [(s_refs[n], smax_refs[n]) for n in range(4)]
    produce(0, buf_a)
    produce(1, buf_b)

    def body(t, carry):
        base = 4 * t
        produce(base + 2, buf_c)
        produce(base + 3, buf_d)
        consume(base, buf_a)
        consume(base + 1, buf_b)
        produce(base + 4, buf_a)
        produce(base + 5, buf_b)
        consume(base + 2, buf_c)
        consume(base + 3, buf_d)
        return carry

    full = (i + 1) // 4
    left = (i + 1) % 4
    lax.fori_loop(0, full, body, 0)
    base = 4 * full

    @pl.when(left == 1)
    def _():
        consume(base, buf_a)

    @pl.when(left == 2)
    def _():
        consume(base, buf_a)
        consume(base + 1, buf_b)

    @pl.when(left == 3)
    def _():
        produce(base + 2, buf_c)
        consume(base, buf_a)
        consume(base + 1, buf_b)
        consume(base + 2, buf_c)

    acc = acc_ref[...]
    out_t = acc[:hd] / acc[hd:hd + 1]
    o_ref[...] = jnp.concatenate([out_t[:, :blk], out_t[:, blk:]], axis=0).T.astype(o_ref.dtype)


def _moba_attention(qkv, B, S, hd):
    T = B * S
    blk = MOBA_BLOCK
    nb = S // blk
    nbp = -(-nb // BF16_SUBLANES) * BF16_SUBLANES
    va = ATTN_HEAD_DIM + BF16_SUBLANES
    n_pairs = hd // LANES
    kern = functools.partial(_moba_kernel, nb=nb, blk=blk)
    return pl.pallas_call(
        kern,
        grid=(B, n_pairs, nb),
        in_specs=[pl.BlockSpec((blk, LANES), lambda b, h, i: (b * nb + i, h)),
                  pl.BlockSpec((S, LANES), lambda b, h, i: (b, n_pairs + h)),
                  pl.BlockSpec((S, LANES), lambda b, h, i: (b, 2 * n_pairs + h))],
        out_specs=pl.BlockSpec((blk, LANES), lambda b, h, i: (b * nb + i, h)),
        out_shape=jax.ShapeDtypeStruct((T, hd), BF16),
        scratch_shapes=[pltpu.VMEM((nbp, LANES), BF16), pltpu.VMEM((nbp, LANES), BF16),
                        pltpu.VMEM((nb, 2, va, blk), BF16),
                        pltpu.VMEM((LANES, 2 * blk), BF16), pltpu.VMEM((nbp, 2 * blk), F32),
                        pltpu.VMEM((1, 2 * blk), F32), pltpu.VMEM((va, 2 * blk), F32)]
                       + [pltpu.VMEM((blk, 2 * blk), F32)] * 4 + [pltpu.VMEM((1, 2 * blk), F32)] * 4,
        compiler_params=_cparams(("parallel", "parallel", "arbitrary")),
        name="moba_attention",
    )(qkv, qkv, qkv)


def _mlstm_proj_kernel(x_ref, w_ref, wg_ref, o_ref, g_ref, *, tn):
    xb = x_ref[...].astype(BF16)
    for c in range(w_ref.shape[1] // tn):
        o_ref[:, c * tn:(c + 1) * tn] = jnp.dot(xb, w_ref[:, c * tn:(c + 1) * tn],
                                                preferred_element_type=F32).astype(o_ref.dtype)
    g_ref[...] = jnp.dot(xb, wg_ref[...], preferred_element_type=F32)


def _mlstm_proj(x, w_main, w_gate):
    T, D = x.shape
    N = w_main.shape[1]
    tm = PROJ_TM
    return pl.pallas_call(
        functools.partial(_mlstm_proj_kernel, tn=PROJ_TN),
        grid=(T // tm,),
        in_specs=[pl.BlockSpec((tm, D), lambda i: (i, 0)),
                  pl.BlockSpec((D, N), lambda i: (0, 0)),
                  pl.BlockSpec((D, LANES), lambda i: (0, 0))],
        out_specs=[pl.BlockSpec((tm, N), lambda i: (i, 0)),
                   pl.BlockSpec((tm, LANES), lambda i: (i, 0))],
        out_shape=[jax.ShapeDtypeStruct((T, N), BF16), jax.ShapeDtypeStruct((T, LANES), F32)],
        compiler_params=_cparams(("parallel",)),
        name="mlstm_proj",
    )(x, w_main, w_gate)


def _log_sigmoid(x):
    return jnp.minimum(x, 0.0) - jnp.log(1.0 + jnp.exp(-jnp.abs(x)))


def _mlstm_kernel(q_ref, k_ref, v_ref, o_ref, gc_ref, gr_ref, bc_ref, br_ref, ng_ref, out_ref,
                  C_ref, n_ref, m_ref, *, L, NH, DQK, DV):
    c = pl.program_id(1)

    @pl.when(c == 0)
    def _():
        C_ref[...] = jnp.zeros_like(C_ref)
        n_ref[...] = jnp.zeros_like(n_ref)
        m_ref[...] = jnp.zeros_like(m_ref)

    gcol = gc_ref[...] + bc_ref[...]
    grow = gr_ref[...] + br_ref[...]
    lane = lax.broadcasted_iota(I32, gcol.shape, 1)
    lf_col = jnp.where((lane >= NH) & (lane < 2 * NH), _log_sigmoid(gcol), 0.0)
    lf_row = _log_sigmoid(grow)
    row = lax.broadcasted_iota(I32, (L, L), 0)
    col = lax.broadcasted_iota(I32, (L, L), 1)
    causal = col <= row
    tri = jnp.where(causal, 1.0, 0.0).astype(BF16)

    def pieces(a):
        out = []
        for _ in range(3):
            p = a.astype(BF16)
            out.append(p)
            a = a - p.astype(F32)
        return out

    b_cols = sum(jnp.dot(tri, p, preferred_element_type=F32) for p in pieces(lf_col))
    b_rows = sum(lax.dot_general(p, tri, NT_DIMS, preferred_element_type=F32) for p in pieces(lf_row))

    for h in range(NH):
        i_row = grow[h:h + 1, :]
        i_col = gcol[:, h:h + 1]
        b_row = b_rows[NH + h:NH + h + 1, :]
        b_col = b_cols[:, NH + h:NH + h + 1]
        m_prev = m_ref[h:h + 1, 0:1]
        qh = q_ref[:, h * DQK:(h + 1) * DQK]
        kh = k_ref[:, h * DQK:(h + 1) * DQK]
        vh = v_ref[:, h * DV:(h + 1) * DV]

        D = jnp.where(causal, b_col - b_row + i_row, NEG_INF)
        g = b_col + m_prev
        m_t = jnp.maximum(g, jnp.max(D, axis=1, keepdims=True))
        w_inter = jnp.exp(g - m_t)
        qk = lax.dot_general(qh, kh, NT_DIMS, preferred_element_type=F32)
        A = jnp.exp(D - m_t) * qk
        num = (w_inter * jnp.dot(qh, C_ref[h].astype(BF16), preferred_element_type=F32)
               + jnp.dot(A.astype(BF16), vh, preferred_element_type=F32))
        qn = jnp.sum(qh.astype(F32) * n_ref[h:h + 1, :], axis=1, keepdims=True)
        den = w_inter * qn + jnp.sum(A, axis=1, keepdims=True)
        hh = num / jnp.maximum(jnp.abs(den), jnp.exp(-m_t))

        m_new = m_t[L - 1:L, :]
        b_last = b_col[L - 1:L, :]
        decay = jnp.exp(b_last + m_prev - m_new)
        w_s = jnp.exp(b_last - b_col + i_col - m_new)
        kw = kh.astype(F32) * w_s
        C_ref[h] = decay * C_ref[h] + lax.dot_general(kw.astype(BF16), vh, TN_DIMS,
                                                      preferred_element_type=F32)
        n_ref[h:h + 1, :] = decay * n_ref[h:h + 1, :] + jnp.sum(kw, axis=0, keepdims=True)
        m_ref[h:h + 1, :] = jnp.broadcast_to(m_new, (1, LANES))

        mu = jnp.mean(hh, axis=1, keepdims=True)
        hc = hh - mu
        var = jnp.mean(hc * hc, axis=1, keepdims=True)
        hn = hc * lax.rsqrt(var + LN_EPS) * ng_ref[:, h * DV:(h + 1) * DV]
        og = jax.nn.sigmoid(o_ref[:, h * DV:(h + 1) * DV].astype(F32))
        out_ref[:, h * DV:(h + 1) * DV] = (og * hn).astype(out_ref.dtype)


def _mlstm_mix(proj, gates, gates_t, b_col, b_row, norm_g, B, S):
    T = B * S
    NH = MLSTM_HEADS
    DV = norm_g.shape[1] // NH
    DQK = DV // 2
    L = MLSTM_L
    nc = S // L
    qw = NH * DQK
    vw = NH * DV
    kern = functools.partial(_mlstm_kernel, L=L, NH=NH, DQK=DQK, DV=DV)
    return pl.pallas_call(
        kern,
        grid=(B, nc),
        in_specs=[pl.BlockSpec((L, qw), lambda b, c: (b * nc + c, 0)),
                  pl.BlockSpec((L, qw), lambda b, c: (b * nc + c, 1)),
                  pl.BlockSpec((L, vw), lambda b, c: (b * nc + c, 2 * qw // vw)),
                  pl.BlockSpec((L, vw), lambda b, c: (b * nc + c, 2 * qw // vw + 1)),
                  pl.BlockSpec((L, LANES), lambda b, c: (b * nc + c, 0)),
                  pl.BlockSpec((SUBLANES, L), lambda b, c: (0, b * nc + c)),
                  pl.BlockSpec((1, LANES), lambda b, c: (0, 0)),
                  pl.BlockSpec((SUBLANES, 1), lambda b, c: (0, 0)),
                  pl.BlockSpec((1, vw), lambda b, c: (0, 0))],
        out_specs=pl.BlockSpec((L, vw), lambda b, c: (b * nc + c, 0)),
        out_shape=jax.ShapeDtypeStruct((T, vw), BF16),
        scratch_shapes=[pltpu.VMEM((NH, DQK, DV), F32), pltpu.VMEM((SUBLANES, DQK), F32),
                        pltpu.VMEM((SUBLANES, LANES), F32)],
        compiler_params=_cparams(("parallel", "arbitrary")),
        name="mlstm_mix",
    )(proj, proj, proj, proj, gates, gates_t, b_col, b_row, norm_g)


def _layer_norm(z, g, b):
    mu = jnp.mean(z, axis=-1, keepdims=True)
    zc = z - mu
    var = jnp.mean(zc * zc, axis=-1, keepdims=True)
    return zc * lax.rsqrt(var + LN_EPS) * g + b


def _mix_out_router_kernel(y_ref, w_ref, x_ref, g_ref, b_ref, rwt_ref, rb_ref,
                           x1_ref, et_ref, gt_ref, pt_ref, cnt_ref, *, tm, E, K):
    y = jnp.dot(y_ref[...], w_ref[...], preferred_element_type=F32)
    x1 = _layer_norm(DEEPNORM_ALPHA * x_ref[...] + y, g_ref[...], b_ref[...])
    x1_ref[...] = x1

    def split(a):
        hi = a.astype(BF16)
        return hi, (a - hi.astype(F32)).astype(BF16)

    x_hi, x_lo = split(x1)
    w_hi, w_lo = split(rwt_ref[...])
    nt = lambda a, b: lax.dot_general(a, b, NT_DIMS, preferred_element_type=F32)
    logits = nt(w_hi, x_hi) + nt(w_lo, x_hi) + nt(w_hi, x_lo) + rb_ref[...]
    sub = lax.broadcasted_iota(I32, logits.shape, 0).astype(F32)
    vals, hits = [], []
    cur = logits
    for _ in range(K):
        mx = jnp.max(cur, axis=0, keepdims=True)
        idx = jnp.min(jnp.where(cur == mx, sub, float(E)), axis=0, keepdims=True)
        hit = sub == idx
        vals.append(mx)
        hits.append(hit)
        cur = jnp.where(hit, NEG_INF, cur)

    chosen = jnp.zeros(logits.shape, F32)
    for hit in hits:
        chosen = jnp.where(hit, 1.0, chosen)
    r_ = lax.broadcasted_iota(I32, (tm, tm), 0)
    c_ = lax.broadcasted_iota(I32, (tm, tm), 1)
    before = (r_ < c_).astype(BF16)
    pos = jnp.dot(chosen.astype(BF16), before, preferred_element_type=F32)
    cnt_ref[...] = jnp.broadcast_to(jnp.sum(chosen, axis=1, keepdims=True), cnt_ref.shape)

    ex = [jnp.exp(v - vals[0]) for v in vals]
    den = ex[0]
    for e_ in ex[1:]:
        den = den + e_
    for r in range(K):
        gt_ref[r:r + 1, :] = ex[r] / den
        et_ref[r:r + 1, :] = jnp.sum(jnp.where(hits[r], sub, 0.0), axis=0, keepdims=True).astype(I32)
        pt_ref[r:r + 1, :] = jnp.sum(jnp.where(hits[r], pos, 0.0), axis=0, keepdims=True).astype(I32)


def _mix_out_router(y, w_bf16, x, ln_g, ln_b, router_w, router_b):
    T, D = x.shape
    Kin = y.shape[1]
    E, K = MOE_EXPERTS, MOE_TOP_K
    tm = MOE_TILE
    kern = functools.partial(_mix_out_router_kernel, tm=tm, E=E, K=K)
    n_tiles = T // tm
    row = lambda i: (i, 0)
    fixed = lambda i: (0, 0)
    tok = lambda i: (0, i)
    return pl.pallas_call(
        kern,
        grid=(n_tiles,),
        in_specs=[pl.BlockSpec((tm, Kin), row), pl.BlockSpec((Kin, D), fixed),
                  pl.BlockSpec((tm, D), row), pl.BlockSpec((1, D), fixed), pl.BlockSpec((1, D), fixed),
                  pl.BlockSpec((E, D), fixed), pl.BlockSpec((E, 1), fixed)],
        out_specs=[pl.BlockSpec((tm, D), row), pl.BlockSpec((K, tm), tok), pl.BlockSpec((K, tm), tok),
                   pl.BlockSpec((K, tm), tok), pl.BlockSpec((E, LANES), tok)],
        out_shape=[jax.ShapeDtypeStruct((T, D), F32), jax.ShapeDtypeStruct((K, T), I32),
                   jax.ShapeDtypeStruct((K, T), F32), jax.ShapeDtypeStruct((K, T), I32),
                   jax.ShapeDtypeStruct((E, n_tiles * LANES), F32)],
        compiler_params=_cparams(("parallel",)),
        name="mix_out_router",
    )(y, w_bf16, x, ln_g.reshape(1, D), ln_b.reshape(1, D), router_w.T, router_b.reshape(E, 1))


def _segment(tables, tile, e, E):
    c8_ref, lo_ref, gd_ref = tables
    idx = tile * E + e
    n = pl.multiple_of(c8_ref[idx], SUBLANES)
    local = pl.multiple_of(lo_ref[idx], SUBLANES)
    glob = pl.multiple_of(gd_ref[idx], SUBLANES)
    return n, local, glob


def _dispatch_kernel(c8_ref, lo_ref, gd_ref, zs_ref, zv_ref, nv_ref, lpos_ref, x_ref, xr_ref,
                     perm_s, sorted_s, zbuf, sem, zsem, *, tm, K, E, RB, n_blocks, LR, CH):
    i = pl.program_id(0)
    tables = (c8_ref, lo_ref, gd_ref)

    def zero_copy(start):
        return pltpu.make_async_copy(zbuf, xr_ref.at[pl.ds(pl.multiple_of(start, RB), RB), :], zsem)

    @pl.when(i == 0)
    def _():
        zbuf[...] = jnp.zeros_like(zbuf)
        for e in range(E):
            @pl.when(zv_ref[e] > 0)
            def _():
                zero_copy(zs_ref[e]).start()

        def tail_start(p, carry):
            zero_copy(p * RB).start()
            return carry

        def tail_wait(p, carry):
            zero_copy(p * RB).wait()
            return carry

        lax.fori_loop(nv_ref[0], n_blocks, tail_start, 0)
        for e in range(E):
            @pl.when(zv_ref[e] > 0)
            def _():
                zero_copy(zs_ref[e]).wait()
        lax.fori_loop(nv_ref[0], n_blocks, tail_wait, 0)

    slot = i % 2

    def seg_copy(tile, e, s):
        n, local, glob = _segment(tables, tile, e, E)
        return pltpu.make_async_copy(sorted_s.at[s, pl.ds(local, n), :], xr_ref.at[pl.ds(glob, n), :], sem.at[s])

    def for_segments(tile, s, fn):
        for e in range(E):
            @pl.when(c8_ref[tile * E + e] > 0)
            def _():
                fn(seg_copy(tile, e, s))

    lpos = lpos_ref[...]
    for c in range(LR // CH):
        rows = lax.broadcasted_iota(I32, (CH, tm), 0) + c * CH
        hit = jnp.zeros((CH, tm), F32)
        for r in range(K):
            hit = hit + jnp.where(rows == lpos[r:r + 1, :], 1.0, 0.0)
        perm_s[c * CH:(c + 1) * CH, :] = hit.astype(BF16)

    @pl.when(i >= 2)
    def _():
        for_segments(i - 2, slot, lambda cp: cp.wait())

    xb = x_ref[...].astype(BF16)
    for c in range(LR // CH):
        sorted_s[slot, c * CH:(c + 1) * CH, :] = jnp.dot(perm_s[c * CH:(c + 1) * CH, :], xb,
                                                         preferred_element_type=F32)
    for_segments(i, slot, lambda cp: cp.start())

    @pl.when(i == pl.num_programs(0) - 1)
    def _():
        @pl.when(i >= 1)
        def _():
            for_segments(i - 1, 1 - slot, lambda cp: cp.wait())
        for_segments(i, slot, lambda cp: cp.wait())


def _dispatch(x1, lpos_t, seg_tables, zero_start, zero_valid, n_valid, n_rows):
    T, D = x1.shape
    K, E, RB = MOE_TOP_K, MOE_EXPERTS, ROW_BLOCK
    tm = MOE_TILE
    LR = _local_rows(tm)
    kern = functools.partial(_dispatch_kernel, tm=tm, K=K, E=E, RB=RB, n_blocks=n_rows // RB, LR=LR, CH=SORT_CHUNK)
    grid_spec = pltpu.PrefetchScalarGridSpec(
        num_scalar_prefetch=6,
        grid=(T // tm,),
        in_specs=[pl.BlockSpec((K, tm), lambda i, *_: (0, i)),
                  pl.BlockSpec((tm, D), lambda i, *_: (i, 0))],
        out_specs=pl.BlockSpec(memory_space=pl.ANY),
        scratch_shapes=[pltpu.VMEM((LR, tm), BF16), pltpu.VMEM((2, LR, D), F32), pltpu.VMEM((RB, D), F32),
                        pltpu.SemaphoreType.DMA((2,)), pltpu.SemaphoreType.DMA],
    )
    return pl.pallas_call(
        kern,
        grid_spec=grid_spec,
        out_shape=jax.ShapeDtypeStruct((n_rows, D), F32),
        compiler_params=pltpu.CompilerParams(dimension_semantics=("arbitrary",),
                                             vmem_limit_bytes=VMEM_LIMIT_BYTES,
                                             has_side_effects=True),
        name="moe_dispatch",
    )(*seg_tables, zero_start, zero_valid, n_valid, lpos_t, x1)


def _expert_kernel(be_ref, nv_ref, nxt_ref, par_ref, x_ref, wgu_hbm, bg_ref, bl_ref, wd_hbm, bd_ref, y_ref,
                   wgu_buf, wd_buf, wg_s, wl_s, wd_s, sem, *, layer):
    p = pl.program_id(0)
    valid = p < nv_ref[0]
    fresh = (p == 0) | (be_ref[p] != be_ref[jnp.maximum(p - 1, 0)])
    half = MXU_DIM // 2

    def weight_copies(e, s):
        return (pltpu.make_async_copy(wgu_hbm.at[layer, e], wgu_buf.at[s], sem.at[0, s]),
                pltpu.make_async_copy(wd_hbm.at[layer, e], wd_buf.at[s], sem.at[1, s]))

    @pl.when(valid & fresh)
    def _():
        s = par_ref[p]

        @pl.when(p == 0)
        def _():
            for cp in weight_copies(be_ref[0], s):
                cp.start()

        @pl.when(nxt_ref[p] >= 0)
        def _():
            for cp in weight_copies(nxt_ref[p], 1 - s):
                cp.start()

        for cp in weight_copies(be_ref[p], s):
            cp.wait()

        r_ = lax.broadcasted_iota(I32, (MXU_DIM, MXU_DIM), 0)
        c_ = lax.broadcasted_iota(I32, (MXU_DIM, MXU_DIM), 1)
        src = jnp.where(c_ < half, 2 * c_, 2 * (c_ - half) + 1)
        perm = jnp.where(r_ == src, 1.0, 0.0).astype(BF16)
        for cb in range(wgu_buf.shape[2] // MXU_DIM):
            blk = wgu_buf[s, :, cb * MXU_DIM:(cb + 1) * MXU_DIM].astype(BF16)
            sep = jnp.dot(blk, perm, preferred_element_type=F32).astype(BF16)
            wg_s[:, cb * half:(cb + 1) * half] = sep[:, :half]
            wl_s[:, cb * half:(cb + 1) * half] = sep[:, half:]
        wd_s[...] = wd_buf[s].astype(BF16)

    @pl.when(valid)
    def _():
        x = x_ref[...].astype(BF16)
        hg = jnp.dot(x, wg_s[...], preferred_element_type=F32) + bg_ref[...]
        hl = jnp.dot(x, wl_s[...], preferred_element_type=F32) + bl_ref[...]
        xg = jnp.minimum(hg, SWIGLU_LIMIT)
        xl = jnp.clip(hl, -SWIGLU_LIMIT, SWIGLU_LIMIT)
        act = xg * jax.nn.sigmoid(SWIGLU_ALPHA * xg) * (xl + 1.0)
        y_ref[...] = jnp.dot(act.astype(BF16), wd_s[...], preferred_element_type=F32) + bd_ref[...]

    @pl.when(jnp.logical_not(valid))
    def _():
        y_ref[...] = jnp.zeros_like(y_ref)


def _experts(x_rows, blk_e, n_valid, blk_next, blk_parity, layer, w_gate_up, b_glu, b_lin, w_down, b_down):
    n_rows, D = x_rows.shape
    _, E, _, F2 = w_gate_up.shape
    F = F2 // 2
    RB = ROW_BLOCK
    n_blocks = n_rows // RB
    xblk = lambda p, be, nv, *_: (jnp.minimum(p, nv[0] - 1), 0)
    yblk = lambda p, *_: (p, 0)
    bsel = lambda p, be, *_: (layer, be[p], 0, 0)
    grid_spec = pltpu.PrefetchScalarGridSpec(
        num_scalar_prefetch=4,
        grid=(n_blocks,),
        in_specs=[pl.BlockSpec((RB, D), xblk),
                  pl.BlockSpec(memory_space=pl.ANY),
                  pl.BlockSpec((None, None, 1, F), bsel), pl.BlockSpec((None, None, 1, F), bsel),
                  pl.BlockSpec(memory_space=pl.ANY), pl.BlockSpec((None, None, 1, D), bsel)],
        out_specs=pl.BlockSpec((RB, D), yblk),
        scratch_shapes=[pltpu.VMEM((2, D, F2), F32), pltpu.VMEM((2, F, D), F32),
                        pltpu.VMEM((D, F), BF16), pltpu.VMEM((D, F), BF16), pltpu.VMEM((F, D), BF16),
                        pltpu.SemaphoreType.DMA((2, 2))],
    )
    return pl.pallas_call(
        functools.partial(_expert_kernel, layer=layer),
        grid_spec=grid_spec,
        out_shape=jax.ShapeDtypeStruct((n_rows, D), F32),
        compiler_params=_cparams(("arbitrary",), EXPERT_VMEM_LIMIT_BYTES),
        name="moe_experts",
    )(blk_e, n_valid, blk_next, blk_parity, x_rows, w_gate_up, b_glu, b_lin, w_down, b_down)


def _combine_kernel(c8_ref, lo_ref, gd_ref, y_ref, lpos_ref, gate_ref, x_ref, g_ref, b_ref, o_ref,
                    ys, sem, *, tm, K, E, LR, CH):
    i = pl.program_id(0)
    n_tiles = pl.num_programs(0)
    slot = i % 2
    tables = (c8_ref, lo_ref, gd_ref)

    def seg_copy(tile, e, s):
        n, local, glob = _segment(tables, tile, e, E)
        return pltpu.make_async_copy(y_ref.at[pl.ds(glob, n), :], ys.at[s, pl.ds(local, n), :], sem.at[s])

    def for_segments(tile, s, fn):
        for e in range(E):
            @pl.when(c8_ref[tile * E + e] > 0)
            def _():
                fn(seg_copy(tile, e, s))

    @pl.when(i == 0)
    def _():
        ys[...] = jnp.zeros_like(ys)
        for_segments(0, 0, lambda cp: cp.start())

    @pl.when(i + 1 < n_tiles)
    def _():
        for_segments(i + 1, 1 - slot, lambda cp: cp.start())

    for_segments(i, slot, lambda cp: cp.wait())

    lpos = lpos_ref[...]
    gate = gate_ref[...]
    f = jnp.zeros((tm, x_ref.shape[1]), F32)
    for c in range(LR // CH):
        cols = lax.broadcasted_iota(I32, (tm, CH), 1) + c * CH
        w = jnp.zeros((tm, CH), F32)
        for r in range(K):
            w = w + jnp.where(cols == lpos[:, r:r + 1], gate[:, r:r + 1], 0.0)
        f = f + jnp.dot(w.astype(BF16), ys[slot, c * CH:(c + 1) * CH, :].astype(BF16),
                        preferred_element_type=F32)
    o_ref[...] = _layer_norm(DEEPNORM_ALPHA * x_ref[...] + f, g_ref[...], b_ref[...])


def _combine(y_rows, lpos_c, seg_tables, x1, gates_c, ln_g, ln_b):
    T, D = x1.shape
    K, E = MOE_TOP_K, MOE_EXPERTS
    tm = MOE_TILE
    LR = _local_rows(tm)
    kern = functools.partial(_combine_kernel, tm=tm, K=K, E=E, LR=LR, CH=SORT_CHUNK)
    grid_spec = pltpu.PrefetchScalarGridSpec(
        num_scalar_prefetch=3,
        grid=(T // tm,),
        in_specs=[pl.BlockSpec(memory_space=pl.ANY),
                  pl.BlockSpec((tm, K), lambda i, *_: (i, 0)),
                  pl.BlockSpec((tm, K), lambda i, *_: (i, 0)),
                  pl.BlockSpec((tm, D), lambda i, *_: (i, 0)),
                  pl.BlockSpec((1, D), lambda i, *_: (0, 0)), pl.BlockSpec((1, D), lambda i, *_: (0, 0))],
        out_specs=pl.BlockSpec((tm, D), lambda i, *_: (i, 0)),
        scratch_shapes=[pltpu.VMEM((2, LR, D), F32), pltpu.SemaphoreType.DMA((2,))],
    )
    return pl.pallas_call(
        kern,
        grid_spec=grid_spec,
        out_shape=jax.ShapeDtypeStruct((T, D), F32),
        compiler_params=_cparams(("arbitrary",)),
        name="moe_combine",
    )(*seg_tables, y_rows, lpos_c, gates_c, x1, ln_g.reshape(1, D), ln_b.reshape(1, D))


def _moe(x1, e_t, g_t, p_t, cnt, layer, w_gate_up, b_glu, b_lin, w_down, b_down, ln_g, ln_b):
    T, D = x1.shape
    E, K, RB = MOE_EXPERTS, MOE_TOP_K, ROW_BLOCK
    tm = MOE_TILE
    n_tiles = T // tm
    n_blocks = -(-(T * K + n_tiles * E * (SUBLANES - 1) + E * (RB - 1)) // RB)
    n_rows = n_blocks * RB

    seg_len = (cnt[:, ::LANES].T.astype(I32) + SUBLANES - 1) // SUBLANES * SUBLANES
    local_off = jnp.cumsum(seg_len, axis=1) - seg_len
    group_rows = jnp.sum(seg_len, axis=0)
    padded = (group_rows + RB - 1) // RB * RB
    pad_end = jnp.cumsum(padded)
    pad_start = pad_end - padded
    global_off = pad_start[None, :] + jnp.cumsum(seg_len, axis=0) - seg_len
    seg_tables = (seg_len.reshape(-1), local_off.reshape(-1).astype(I32), global_off.reshape(-1).astype(I32))

    experts = jnp.arange(E, dtype=I32)
    e_tiles = e_t.reshape(K, n_tiles, tm)
    off_sel = jnp.sum(jnp.where(e_tiles[..., None] == experts, local_off[None, :, None, :], 0), axis=-1)
    lpos_t = (off_sel.reshape(K, T) + p_t).astype(I32)

    n_valid = (pad_end[-1] // RB).astype(I32)
    blk_ids = jnp.minimum(jnp.arange(n_blocks, dtype=I32), n_valid - 1)
    blk_e = jnp.minimum(jnp.sum((pad_end[None, :] <= blk_ids[:, None] * RB).astype(I32), axis=1), E - 1)
    zero_start = jnp.maximum(pad_end - RB, 0).astype(I32)
    nonempty = group_rows > 0
    zero_valid = nonempty.astype(I32)
    n_valid = n_valid.reshape(1)
    later = (experts[None, :] > experts[:, None]) & nonempty[None, :]
    next_e = jnp.min(jnp.where(later, experts[None, :], E), axis=1)
    next_e = jnp.where(next_e < E, next_e, -1).astype(I32)
    parity_e = ((jnp.cumsum(nonempty.astype(I32)) - nonempty.astype(I32)) % 2).astype(I32)
    at_block = blk_e[:, None] == experts[None, :]
    blk_next = jnp.sum(jnp.where(at_block, next_e[None, :], 0), axis=1).astype(I32)
    blk_parity = jnp.sum(jnp.where(at_block, parity_e[None, :], 0), axis=1).astype(I32)

    x_rows = _dispatch(x1, lpos_t, seg_tables, zero_start, zero_valid, n_valid, n_rows)
    y_rows = _experts(x_rows, blk_e, n_valid, blk_next, blk_parity, layer, w_gate_up, b_glu, b_lin, w_down, b_down)
    return _combine(y_rows, lpos_t.T, seg_tables, x1, g_t.T, ln_g, ln_b)


def kernel(x, positions, attn_w_qkv, attn_w_o, mlstm_w_in, mlstm_b_gates, mlstm_norm_g, mlstm_w_out,
           ln_mix_g, ln_mix_b, ln_ffn_g, ln_ffn_b, router_w, router_b,
           w_gate_up, b_gate_up, w_down, b_down):
    B, S, D = x.shape
    T = B * S
    xt = x.reshape(T, D)
    NH = MLSTM_HEADS
    b_glu = b_gate_up[:, :, None, 0::2]
    b_lin = b_gate_up[:, :, None, 1::2]
    b_dn = b_down[:, :, None, :]

    for layer in range(DEPTH):
        slot = layer // 2
        if layer % 2 == 0:
            hd = attn_w_o.shape[1]
            qkv = _qkv_rope(xt, positions.reshape(T, 1), attn_w_qkv[slot].astype(BF16), hd)
            mixed = _moba_attention(qkv, B, S, hd)
            w_out = attn_w_o[slot].astype(BF16)
        else:
            vw = mlstm_w_out.shape[1]
            qw = vw // 2
            w_in = mlstm_w_in[slot]
            dqk = qw // NH
            col_scale = jnp.concatenate([jnp.ones((qw,), F32), jnp.full((qw,), dqk ** -0.5, F32),
                                         jnp.ones((2 * vw,), F32)])
            w_main = (w_in[:, :2 * qw + 2 * vw] * col_scale).astype(BF16)
            w_gate = jnp.pad(w_in[:, 2 * qw + 2 * vw:], ((0, 0), (0, LANES - 2 * NH))).astype(BF16)
            proj, gates = _mlstm_proj(xt, w_main, w_gate)
            bg = mlstm_b_gates[slot]
            b_col = jnp.pad(bg, (0, LANES - 2 * NH)).reshape(1, LANES)
            b_row = bg.reshape(2 * NH, 1)
            mixed = _mlstm_mix(proj, gates, gates[:, :2 * NH].T, b_col, b_row,
                               mlstm_norm_g[slot].reshape(1, vw), B, S)
            w_out = mlstm_w_out[slot].astype(BF16)
        x1, e_t, g_t, p_t, cnt = _mix_out_router(mixed, w_out, xt, ln_mix_g[layer], ln_mix_b[layer],
                                                 router_w[layer], router_b[layer])
        xt = _moe(x1, e_t, g_t, p_t, cnt, layer, w_gate_up, b_glu, b_lin, w_down, b_dn,
                  ln_ffn_g[layer], ln_ffn_b[layer])
    return xt.reshape(B, S, D)
```

```python
import functools

import jax
import jax.numpy as jnp
from jax import lax
from jax.experimental import pallas as pl
from jax.experimental.pallas import tpu as pltpu

F32 = jnp.float32
BF16 = jnp.bfloat16
I32 = jnp.int32

DEPTH = 2
ATTN_HEAD_DIM = 64
ROT_DIM = ATTN_HEAD_DIM // 4
ROPE_THETA = 500000.0
MOBA_BLOCK = 256
MOBA_TOP_K = 3
MLSTM_HEADS = 4
MOE_EXPERTS = 32
MOE_TOP_K = 4
SWIGLU_LIMIT = 7.0
SWIGLU_ALPHA = 1.702
DEEPNORM_ALPHA = (2 * DEPTH) ** 0.25
LN_EPS = 1e-5

LANES = 128
SUBLANES = 8
BF16_SUBLANES = 16
MXU_DIM = 256
VMEM_LIMIT_BYTES = 48 * 1024 * 1024
EXPERT_VMEM_LIMIT_BYTES = 58 * 1024 * 1024

PROJ_TM = 512
PROJ_TN = 512
MOE_TILE = 512
MLSTM_L = 256
ROW_BLOCK = 512
SORT_CHUNK = 256

NT_DIMS = (((1,), (1,)), ((), ()))
TN_DIMS = (((0,), (0,)), ((), ()))
NEG_INF = float("-inf")
MASKED = -1e30
LOG2_E = 1.4426950408889634


def _cparams(sem, vmem=VMEM_LIMIT_BYTES):
    return pltpu.CompilerParams(dimension_semantics=sem, vmem_limit_bytes=vmem)


def _local_rows(tm):
    rows = tm * MOE_TOP_K + MOE_EXPERTS * SUBLANES
    return -(-rows // SORT_CHUNK) * SORT_CHUNK


def _qkv_rope_kernel(pos_ref, invf_ref, x_ref, w_ref, o_ref, *, n_rope_chunks, n_q_chunks, tn):
    ang = pos_ref[...].astype(F32) * invf_ref[...]
    d = lax.broadcasted_iota(I32, ang.shape, 1) & (ATTN_HEAD_DIM - 1)
    cos = jnp.cos(ang)
    sin = jnp.sin(ang)
    half = ROT_DIM // 2
    c_tab = jnp.where(d < ROT_DIM, cos, 1.0)
    s_up = jnp.where(d < half, -sin, 0.0)
    s_dn = jnp.where((d >= half) & (d < ROT_DIM), sin, 0.0)

    xb = x_ref[...].astype(BF16)
    for c in range(w_ref.shape[1] // tn):
        acc = jnp.dot(xb, w_ref[:, c * tn:(c + 1) * tn], preferred_element_type=F32)
        if c >= n_rope_chunks:
            o_ref[:, c * tn:(c + 1) * tn] = acc.astype(o_ref.dtype)
            continue
        scale = ATTN_HEAD_DIM ** -0.5 * LOG2_E if c < n_q_chunks else 1.0
        for s in range(tn // LANES):
            blk = acc[:, s * LANES:(s + 1) * LANES]
            r = (blk * c_tab + pltpu.roll(blk, LANES - half, 1) * s_up + pltpu.roll(blk, half, 1) * s_dn)
            lo = c * tn + s * LANES
            o_ref[:, lo:lo + LANES] = (r * scale).astype(o_ref.dtype)


def _qkv_rope(x, pos, w_bf16, hd):
    T, D = x.shape
    N = w_bf16.shape[1]
    tm, tn = PROJ_TM, PROJ_TN
    inv_freq = ROPE_THETA ** (-jnp.arange(0, ROT_DIM, 2, dtype=F32) / ROT_DIM)
    lane_d = jnp.arange(LANES) % ATTN_HEAD_DIM
    invf = jnp.where(lane_d < ROT_DIM, inv_freq[lane_d % (ROT_DIM // 2)], 0.0).reshape(1, LANES).astype(F32)
    kern = functools.partial(_qkv_rope_kernel, n_rope_chunks=2 * hd // tn, n_q_chunks=hd // tn, tn=tn)
    return pl.pallas_call(
        kern,
        grid=(T // tm,),
        in_specs=[pl.BlockSpec((tm, 1), lambda i: (i, 0)),
                  pl.BlockSpec((1, LANES), lambda i: (0, 0)),
                  pl.BlockSpec((tm, D), lambda i: (i, 0)),
                  pl.BlockSpec((D, N), lambda i: (0, 0))],
        out_specs=pl.BlockSpec((tm, N), lambda i: (i, 0)),
        out_shape=jax.ShapeDtypeStruct((T, N), BF16),
        compiler_params=_cparams(("parallel",)),
        name="qkv_rope",
    )(pos, invf, x, w_bf16)


def _moba_kernel(q_ref, k_ref, v_ref, o_ref, kmh_ref, kml_ref, vt_ref, q2t_ref, bias_ref,
                 m_ref, acc_ref, *score_bufs, nb, blk):
    s_refs, smax_refs = score_bufs[:4], score_bufs[4:]
    i = pl.program_id(2)
    hd = ATTN_HEAD_DIM
    nbp = kmh_ref.shape[0]
    va = vt_ref.shape[2]

    @pl.when(i == 0)
    def _():
        rows = [jnp.mean(k_ref[j * blk:(j + 1) * blk, :].astype(F32), axis=0, keepdims=True)
                for j in range(nb)]
        if nbp > nb:
            rows.append(jnp.zeros((nbp - nb, LANES), F32))
        km = jnp.concatenate(rows, axis=0)
        hi = km.astype(BF16)
        kmh_ref[...] = hi
        kml_ref[...] = (km - hi.astype(F32)).astype(BF16)
        sub_v = lax.broadcasted_iota(I32, (va - hd, blk), 0)
        ones_row = jnp.where(sub_v == 0, 1.0, 0.0).astype(BF16)
        for j in range(nb):
            vt = v_ref[j * blk:(j + 1) * blk, :].astype(F32).T.astype(BF16)
            vt_ref[j, 0] = jnp.concatenate([vt[:hd], ones_row], axis=0)
            vt_ref[j, 1] = jnp.concatenate([vt[hd:], ones_row], axis=0)

    qt = q_ref[...].astype(F32).T
    sub_q = lax.broadcasted_iota(I32, qt.shape, 0)
    zero = jnp.zeros_like(qt)
    q2t = jnp.concatenate([jnp.where(sub_q < hd, qt, zero), jnp.where(sub_q >= hd, qt, zero)],
                          axis=1).astype(BF16)
    q2t_ref[...] = q2t

    gate = (jnp.dot(kmh_ref[...], q2t, preferred_element_type=F32)
            + jnp.dot(kml_ref[...], q2t, preferred_element_type=F32))
    sub = lax.broadcasted_iota(I32, gate.shape, 0)
    sub_f = sub.astype(F32)
    g = jnp.where(sub < i, gate, NEG_INF)
    sel = jnp.zeros(gate.shape, F32)
    for _ in range(MOBA_TOP_K):
        mx = jnp.max(g, axis=0, keepdims=True)
        idx = jnp.min(jnp.where(g == mx, sub_f, float(nbp)), axis=0, keepdims=True)
        hit = (sub_f == idx) & (mx > NEG_INF)
        sel = jnp.where(hit, 1.0, sel)
        g = jnp.where(hit, NEG_INF, g)
    bias_ref[...] = jnp.where(sel > 0.0, 0.0, MASKED)

    def block_bias(pos, j):
        return jnp.where(pos == 0, 0.0, bias_ref[pl.ds(j, 1), :])

    def produce(pos, buf):
        s_ref, smax_ref = buf
        if isinstance(pos, int) and pos == 0:
            ki = k_ref[pl.ds(pl.multiple_of(i * blk, blk), blk), :]
            s = jnp.dot(ki, q2t, preferred_element_type=F32)
            key = lax.broadcasted_iota(I32, s.shape, 0)
            qry = lax.broadcasted_iota(I32, s.shape, 1) & (blk - 1)
            s = jnp.where(key <= qry, s, MASKED)
            smax = jnp.max(s, axis=0, keepdims=True)
        else:
            j = jnp.minimum(pos - 1, i)
            kj = k_ref[pl.ds(pl.multiple_of(j * blk, blk), blk), :]
            s = jnp.dot(kj, q2t_ref[...], preferred_element_type=F32)
            smax = jnp.max(s, axis=0, keepdims=True) + block_bias(pos, j)
        s_ref[...] = s
        smax_ref[...] = smax

    def consume(pos, buf):
        s_ref, smax_ref = buf
        j = jnp.where(pos == 0, i, jnp.minimum(pos - 1, i))
        m_old = m_ref[...]
        m_new = jnp.maximum(m_old, smax_ref[...])
        a = jnp.exp2(m_old - m_new)
        pb = jnp.exp2(s_ref[...] - (m_new - block_bias(pos, j))).astype(BF16)
        upd = jnp.concatenate([jnp.dot(vt_ref[j, 0], pb[:, :blk], preferred_element_type=F32),
                               jnp.dot(vt_ref[j, 1], pb[:, blk:], preferred_element_type=F32)], axis=1)
        acc_ref[...] = a * acc_ref[...] + upd
        m_ref[...] = m_new

    m_ref[...] = jnp.full(m_ref.shape, MASKED, F32)
    acc_ref[...] = jnp.zeros_like(acc_ref)
    buf_a, buf_b, buf_c, buf_d = [(s_refs[n], smax_refs[n]) for n in range(4)]
    produce(0, buf_a)
    produce(1, buf_b)

    def body(t, carry):
        base = 4 * t
        produce(base + 2, buf_c)
        produce(base + 3, buf_d)
        consume(base, buf_a)
        consume(base + 1, buf_b)
        produce(base + 4, buf_a)
        produce(base + 5, buf_b)
        consume(base + 2, buf_c)
        consume(base + 3, buf_d)
        return carry

    full = (i + 1) // 4
    left = (i + 1) % 4
    lax.fori_loop(0, full, body, 0)
    base = 4 * full

    @pl.when(left == 1)
    def _():
        consume(base, buf_a)

    @pl.when(left == 2)
    def _():
        consume(base, buf_a)
        consume(base + 1, buf_b)

    @pl.when(left == 3)
    def _():
        produce(base + 2, buf_c)
        consume(base, buf_a)
        consume(base + 1, buf_b)
        consume(base + 2, buf_c)

    acc = acc_ref[...]
    out_t = acc[:hd] / acc[hd:hd + 1]
    o_ref[...] = jnp.concatenate([out_t[:, :blk], out_t[:, blk:]], axis=0).T.astype(o_ref.dtype)


def _moba_attention(qkv, B, S, hd):
    T = B * S
    blk = MOBA_BLOCK
    nb = S // blk
    nbp = -(-nb // BF16_SUBLANES) * BF16_SUBLANES
    va = ATTN_HEAD_DIM + BF16_SUBLANES
    n_pairs = hd // LANES
    kern = functools.partial(_moba_kernel, nb=nb, blk=blk)
    return pl.pallas_call(
        kern,
        grid=(B, n_pairs, nb),
        in_specs=[pl.BlockSpec((blk, LANES), lambda b, h, i: (b * nb + i, h)),
                  pl.BlockSpec((S, LANES), lambda b, h, i: (b, n_pairs + h)),
                  pl.BlockSpec((S, LANES), lambda b, h, i: (b, 2 * n_pairs + h))],
        out_specs=pl.BlockSpec((blk, LANES), lambda b, h, i: (b * nb + i, h)),
        out_shape=jax.ShapeDtypeStruct((T, hd), BF16),
        scratch_shapes=[pltpu.VMEM((nbp, LANES), BF16), pltpu.VMEM((nbp, LANES), BF16),
                        pltpu.VMEM((nb, 2, va, blk), BF16),
                        pltpu.VMEM((LANES, 2 * blk), BF16), pltpu.VMEM((nbp, 2 * blk), F32),
                        pltpu.VMEM((1, 2 * blk), F32), pltpu.VMEM((va, 2 * blk), F32)]
                       + [pltpu.VMEM((blk, 2 * blk), F32)] * 4 + [pltpu.VMEM((1, 2 * blk), F32)] * 4,
        compiler_params=_cparams(("parallel", "parallel", "arbitrary")),
        name="moba_attention",
    )(qkv, qkv, qkv)


def _mlstm_proj_kernel(x_ref, w_ref, wg_ref, o_ref, g_ref, *, tn):
    xb = x_ref[...].astype(BF16)
    for c in range(w_ref.shape[1] // tn):
        o_ref[:, c * tn:(c + 1) * tn] = jnp.dot(xb, w_ref[:, c * tn:(c + 1) * tn],
                                                preferred_element_type=F32).astype(o_ref.dtype)
    g_ref[...] = jnp.dot(xb, wg_ref[...], preferred_element_type=F32)


def _mlstm_proj(x, w_main, w_gate):
    T, D = x.shape
    N = w_main.shape[1]
    tm = PROJ_TM
    return pl.pallas_call(
        functools.partial(_mlstm_proj_kernel, tn=PROJ_TN),
        grid=(T // tm,),
        in_specs=[pl.BlockSpec((tm, D), lambda i: (i, 0)),
                  pl.BlockSpec((D, N), lambda i: (0, 0)),
                  pl.BlockSpec((D, LANES), lambda i: (0, 0))],
        out_specs=[pl.BlockSpec((tm, N), lambda i: (i, 0)),
                   pl.BlockSpec((tm, LANES), lambda i: (i, 0))],
        out_shape=[jax.ShapeDtypeStruct((T, N), BF16), jax.ShapeDtypeStruct((T, LANES), F32)],
        compiler_params=_cparams(("parallel",)),
        name="mlstm_proj",
    )(x, w_main, w_gate)


def _log_sigmoid(x):
    return jnp.minimum(x, 0.0) - jnp.log(1.0 + jnp.exp(-jnp.abs(x)))


def _mlstm_kernel(q_ref, k_ref, v_ref, o_ref, gc_ref, gr_ref, bc_ref, br_ref, ng_ref, out_ref,
                  C_ref, n_ref, m_ref, *, L, NH, DQK, DV):
    c = pl.program_id(1)

    @pl.when(c == 0)
    def _():
        C_ref[...] = jnp.zeros_like(C_ref)
        n_ref[...] = jnp.zeros_like(n_ref)
        m_ref[...] = jnp.zeros_like(m_ref)

    gcol = gc_ref[...] + bc_ref[...]
    grow = gr_ref[...] + br_ref[...]
    lane = lax.broadcasted_iota(I32, gcol.shape, 1)
    lf_col = jnp.where((lane >= NH) & (lane < 2 * NH), _log_sigmoid(gcol), 0.0)
    lf_row = _log_sigmoid(grow)
    row = lax.broadcasted_iota(I32, (L, L), 0)
    col = lax.broadcasted_iota(I32, (L, L), 1)
    causal = col <= row
    tri = jnp.where(causal, 1.0, 0.0).astype(BF16)

    def pieces(a):
        out = []
        for _ in range(3):
            p = a.astype(BF16)
            out.append(p)
            a = a - p.astype(F32)
        return out

    b_cols = sum(jnp.dot(tri, p, preferred_element_type=F32) for p in pieces(lf_col))
    b_rows = sum(lax.dot_general(p, tri, NT_DIMS, preferred_element_type=F32) for p in pieces(lf_row))

    for h in range(NH):
        i_row = grow[h:h + 1, :]
        i_col = gcol[:, h:h + 1]
        b_row = b_rows[NH + h:NH + h + 1, :]
        b_col = b_cols[:, NH + h:NH + h + 1]
        m_prev = m_ref[h:h + 1, 0:1]
        qh = q_ref[:, h * DQK:(h + 1) * DQK]
        kh = k_ref[:, h * DQK:(h + 1) * DQK]
        vh = v_ref[:, h * DV:(h + 1) * DV]

        D = jnp.where(causal, b_col - b_row + i_row, NEG_INF)
        g = b_col + m_prev
        m_t = jnp.maximum(g, jnp.max(D, axis=1, keepdims=True))
        w_inter = jnp.exp(g - m_t)
        qk = lax.dot_general(qh, kh, NT_DIMS, preferred_element_type=F32)
        A = jnp.exp(D - m_t) * qk
        num = (w_inter * jnp.dot(qh, C_ref[h].astype(BF16), preferred_element_type=F32)
               + jnp.dot(A.astype(BF16), vh, preferred_element_type=F32))
        qn = jnp.sum(qh.astype(F32) * n_ref[h:h + 1, :], axis=1, keepdims=True)
        den = w_inter * qn + jnp.sum(A, axis=1, keepdims=True)
        hh = num / jnp.maximum(jnp.abs(den), jnp.exp(-m_t))

        m_new = m_t[L - 1:L, :]
        b_last = b_col[L - 1:L, :]
        decay = jnp.exp(b_last + m_prev - m_new)
        w_s = jnp.exp(b_last - b_col + i_col - m_new)
        kw = kh.astype(F32) * w_s
        C_ref[h] = decay * C_ref[h] + lax.dot_general(kw.astype(BF16), vh, TN_DIMS,
                                                      preferred_element_type=F32)
        n_ref[h:h + 1, :] = decay * n_ref[h:h + 1, :] + jnp.sum(kw, axis=0, keepdims=True)
        m_ref[h:h + 1, :] = jnp.broadcast_to(m_new, (1, LANES))

        mu = jnp.mean(hh, axis=1, keepdims=True)
        hc = hh - mu
        var = jnp.mean(hc * hc, axis=1, keepdims=True)
        hn = hc * lax.rsqrt(var + LN_EPS) * ng_ref[:, h * DV:(h + 1) * DV]
        og = jax.nn.sigmoid(o_ref[:, h * DV:(h + 1) * DV].astype(F32))
        out_ref[:, h * DV:(h + 1) * DV] = (og * hn).astype(out_ref.dtype)


def _mlstm_mix(proj, gates, gates_t, b_col, b_row, norm_g, B, S):
    T = B * S
    NH = MLSTM_HEADS
    DV = norm_g.shape[1] // NH
    DQK = DV // 2
    L = MLSTM_L
    nc = S // L
    qw = NH * DQK
    vw = NH * DV
    kern = functools.partial(_mlstm_kernel, L=L, NH=NH, DQK=DQK, DV=DV)
    return pl.pallas_call(
        kern,
        grid=(B, nc),
        in_specs=[pl.BlockSpec((L, qw), lambda b, c: (b * nc + c, 0)),
                  pl.BlockSpec((L, qw), lambda b, c: (b * nc + c, 1)),
                  pl.BlockSpec((L, vw), lambda b, c: (b * nc + c, 2 * qw // vw)),
                  pl.BlockSpec((L, vw), lambda b, c: (b * nc + c, 2 * qw // vw + 1)),
                  pl.BlockSpec((L, LANES), lambda b, c: (b * nc + c, 0)),
                  pl.BlockSpec((SUBLANES, L), lambda b, c: (0, b * nc + c)),
                  pl.BlockSpec((1, LANES), lambda b, c: (0, 0)),
                  pl.BlockSpec((SUBLANES, 1), lambda b, c: (0, 0)),
                  pl.BlockSpec((1, vw), lambda b, c: (0, 0))],
        out_specs=pl.BlockSpec((L, vw), lambda b, c: (b * nc + c, 0)),
        out_shape=jax.ShapeDtypeStruct((T, vw), BF16),
        scratch_shapes=[pltpu.VMEM((NH, DQK, DV), F32), pltpu.VMEM((SUBLANES, DQK), F32),
                        pltpu.VMEM((SUBLANES, LANES), F32)],
        compiler_params=_cparams(("parallel", "arbitrary")),
        name="mlstm_mix",
    )(proj, proj, proj, proj, gates, gates_t, b_col, b_row, norm_g)


def _layer_norm(z, g, b):
    mu = jnp.mean(z, axis=-1, keepdims=True)
    zc = z - mu
    var = jnp.mean(zc * zc, axis=-1, keepdims=True)
    return zc * lax.rsqrt(var + LN_EPS) * g + b


def _mix_out_router_kernel(y_ref, w_ref, x_ref, g_ref, b_ref, rwt_ref, rb_ref,
                           x1_ref, et_ref, gt_ref, pt_ref, cnt_ref, *, tm, E, K):
    y = jnp.dot(y_ref[...], w_ref[...], preferred_element_type=F32)
    x1 = _layer_norm(DEEPNORM_ALPHA * x_ref[...] + y, g_ref[...], b_ref[...])
    x1_ref[...] = x1

    def split(a):
        hi = a.astype(BF16)
        return hi, (a - hi.astype(F32)).astype(BF16)

    x_hi, x_lo = split(x1)
    w_hi, w_lo = split(rwt_ref[...])
    nt = lambda a, b: lax.dot_general(a, b, NT_DIMS, preferred_element_type=F32)
    logits = nt(w_hi, x_hi) + nt(w_lo, x_hi) + nt(w_hi, x_lo) + rb_ref[...]
    sub = lax.broadcasted_iota(I32, logits.shape, 0).astype(F32)
    vals, hits = [], []
    cur = logits
    for _ in range(K):
        mx = jnp.max(cur, axis=0, keepdims=True)
        idx = jnp.min(jnp.where(cur == mx, sub, float(E)), axis=0, keepdims=True)
        hit = sub == idx
        vals.append(mx)
        hits.append(hit)
        cur = jnp.where(hit, NEG_INF, cur)

    chosen = jnp.zeros(logits.shape, F32)
    for hit in hits:
        chosen = jnp.where(hit, 1.0, chosen)
    r_ = lax.broadcasted_iota(I32, (tm, tm), 0)
    c_ = lax.broadcasted_iota(I32, (tm, tm), 1)
    before = (r_ < c_).astype(BF16)
    pos = jnp.dot(chosen.astype(BF16), before, preferred_element_type=F32)
    cnt_ref[...] = jnp.broadcast_to(jnp.sum(chosen, axis=1, keepdims=True), cnt_ref.shape)

    ex = [jnp.exp(v - vals[0]) for v in vals]
    den = ex[0]
    for e_ in ex[1:]:
        den = den + e_
    for r in range(K):
        gt_ref[r:r + 1, :] = ex[r] / den
        et_ref[r:r + 1, :] = jnp.sum(jnp.where(hits[r], sub, 0.0), axis=0, keepdims=True).astype(I32)
        pt_ref[r:r + 1, :] = jnp.sum(jnp.where(hits[r], pos, 0.0), axis=0, keepdims=True).astype(I32)


def _mix_out_router(y, w_bf16, x, ln_g, ln_b, router_w, router_b):
    T, D = x.shape
    Kin = y.shape[1]
    E, K = MOE_EXPERTS, MOE_TOP_K
    tm = MOE_TILE
    kern = functools.partial(_mix_out_router_kernel, tm=tm, E=E, K=K)
    n_tiles = T // tm
    row = lambda i: (i, 0)
    fixed = lambda i: (0, 0)
    tok = lambda i: (0, i)
    return pl.pallas_call(
        kern,
        grid=(n_tiles,),
        in_specs=[pl.BlockSpec((tm, Kin), row), pl.BlockSpec((Kin, D), fixed),
                  pl.BlockSpec((tm, D), row), pl.BlockSpec((1, D), fixed), pl.BlockSpec((1, D), fixed),
                  pl.BlockSpec((E, D), fixed), pl.BlockSpec((E, 1), fixed)],
        out_specs=[pl.BlockSpec((tm, D), row), pl.BlockSpec((K, tm), tok), pl.BlockSpec((K, tm), tok),
                   pl.BlockSpec((K, tm), tok), pl.BlockSpec((E, LANES), tok)],
        out_shape=[jax.ShapeDtypeStruct((T, D), F32), jax.ShapeDtypeStruct((K, T), I32),
                   jax.ShapeDtypeStruct((K, T), F32), jax.ShapeDtypeStruct((K, T), I32),
                   jax.ShapeDtypeStruct((E, n_tiles * LANES), F32)],
        compiler_params=_cparams(("parallel",)),
        name="mix_out_router",
    )(y, w_bf16, x, ln_g.reshape(1, D), ln_b.reshape(1, D), router_w.T, router_b.reshape(E, 1))


def _segment(tables, tile, e, E):
    c8_ref, lo_ref, gd_ref = tables
    idx = tile * E + e
    n = pl.multiple_of(c8_ref[idx], SUBLANES)
    local = pl.multiple_of(lo_ref[idx], SUBLANES)
    glob = pl.multiple_of(gd_ref[idx], SUBLANES)
    return n, local, glob


def _dispatch_kernel(c8_ref, lo_ref, gd_ref, zs_ref, zv_ref, nv_ref, lpos_ref, x_ref, xr_ref,
                     sorted_s, zbuf, sem, zsem, *, tm, K, E, RB, n_blocks, LR, CH):
    i = pl.program_id(0)
    tables = (c8_ref, lo_ref, gd_ref)

    def zero_copy(start):
        return pltpu.make_async_copy(zbuf, xr_ref.at[pl.ds(pl.multiple_of(start, RB), RB), :], zsem)

    @pl.when(i == 0)
    def _():
        zbuf[...] = jnp.zeros_like(zbuf)
        for e in range(E):
            @pl.when(zv_ref[e] > 0)
            def _():
                zero_copy(zs_ref[e]).start()

        def tail_start(p, carry):
            zero_copy(p * RB).start()
            return carry

        def tail_wait(p, carry):
            zero_copy(p * RB).wait()
            return carry

        lax.fori_loop(nv_ref[0], n_blocks, tail_start, 0)
        for e in range(E):
            @pl.when(zv_ref[e] > 0)
            def _():
                zero_copy(zs_ref[e]).wait()
        lax.fori_loop(nv_ref[0], n_blocks, tail_wait, 0)

    slot = i % 2

    def seg_copy(tile, e, s):
        n, local, glob = _segment(tables, tile, e, E)
        return pltpu.make_async_copy(sorted_s.at[s, pl.ds(local, n), :], xr_ref.at[pl.ds(glob, n), :], sem.at[s])

    def for_segments(tile, s, fn):
        for e in range(E):
            @pl.when(c8_ref[tile * E + e] > 0)
            def _():
                fn(seg_copy(tile, e, s))

    @pl.when(i >= 2)
    def _():
        for_segments(i - 2, slot, lambda cp: cp.wait())

    lpos = lpos_ref[...]
    xb = x_ref[...].astype(BF16)
    for c in range(LR // CH):
        rows = lax.broadcasted_iota(I32, (CH, tm), 0) + c * CH
        hit = jnp.zeros((CH, tm), F32)
        for r in range(K):
            hit = hit + jnp.where(rows == lpos[r:r + 1, :], 1.0, 0.0)
        sorted_s[slot, c * CH:(c + 1) * CH, :] = jnp.dot(hit.astype(BF16), xb, preferred_element_type=F32)
    for_segments(i, slot, lambda cp: cp.start())

    @pl.when(i == pl.num_programs(0) - 1)
    def _():
        @pl.when(i >= 1)
        def _():
            for_segments(i - 1, 1 - slot, lambda cp: cp.wait())
        for_segments(i, slot, lambda cp: cp.wait())


def _dispatch(x1, lpos_t, seg_tables, zero_start, zero_valid, n_valid, n_rows):
    T, D = x1.shape
    K, E, RB = MOE_TOP_K, MOE_EXPERTS, ROW_BLOCK
    tm = MOE_TILE
    LR = _local_rows(tm)
    kern = functools.partial(_dispatch_kernel, tm=tm, K=K, E=E, RB=RB, n_blocks=n_rows // RB, LR=LR, CH=SORT_CHUNK)
    grid_spec = pltpu.PrefetchScalarGridSpec(
        num_scalar_prefetch=6,
        grid=(T // tm,),
        in_specs=[pl.BlockSpec((K, tm), lambda i, *_: (0, i)),
                  pl.BlockSpec((tm, D), lambda i, *_: (i, 0))],
        out_specs=pl.BlockSpec(memory_space=pl.ANY),
        scratch_shapes=[pltpu.VMEM((2, LR, D), F32), pltpu.VMEM((RB, D), F32),
                        pltpu.SemaphoreType.DMA((2,)), pltpu.SemaphoreType.DMA],
    )
    return pl.pallas_call(
        kern,
        grid_spec=grid_spec,
        out_shape=jax.ShapeDtypeStruct((n_rows, D), F32),
        compiler_params=pltpu.CompilerParams(dimension_semantics=("arbitrary",),
                                             vmem_limit_bytes=VMEM_LIMIT_BYTES,
                                             has_side_effects=True),
        name="moe_dispatch",
    )(*seg_tables, zero_start, zero_valid, n_valid, lpos_t, x1)


def _expert_kernel(be_ref, nv_ref, nxt_ref, par_ref, x_ref, wgu_hbm, bg_ref, bl_ref, wd_hbm, bd_ref, y_ref,
                   wgu_buf, wd_buf, wg_s, wl_s, wd_s, sem, *, layer):
    p = pl.program_id(0)
    valid = p < nv_ref[0]
    fresh = (p == 0) | (be_ref[p] != be_ref[jnp.maximum(p - 1, 0)])
    half = MXU_DIM // 2

    def weight_copies(e, s):
        return (pltpu.make_async_copy(wgu_hbm.at[layer, e], wgu_buf.at[s], sem.at[0, s]),
                pltpu.make_async_copy(wd_hbm.at[layer, e], wd_buf.at[s], sem.at[1, s]))

    @pl.when(valid & fresh)
    def _():
        s = par_ref[p]

        @pl.when(p == 0)
        def _():
            for cp in weight_copies(be_ref[0], s):
                cp.start()

        @pl.when(nxt_ref[p] >= 0)
        def _():
            for cp in weight_copies(nxt_ref[p], 1 - s):
                cp.start()

        for cp in weight_copies(be_ref[p], s):
            cp.wait()

        r_ = lax.broadcasted_iota(I32, (MXU_DIM, MXU_DIM), 0)
        c_ = lax.broadcasted_iota(I32, (MXU_DIM, MXU_DIM), 1)
        src = jnp.where(c_ < half, 2 * c_, 2 * (c_ - half) + 1)
        perm = jnp.where(r_ == src, 1.0, 0.0).astype(BF16)
        for cb in range(wgu_buf.shape[2] // MXU_DIM):
            blk = wgu_buf[s, :, cb * MXU_DIM:(cb + 1) * MXU_DIM].astype(BF16)
            sep = jnp.dot(blk, perm, preferred_element_type=F32).astype(BF16)
            wg_s[:, cb * half:(cb + 1) * half] = sep[:, :half]
            wl_s[:, cb * half:(cb + 1) * half] = sep[:, half:]
        wd_s[...] = wd_buf[s].astype(BF16)

    @pl.when(valid)
    def _():
        x = x_ref[...].astype(BF16)
        hg = jnp.dot(x, wg_s[...], preferred_element_type=F32) + bg_ref[...]
        hl = jnp.dot(x, wl_s[...], preferred_element_type=F32) + bl_ref[...]
        xg = jnp.minimum(hg, SWIGLU_LIMIT)
        xl = jnp.clip(hl, -SWIGLU_LIMIT, SWIGLU_LIMIT)
        act = xg * jax.nn.sigmoid(SWIGLU_ALPHA * xg) * (xl + 1.0)
        y_ref[...] = jnp.dot(act.astype(BF16), wd_s[...], preferred_element_type=F32) + bd_ref[...]

    @pl.when(jnp.logical_not(valid))
    def _():
        y_ref[...] = jnp.zeros_like(y_ref)


def _experts(x_rows, blk_e, n_valid, blk_next, blk_parity, layer, w_gate_up, b_glu, b_lin, w_down, b_down):
    n_rows, D = x_rows.shape
    _, E, _, F2 = w_gate_up.shape
    F = F2 // 2
    RB = ROW_BLOCK
    n_blocks = n_rows // RB
    xblk = lambda p, be, nv, *_: (jnp.minimum(p, nv[0] - 1), 0)
    yblk = lambda p, *_: (p, 0)
    bsel = lambda p, be, *_: (layer, be[p], 0, 0)
    grid_spec = pltpu.PrefetchScalarGridSpec(
        num_scalar_prefetch=4,
        grid=(n_blocks,),
        in_specs=[pl.BlockSpec((RB, D), xblk),
                  pl.BlockSpec(memory_space=pl.ANY),
                  pl.BlockSpec((None, None, 1, F), bsel), pl.BlockSpec((None, None, 1, F), bsel),
                  pl.BlockSpec(memory_space=pl.ANY), pl.BlockSpec((None, None, 1, D), bsel)],
        out_specs=pl.BlockSpec((RB, D), yblk),
        scratch_shapes=[pltpu.VMEM((2, D, F2), F32), pltpu.VMEM((2, F, D), F32),
                        pltpu.VMEM((D, F), BF16), pltpu.VMEM((D, F), BF16), pltpu.VMEM((F, D), BF16),
                        pltpu.SemaphoreType.DMA((2, 2))],
    )
    return pl.pallas_call(
        functools.partial(_expert_kernel, layer=layer),
        grid_spec=grid_spec,
        out_shape=jax.ShapeDtypeStruct((n_rows, D), F32),
        compiler_params=_cparams(("arbitrary",), EXPERT_VMEM_LIMIT_BYTES),
        name="moe_experts",
    )(blk_e, n_valid, blk_next, blk_parity, x_rows, w_gate_up, b_glu, b_lin, w_down, b_down)


def _combine_kernel(c8_ref, lo_ref, gd_ref, y_ref, lpos_ref, gate_ref, x_ref, g_ref, b_ref, o_ref,
                    ys, sem, *, tm, K, E, LR, CH):
    i = pl.program_id(0)
    n_tiles = pl.num_programs(0)
    slot = i % 2
    tables = (c8_ref, lo_ref, gd_ref)

    def seg_copy(tile, e, s):
        n, local, glob = _segment(tables, tile, e, E)
        return pltpu.make_async_copy(y_ref.at[pl.ds(glob, n), :], ys.at[s, pl.ds(local, n), :], sem.at[s])

    def for_segments(tile, s, fn):
        for e in range(E):
            @pl.when(c8_ref[tile * E + e] > 0)
            def _():
                fn(seg_copy(tile, e, s))

    @pl.when(i == 0)
    def _():
        ys[...] = jnp.zeros_like(ys)
        for_segments(0, 0, lambda cp: cp.start())

    @pl.when(i + 1 < n_tiles)
    def _():
        for_segments(i + 1, 1 - slot, lambda cp: cp.start())

    for_segments(i, slot, lambda cp: cp.wait())

    lpos = lpos_ref[...]
    gate = gate_ref[...]
    f = jnp.zeros((tm, x_ref.shape[1]), F32)
    for c in range(LR // CH):
        cols = lax.broadcasted_iota(I32, (tm, CH), 1) + c * CH
        w = jnp.zeros((tm, CH), F32)
        for r in range(K):
            w = w + jnp.where(cols == lpos[:, r:r + 1], gate[:, r:r + 1], 0.0)
        f = f + jnp.dot(w.astype(BF16), ys[slot, c * CH:(c + 1) * CH, :].astype(BF16),
                        preferred_element_type=F32)
    o_ref[...] = _layer_norm(DEEPNORM_ALPHA * x_ref[...] + f, g_ref[...], b_ref[...])


def _combine(y_rows, lpos_c, seg_tables, x1, gates_c, ln_g, ln_b):
    T, D = x1.shape
    K, E = MOE_TOP_K, MOE_EXPERTS
    tm = MOE_TILE
    LR = _local_rows(tm)
    kern = functools.partial(_combine_kernel, tm=tm, K=K, E=E, LR=LR, CH=SORT_CHUNK)
    grid_spec = pltpu.PrefetchScalarGridSpec(
        num_scalar_prefetch=3,
        grid=(T // tm,),
        in_specs=[pl.BlockSpec(memory_space=pl.ANY),
                  pl.BlockSpec((tm, K), lambda i, *_: (i, 0)),
                  pl.BlockSpec((tm, K), lambda i, *_: (i, 0)),
                  pl.BlockSpec((tm, D), lambda i, *_: (i, 0)),
                  pl.BlockSpec((1, D), lambda i, *_: (0, 0)), pl.BlockSpec((1, D), lambda i, *_: (0, 0))],
        out_specs=pl.BlockSpec((tm, D), lambda i, *_: (i, 0)),
        scratch_shapes=[pltpu.VMEM((2, LR, D), F32), pltpu.SemaphoreType.DMA((2,))],
    )
    return pl.pallas_call(
        kern,
        grid_spec=grid_spec,
        out_shape=jax.ShapeDtypeStruct((T, D), F32),
        compiler_params=_cparams(("arbitrary",)),
        name="moe_combine",
    )(*seg_tables, y_rows, lpos_c, gates_c, x1, ln_g.reshape(1, D), ln_b.reshape(1, D))


def _moe(x1, e_t, g_t, p_t, cnt, layer, w_gate_up, b_glu, b_lin, w_down, b_down, ln_g, ln_b):
    T, D = x1.shape
    E, K, RB = MOE_EXPERTS, MOE_TOP_K, ROW_BLOCK
    tm = MOE_TILE
    n_tiles = T // tm
    n_blocks = -(-(T * K + n_tiles * E * (SUBLANES - 1) + E * (RB - 1)) // RB)
    n_rows = n_blocks * RB

    seg_len = (cnt[:, ::LANES].T.astype(I32) + SUBLANES - 1) // SUBLANES * SUBLANES
    local_off = jnp.cumsum(seg_len, axis=1) - seg_len
    group_rows = jnp.sum(seg_len, axis=0)
    padded = (group_rows + RB - 1) // RB * RB
    pad_end = jnp.cumsum(padded)
    pad_start = pad_end - padded
    global_off = pad_start[None, :] + jnp.cumsum(seg_len, axis=0) - seg_len
    seg_tables = (seg_len.reshape(-1), local_off.reshape(-1).astype(I32), global_off.reshape(-1).astype(I32))

    experts = jnp.arange(E, dtype=I32)
    e_tiles = e_t.reshape(K, n_tiles, tm)
    off_sel = jnp.sum(jnp.where(e_tiles[..., None] == experts, local_off[None, :, None, :], 0), axis=-1)
    lpos_t = (off_sel.reshape(K, T) + p_t).astype(I32)

    n_valid = (pad_end[-1] // RB).astype(I32)
    blk_ids = jnp.minimum(jnp.arange(n_blocks, dtype=I32), n_valid - 1)
    blk_e = jnp.minimum(jnp.sum((pad_end[None, :] <= blk_ids[:, None] * RB).astype(I32), axis=1), E - 1)
    zero_start = jnp.maximum(pad_end - RB, 0).astype(I32)
    nonempty = group_rows > 0
    zero_valid = nonempty.astype(I32)
    n_valid = n_valid.reshape(1)
    later = (experts[None, :] > experts[:, None]) & nonempty[None, :]
    next_e = jnp.min(jnp.where(later, experts[None, :], E), axis=1)
    next_e = jnp.where(next_e < E, next_e, -1).astype(I32)
    parity_e = ((jnp.cumsum(nonempty.astype(I32)) - nonempty.astype(I32)) % 2).astype(I32)
    at_block = blk_e[:, None] == experts[None, :]
    blk_next = jnp.sum(jnp.where(at_block, next_e[None, :], 0), axis=1).astype(I32)
    blk_parity = jnp.sum(jnp.where(at_block, parity_e[None, :], 0), axis=1).astype(I32)

    x_rows = _dispatch(x1, lpos_t, seg_tables, zero_start, zero_valid, n_valid, n_rows)
    y_rows = _experts(x_rows, blk_e, n_valid, blk_next, blk_parity, layer, w_gate_up, b_glu, b_lin, w_down, b_down)
    return _combine(y_rows, lpos_t.T, seg_tables, x1, g_t.T, ln_g, ln_b)


def kernel(x, positions, attn_w_qkv, attn_w_o, mlstm_w_in, mlstm_b_gates, mlstm_norm_g, mlstm_w_out,
           ln_mix_g, ln_mix_b, ln_ffn_g, ln_ffn_b, router_w, router_b,
           w_gate_up, b_gate_up, w_down, b_down):
    B, S, D = x.shape
    T = B * S
    xt = x.reshape(T, D)
    NH = MLSTM_HEADS
    b_glu = b_gate_up[:, :, None, 0::2]
    b_lin = b_gate_up[:, :, None, 1::2]
    b_dn = b_down[:, :, None, :]

    for layer in range(DEPTH):
        slot = layer // 2
        if layer % 2 == 0:
            hd = attn_w_o.shape[1]
            qkv = _qkv_rope(xt, positions.reshape(T, 1), attn_w_qkv[slot].astype(BF16), hd)
            mixed = _moba_attention(qkv, B, S, hd)
            w_out = attn_w_o[slot].astype(BF16)
        else:
            vw = mlstm_w_out.shape[1]
            qw = vw // 2
            w_in = mlstm_w_in[slot]
            dqk = qw // NH
            col_scale = jnp.concatenate([jnp.ones((qw,), F32), jnp.full((qw,), dqk ** -0.5, F32),
                                         jnp.ones((2 * vw,), F32)])
            w_main = (w_in[:, :2 * qw + 2 * vw] * col_scale).astype(BF16)
            w_gate = jnp.pad(w_in[:, 2 * qw + 2 * vw:], ((0, 0), (0, LANES - 2 * NH))).astype(BF16)
            proj, gates = _mlstm_proj(xt, w_main, w_gate)
            bg = mlstm_b_gates[slot]
            b_col = jnp.pad(bg, (0, LANES - 2 * NH)).reshape(1, LANES)
            b_row = bg.reshape(2 * NH, 1)
            mixed = _mlstm_mix(proj, gates, gates[:, :2 * NH].T, b_col, b_row,
                               mlstm_norm_g[slot].reshape(1, vw), B, S)
            w_out = mlstm_w_out[slot].astype(BF16)
        x1, e_t, g_t, p_t, cnt = _mix_out_router(mixed, w_out, xt, ln_mix_g[layer], ln_mix_b[layer],
                                                 router_w[layer], router_b[layer])
        xt = _moe(x1, e_t, g_t, p_t, cnt, layer, w_gate_up, b_glu, b_lin, w_down, b_dn,
                  ln_ffn_g[layer], ln_ffn_b[layer])
    return xt.reshape(B, S, D)
```

```python
import functools

import jax
import jax.numpy as jnp
from jax import lax
from jax.experimental import pallas as pl
from jax.experimental.pallas import tpu as pltpu

F32 = jnp.float32
BF16 = jnp.bfloat16
I32 = jnp.int32

DEPTH = 2
ATTN_HEAD_DIM = 64
ROT_DIM = ATTN_HEAD_DIM // 4
ROPE_THETA = 500000.0
MOBA_BLOCK = 256
MOBA_TOP_K = 3
MLSTM_HEADS = 4
MOE_EXPERTS = 32
MOE_TOP_K = 4
SWIGLU_LIMIT = 7.0
SWIGLU_ALPHA = 1.702
DEEPNORM_ALPHA = (2 * DEPTH) ** 0.25
LN_EPS = 1e-5

LANES = 128
SUBLANES = 8
BF16_SUBLANES = 16
MXU_DIM = 256
VMEM_LIMIT_BYTES = 48 * 1024 * 1024
EXPERT_VMEM_LIMIT_BYTES = 58 * 1024 * 1024

PROJ_TM = 512
PROJ_TN = 512
MOE_TILE = 512
MLSTM_L = 256
ROW_BLOCK = 512
SORT_CHUNK = 256

NT_DIMS = (((1,), (1,)), ((), ()))
TN_DIMS = (((0,), (0,)), ((), ()))
NEG_INF = float("-inf")
MASKED = -1e30
LOG2_E = 1.4426950408889634


def _cparams(sem, vmem=VMEM_LIMIT_BYTES):
    return pltpu.CompilerParams(dimension_semantics=sem, vmem_limit_bytes=vmem)


def _local_rows(tm):
    rows = tm * MOE_TOP_K + MOE_EXPERTS * SUBLANES
    return -(-rows // SORT_CHUNK) * SORT_CHUNK


def _qkv_rope_kernel(pos_ref, invf_ref, x_ref, w_ref, o_ref, *, n_rope_chunks, n_q_chunks, tn):
    ang = pos_ref[...].astype(F32) * invf_ref[...]
    d = lax.broadcasted_iota(I32, ang.shape, 1) & (ATTN_HEAD_DIM - 1)
    cos = jnp.cos(ang)
    sin = jnp.sin(ang)
    half = ROT_DIM // 2
    c_tab = jnp.where(d < ROT_DIM, cos, 1.0)
    s_up = jnp.where(d < half, -sin, 0.0)
    s_dn = jnp.where((d >= half) & (d < ROT_DIM), sin, 0.0)

    xb = x_ref[...].astype(BF16)
    for c in range(w_ref.shape[1] // tn):
        acc = jnp.dot(xb, w_ref[:, c * tn:(c + 1) * tn], preferred_element_type=F32)
        if c >= n_rope_chunks:
            o_ref[:, c * tn:(c + 1) * tn] = acc.astype(o_ref.dtype)
            continue
        scale = ATTN_HEAD_DIM ** -0.5 * LOG2_E if c < n_q_chunks else 1.0
        for s in range(tn // LANES):
            blk = acc[:, s * LANES:(s + 1) * LANES]
            r = (blk * c_tab + pltpu.roll(blk, LANES - half, 1) * s_up + pltpu.roll(blk, half, 1) * s_dn)
            lo = c * tn + s * LANES
            o_ref[:, lo:lo + LANES] = (r * scale).astype(o_ref.dtype)


def _qkv_rope(x, pos, w_bf16, hd):
    T, D = x.shape
    N = w_bf16.shape[1]
    tm, tn = PROJ_TM, PROJ_TN
    inv_freq = ROPE_THETA ** (-jnp.arange(0, ROT_DIM, 2, dtype=F32) / ROT_DIM)
    lane_d = jnp.arange(LANES) % ATTN_HEAD_DIM
    invf = jnp.where(lane_d < ROT_DIM, inv_freq[lane_d % (ROT_DIM // 2)], 0.0).reshape(1, LANES).astype(F32)
    kern = functools.partial(_qkv_rope_kernel, n_rope_chunks=2 * hd // tn, n_q_chunks=hd // tn, tn=tn)
    return pl.pallas_call(
        kern,
        grid=(T // tm,),
        in_specs=[pl.BlockSpec((tm, 1), lambda i: (i, 0)),
                  pl.BlockSpec((1, LANES), lambda i: (0, 0)),
                  pl.BlockSpec((tm, D), lambda i: (i, 0)),
                  pl.BlockSpec((D, N), lambda i: (0, 0))],
        out_specs=pl.BlockSpec((tm, N), lambda i: (i, 0)),
        out_shape=jax.ShapeDtypeStruct((T, N), BF16),
        compiler_params=_cparams(("parallel",)),
        name="qkv_rope",
    )(pos, invf, x, w_bf16)


def _moba_kernel(q_ref, k_ref, v_ref, o_ref, kmh_ref, kml_ref, vt_ref, q2t_ref, bias_ref,
                 m_ref, acc_ref, *score_bufs, nb, blk):
    s_refs, smax_refs = score_bufs[:4], score_bufs[4:]
    i = pl.program_id(2)
    hd = ATTN_HEAD_DIM
    nbp = kmh_ref.shape[0]
    va = vt_ref.shape[2]

    @pl.when(i == 0)
    def _():
        rows = [jnp.mean(k_ref[j * blk:(j + 1) * blk, :].astype(F32), axis=0, keepdims=True)
                for j in range(nb)]
        if nbp > nb:
            rows.append(jnp.zeros((nbp - nb, LANES), F32))
        km = jnp.concatenate(rows, axis=0)
        hi = km.astype(BF16)
        kmh_ref[...] = hi
        kml_ref[...] = (km - hi.astype(F32)).astype(BF16)
        sub_v = lax.broadcasted_iota(I32, (va - hd, blk), 0)
        ones_row = jnp.where(sub_v == 0, 1.0, 0.0).astype(BF16)
        for j in range(nb):
            vt = v_ref[j * blk:(j + 1) * blk, :].astype(F32).T.astype(BF16)
            vt_ref[j, 0] = jnp.concatenate([vt[:hd], ones_row], axis=0)
            vt_ref[j, 1] = jnp.concatenate([vt[hd:], ones_row], axis=0)

    qt = q_ref[...].astype(F32).T
    sub_q = lax.broadcasted_iota(I32, qt.shape, 0)
    zero = jnp.zeros_like(qt)
    q2t = jnp.concatenate([jnp.where(sub_q < hd, qt, zero), jnp.where(sub_q >= hd, qt, zero)],
                          axis=1).astype(BF16)
    q2t_ref[...] = q2t

    gate = (jnp.dot(kmh_ref[...], q2t, preferred_element_type=F32)
            + jnp.dot(kml_ref[...], q2t, preferred_element_type=F32))
    sub = lax.broadcasted_iota(I32, gate.shape, 0)
    sub_f = sub.astype(F32)
    g = jnp.where(sub < i, gate, NEG_INF)
    sel = jnp.zeros(gate.shape, F32)
    for _ in range(MOBA_TOP_K):
        mx = jnp.max(g, axis=0, keepdims=True)
        idx = jnp.min(jnp.where(g == mx, sub_f, float(nbp)), axis=0, keepdims=True)
        hit = (sub_f == idx) & (mx > NEG_INF)
        sel = jnp.where(hit, 1.0, sel)
        g = jnp.where(hit, NEG_INF, g)
    bias_ref[...] = jnp.where(sel > 0.0, 0.0, MASKED)

    def block_bias(pos, j):
        return jnp.where(pos == 0, 0.0, bias_ref[pl.ds(j, 1), :])

    def produce(pos, buf):
        s_ref, smax_ref = buf
        if isinstance(pos, int) and pos == 0:
            ki = k_ref[pl.ds(pl.multiple_of(i * blk, blk), blk), :]
            s = jnp.dot(ki, q2t, preferred_element_type=F32)
            key = lax.broadcasted_iota(I32, s.shape, 0)
            qry = lax.broadcasted_iota(I32, s.shape, 1) & (blk - 1)
            s = jnp.where(key <= qry, s, MASKED)
            smax = jnp.max(s, axis=0, keepdims=True)
        else:
            j = jnp.minimum(pos - 1, i)
            kj = k_ref[pl.ds(pl.multiple_of(j * blk, blk), blk), :]
            s = jnp.dot(kj, q2t_ref[...], preferred_element_type=F32)
            smax = jnp.max(s, axis=0, keepdims=True) + block_bias(pos, j)
        s_ref[...] = s
        smax_ref[...] = smax

    def consume(pos, buf):
        s_ref, smax_ref = buf
        j = jnp.where(pos == 0, i, jnp.minimum(pos - 1, i))
        m_old = m_ref[...]
        m_new = jnp.maximum(m_old, smax_ref[...])
        a = jnp.exp2(m_old - m_new)
        pb = jnp.exp2(s_ref[...] - (m_new - block_bias(pos, j))).astype(BF16)
        upd = jnp.concatenate([jnp.dot(vt_ref[j, 0], pb[:, :blk], preferred_element_type=F32),
                               jnp.dot(vt_ref[j, 1], pb[:, blk:], preferred_element_type=F32)], axis=1)
        acc_ref[...] = a * acc_ref[...] + upd
        m_ref[...] = m_new

    m_ref[...] = jnp.full(m_ref.shape, MASKED, F32)
    acc_ref[...] = jnp.zeros_like(acc_ref)
    buf_a, buf_b, buf_c, buf_d = [(s_refs[n], smax_refs[n]) for n in range(4)]
    produce(0, buf_a)
    produce(1, buf_b)

    def body(t, carry):
        base = 4 * t
        produce(base + 2, buf_c)
        produce(base + 3, buf_d)
        consume(base, buf_a)
        consume(base + 1, buf_b)
        produce(base + 4, buf_a)
        produce(base + 5, buf_b)
        consume(base + 2, buf_c)
        consume(base + 3, buf_d)
        return carry

    full = (i + 1) // 4
    left = (i + 1) % 4
    lax.fori_loop(0, full, body, 0)
    base = 4 * full

    @pl.when(left == 1)
    def _():
        consume(base, buf_a)

    @pl.when(left == 2)
    def _():
        consume(base, buf_a)
        consume(base + 1, buf_b)

    @pl.when(left == 3)
    def _():
        produce(base + 2, buf_c)
        consume(base, buf_a)
        consume(base + 1, buf_b)
        consume(base + 2, buf_c)

    acc = acc_ref[...]
    out_t = acc[:hd] / acc[hd:hd + 1]
    o_ref[...] = jnp.concatenate([out_t[:, :blk], out_t[:, blk:]], axis=0).T.astype(o_ref.dtype)


def _moba_attention(qkv, B, S, hd):
    T = B * S
    blk = MOBA_BLOCK
    nb = S // blk
    nbp = -(-nb // BF16_SUBLANES) * BF16_SUBLANES
    va = ATTN_HEAD_DIM + BF16_SUBLANES
    n_pairs = hd // LANES
    kern = functools.partial(_moba_kernel, nb=nb, blk=blk)
    return pl.pallas_call(
        kern,
        grid=(B, n_pairs, nb),
        in_specs=[pl.BlockSpec((blk, LANES), lambda b, h, i: (b * nb + i, h)),
                  pl.BlockSpec((S, LANES), lambda b, h, i: (b, n_pairs + h)),
                  pl.BlockSpec((S, LANES), lambda b, h, i: (b, 2 * n_pairs + h))],
        out_specs=pl.BlockSpec((blk, LANES), lambda b, h, i: (b * nb + i, h)),
        out_shape=jax.ShapeDtypeStruct((T, hd), BF16),
        scratch_shapes=[pltpu.VMEM((nbp, LANES), BF16), pltpu.VMEM((nbp, LANES), BF16),
                        pltpu.VMEM((nb, 2, va, blk), BF16),
                        pltpu.VMEM((LANES, 2 * blk), BF16), pltpu.VMEM((nbp, 2 * blk), F32),
                        pltpu.VMEM((1, 2 * blk), F32), pltpu.VMEM((va, 2 * blk), F32)]
                       + [pltpu.VMEM((blk, 2 * blk), F32)] * 4 + [pltpu.VMEM((1, 2 * blk), F32)] * 4,
        compiler_params=_cparams(("parallel", "parallel", "arbitrary")),
        name="moba_attention",
    )(qkv, qkv, qkv)


def _mlstm_proj_kernel(x_ref, w_ref, wg_ref, o_ref, g_ref, *, tn):
    xb = x_ref[...].astype(BF16)
    for c in range(w_ref.shape[1] // tn):
        o_ref[:, c * tn:(c + 1) * tn] = jnp.dot(xb, w_ref[:, c * tn:(c + 1) * tn],
                                                preferred_element_type=F32).astype(o_ref.dtype)
    g_ref[...] = jnp.dot(xb, wg_ref[...], preferred_element_type=F32)


def _mlstm_proj(x, w_main, w_gate):
    T, D = x.shape
    N = w_main.shape[1]
    tm = PROJ_TM
    return pl.pallas_call(
        functools.partial(_mlstm_proj_kernel, tn=PROJ_TN),
        grid=(T // tm,),
        in_specs=[pl.BlockSpec((tm, D), lambda i: (i, 0)),
                  pl.BlockSpec((D, N), lambda i: (0, 0)),
                  pl.BlockSpec((D, LANES), lambda i: (0, 0))],
        out_specs=[pl.BlockSpec((tm, N), lambda i: (i, 0)),
                   pl.BlockSpec((tm, LANES), lambda i: (i, 0))],
        out_shape=[jax.ShapeDtypeStruct((T, N), BF16), jax.ShapeDtypeStruct((T, LANES), F32)],
        compiler_params=_cparams(("parallel",)),
        name="mlstm_proj",
    )(x, w_main, w_gate)


def _log_sigmoid(x):
    return jnp.minimum(x, 0.0) - jnp.log(1.0 + jnp.exp(-jnp.abs(x)))


def _mlstm_kernel(q_ref, k_ref, v_ref, o_ref, gc_ref, gr_ref, bc_ref, br_ref, ng_ref, out_ref,
                  C_ref, n_ref, m_ref, *, L, NH, DQK, DV):
    c = pl.program_id(1)

    @pl.when(c == 0)
    def _():
        C_ref[...] = jnp.zeros_like(C_ref)
        n_ref[...] = jnp.zeros_like(n_ref)
        m_ref[...] = jnp.zeros_like(m_ref)

    gcol = gc_ref[...] + bc_ref[...]
    grow = gr_ref[...] + br_ref[...]
    lane = lax.broadcasted_iota(I32, gcol.shape, 1)
    lf_col = jnp.where((lane >= NH) & (lane < 2 * NH), _log_sigmoid(gcol), 0.0)
    lf_row = _log_sigmoid(grow)
    row = lax.broadcasted_iota(I32, (L, L), 0)
    col = lax.broadcasted_iota(I32, (L, L), 1)
    causal = col <= row
    tri = jnp.where(causal, 1.0, 0.0).astype(BF16)

    def pieces(a):
        out = []
        for _ in range(3):
            p = a.astype(BF16)
            out.append(p)
            a = a - p.astype(F32)
        return out

    b_cols = sum(jnp.dot(tri, p, preferred_element_type=F32) for p in pieces(lf_col))
    b_rows = sum(lax.dot_general(p, tri, NT_DIMS, preferred_element_type=F32) for p in pieces(lf_row))

    for h in range(NH):
        i_row = grow[h:h + 1, :]
        i_col = gcol[:, h:h + 1]
        b_row = b_rows[NH + h:NH + h + 1, :]
        b_col = b_cols[:, NH + h:NH + h + 1]
        m_prev = m_ref[h:h + 1, 0:1]
        qh = q_ref[:, h * DQK:(h + 1) * DQK]
        kh = k_ref[:, h * DQK:(h + 1) * DQK]
        vh = v_ref[:, h * DV:(h + 1) * DV]

        D = jnp.where(causal, b_col - b_row + i_row, NEG_INF)
        g = b_col + m_prev
        m_t = jnp.maximum(g, jnp.max(D, axis=1, keepdims=True))
        w_inter = jnp.exp(g - m_t)
        qk = lax.dot_general(qh, kh, NT_DIMS, preferred_element_type=F32)
        A = jnp.exp(D - m_t) * qk
        num = (w_inter * jnp.dot(qh, C_ref[h].astype(BF16), preferred_element_type=F32)
               + jnp.dot(A.astype(BF16), vh, preferred_element_type=F32))
        qn = jnp.sum(qh.astype(F32) * n_ref[h:h + 1, :], axis=1, keepdims=True)
        den = w_inter * qn + jnp.sum(A, axis=1, keepdims=True)
        hh = num / jnp.maximum(jnp.abs(den), jnp.exp(-m_t))

        m_new = m_t[L - 1:L, :]
        b_last = b_col[L - 1:L, :]
        decay = jnp.exp(b_last + m_prev - m_new)
        w_s = jnp.exp(b_last - b_col + i_col - m_new)
        kw = kh.astype(F32) * w_s
        C_ref[h] = decay * C_ref[h] + lax.dot_general(kw.astype(BF16), vh, TN_DIMS,
                                                      preferred_element_type=F32)
        n_ref[h:h + 1, :] = decay * n_ref[h:h + 1, :] + jnp.sum(kw, axis=0, keepdims=True)
        m_ref[h:h + 1, :] = jnp.broadcast_to(m_new, (1, LANES))

        mu = jnp.mean(hh, axis=1, keepdims=True)
        hc = hh - mu
        var = jnp.mean(hc * hc, axis=1, keepdims=True)
        hn = hc * lax.rsqrt(var + LN_EPS) * ng_ref[:, h * DV:(h + 1) * DV]
        og = jax.nn.sigmoid(o_ref[:, h * DV:(h + 1) * DV].astype(F32))
        out_ref[:, h * DV:(h + 1) * DV] = (og * hn).astype(out_ref.dtype)


def _mlstm_mix(proj, gates, gates_t, b_col, b_row, norm_g, B, S):
    T = B * S
    NH = MLSTM_HEADS
    DV = norm_g.shape[1] // NH
    DQK = DV // 2
    L = MLSTM_L
    nc = S // L
    qw = NH * DQK
    vw = NH * DV
    kern = functools.partial(_mlstm_kernel, L=L, NH=NH, DQK=DQK, DV=DV)
    return pl.pallas_call(
        kern,
        grid=(B, nc),
        in_specs=[pl.BlockSpec((L, qw), lambda b, c: (b * nc + c, 0)),
                  pl.BlockSpec((L, qw), lambda b, c: (b * nc + c, 1)),
                  pl.BlockSpec((L, vw), lambda b, c: (b * nc + c, 2 * qw // vw)),
                  pl.BlockSpec((L, vw), lambda b, c: (b * nc + c, 2 * qw // vw + 1)),
                  pl.BlockSpec((L, LANES), lambda b, c: (b * nc + c, 0)),
                  pl.BlockSpec((SUBLANES, L), lambda b, c: (0, b * nc + c)),
                  pl.BlockSpec((1, LANES), lambda b, c: (0, 0)),
                  pl.BlockSpec((SUBLANES, 1), lambda b, c: (0, 0)),
                  pl.BlockSpec((1, vw), lambda b, c: (0, 0))],
        out_specs=pl.BlockSpec((L, vw), lambda b, c: (b * nc + c, 0)),
        out_shape=jax.ShapeDtypeStruct((T, vw), BF16),
        scratch_shapes=[pltpu.VMEM((NH, DQK, DV), F32), pltpu.VMEM((SUBLANES, DQK), F32),
                        pltpu.VMEM((SUBLANES, LANES), F32)],
        compiler_params=_cparams(("parallel", "arbitrary")),
        name="mlstm_mix",
    )(proj, proj, proj, proj, gates, gates_t, b_col, b_row, norm_g)


def _layer_norm(z, g, b):
    mu = jnp.mean(z, axis=-1, keepdims=True)
    zc = z - mu
    var = jnp.mean(zc * zc, axis=-1, keepdims=True)
    return zc * lax.rsqrt(var + LN_EPS) * g + b


def _mix_out_router_kernel(y_ref, w_ref, x_ref, g_ref, b_ref, rwt_ref, rb_ref,
                           x1_ref, et_ref, gt_ref, pt_ref, cnt_ref, *, tm, E, K):
    y = jnp.dot(y_ref[...], w_ref[...], preferred_element_type=F32)
    x1 = _layer_norm(DEEPNORM_ALPHA * x_ref[...] + y, g_ref[...], b_ref[...])
    x1_ref[...] = x1

    def split(a):
        hi = a.astype(BF16)
        return hi, (a - hi.astype(F32)).astype(BF16)

    x_hi, x_lo = split(x1)
    w_hi, w_lo = split(rwt_ref[...])
    nt = lambda a, b: lax.dot_general(a, b, NT_DIMS, preferred_element_type=F32)
    logits = nt(w_hi, x_hi) + nt(w_lo, x_hi) + nt(w_hi, x_lo) + rb_ref[...]
    sub = lax.broadcasted_iota(I32, logits.shape, 0).astype(F32)
    vals, hits = [], []
    cur = logits
    for _ in range(K):
        mx = jnp.max(cur, axis=0, keepdims=True)
        idx = jnp.min(jnp.where(cur == mx, sub, float(E)), axis=0, keepdims=True)
        hit = sub == idx
        vals.append(mx)
        hits.append(hit)
        cur = jnp.where(hit, NEG_INF, cur)

    chosen = jnp.zeros(logits.shape, F32)
    for hit in hits:
        chosen = jnp.where(hit, 1.0, chosen)
    r_ = lax.broadcasted_iota(I32, (tm, tm), 0)
    c_ = lax.broadcasted_iota(I32, (tm, tm), 1)
    before = (r_ < c_).astype(BF16)
    pos = jnp.dot(chosen.astype(BF16), before, preferred_element_type=F32)
    cnt_ref[...] = jnp.broadcast_to(jnp.sum(chosen, axis=1, keepdims=True), cnt_ref.shape)

    ex = [jnp.exp(v - vals[0]) for v in vals]
    den = ex[0]
    for e_ in ex[1:]:
        den = den + e_
    for r in range(K):
        gt_ref[r:r + 1, :] = ex[r] / den
        et_ref[r:r + 1, :] = jnp.sum(jnp.where(hits[r], sub, 0.0), axis=0, keepdims=True).astype(I32)
        pt_ref[r:r + 1, :] = jnp.sum(jnp.where(hits[r], pos, 0.0), axis=0, keepdims=True).astype(I32)


def _mix_out_router(y, w_bf16, x, ln_g, ln_b, router_w, router_b):
    T, D = x.shape
    Kin = y.shape[1]
    E, K = MOE_EXPERTS, MOE_TOP_K
    tm = MOE_TILE
    kern = functools.partial(_mix_out_router_kernel, tm=tm, E=E, K=K)
    n_tiles = T // tm
    row = lambda i: (i, 0)
    fixed = lambda i: (0, 0)
    tok = lambda i: (0, i)
    return pl.pallas_call(
        kern,
        grid=(n_tiles,),
        in_specs=[pl.BlockSpec((tm, Kin), row), pl.BlockSpec((Kin, D), fixed),
                  pl.BlockSpec((tm, D), row), pl.BlockSpec((1, D), fixed), pl.BlockSpec((1, D), fixed),
                  pl.BlockSpec((E, D), fixed), pl.BlockSpec((E, 1), fixed)],
        out_specs=[pl.BlockSpec((tm, D), row), pl.BlockSpec((K, tm), tok), pl.BlockSpec((K, tm), tok),
                   pl.BlockSpec((K, tm), tok), pl.BlockSpec((E, LANES), tok)],
        out_shape=[jax.ShapeDtypeStruct((T, D), F32), jax.ShapeDtypeStruct((K, T), I32),
                   jax.ShapeDtypeStruct((K, T), F32), jax.ShapeDtypeStruct((K, T), I32),
                   jax.ShapeDtypeStruct((E, n_tiles * LANES), F32)],
        compiler_params=_cparams(("parallel",)),
        name="mix_out_router",
    )(y, w_bf16, x, ln_g.reshape(1, D), ln_b.reshape(1, D), router_w.T, router_b.reshape(E, 1))


def _segment(tables, tile, e, E):
    c8_ref, lo_ref, gd_ref = tables
    idx = tile * E + e
    n = pl.multiple_of(c8_ref[idx], SUBLANES)
    local = pl.multiple_of(lo_ref[idx], SUBLANES)
    glob = pl.multiple_of(gd_ref[idx], SUBLANES)
    return n, local, glob


def _dispatch_kernel(c8_ref, lo_ref, gd_ref, tr_ref, zs_ref, zv_ref, nv_ref, lpos_ref, x_ref, xr_ref,
                     sorted_s, zbuf, sem, zsem, *, tm, K, E, RB, n_blocks, LR, CH):
    i = pl.program_id(0)
    tables = (c8_ref, lo_ref, gd_ref)

    def zero_copy(start):
        return pltpu.make_async_copy(zbuf, xr_ref.at[pl.ds(pl.multiple_of(start, RB), RB), :], zsem)

    @pl.when(i == 0)
    def _():
        zbuf[...] = jnp.zeros_like(zbuf)
        for e in range(E):
            @pl.when(zv_ref[e] > 0)
            def _():
                zero_copy(zs_ref[e]).start()

        def tail_start(p, carry):
            zero_copy(p * RB).start()
            return carry

        def tail_wait(p, carry):
            zero_copy(p * RB).wait()
            return carry

        lax.fori_loop(nv_ref[0], n_blocks, tail_start, 0)
        for e in range(E):
            @pl.when(zv_ref[e] > 0)
            def _():
                zero_copy(zs_ref[e]).wait()
        lax.fori_loop(nv_ref[0], n_blocks, tail_wait, 0)

    slot = i % 2

    def seg_copy(tile, e, s):
        n, local, glob = _segment(tables, tile, e, E)
        return pltpu.make_async_copy(sorted_s.at[s, pl.ds(local, n), :], xr_ref.at[pl.ds(glob, n), :], sem.at[s])

    def for_segments(tile, s, fn):
        for e in range(E):
            @pl.when(c8_ref[tile * E + e] > 0)
            def _():
                fn(seg_copy(tile, e, s))

    def wait_tile(tile, s):
        n = pl.multiple_of(tr_ref[tile], SUBLANES)
        pltpu.make_async_copy(sorted_s.at[s, pl.ds(0, n), :], xr_ref.at[pl.ds(0, n), :], sem.at[s]).wait()

    @pl.when(i >= 2)
    def _():
        wait_tile(i - 2, slot)

    lpos = lpos_ref[...]
    xb = x_ref[...].astype(BF16)
    for c in range(LR // CH):
        rows = lax.broadcasted_iota(I32, (CH, tm), 0) + c * CH
        hit = jnp.zeros((CH, tm), F32)
        for r in range(K):
            hit = hit + jnp.where(rows == lpos[r:r + 1, :], 1.0, 0.0)
        sorted_s[slot, c * CH:(c + 1) * CH, :] = jnp.dot(hit.astype(BF16), xb, preferred_element_type=F32)
    for_segments(i, slot, lambda cp: cp.start())

    @pl.when(i == pl.num_programs(0) - 1)
    def _():
        @pl.when(i >= 1)
        def _():
            wait_tile(i - 1, 1 - slot)
        wait_tile(i, slot)


def _dispatch(x1, lpos_t, seg_tables, zero_start, zero_valid, n_valid, n_rows):
    T, D = x1.shape
    K, E, RB = MOE_TOP_K, MOE_EXPERTS, ROW_BLOCK
    tm = MOE_TILE
    LR = _local_rows(tm)
    kern = functools.partial(_dispatch_kernel, tm=tm, K=K, E=E, RB=RB, n_blocks=n_rows // RB, LR=LR, CH=SORT_CHUNK)
    grid_spec = pltpu.PrefetchScalarGridSpec(
        num_scalar_prefetch=7,
        grid=(T // tm,),
        in_specs=[pl.BlockSpec((K, tm), lambda i, *_: (0, i)),
                  pl.BlockSpec((tm, D), lambda i, *_: (i, 0))],
        out_specs=pl.BlockSpec(memory_space=pl.ANY),
        scratch_shapes=[pltpu.VMEM((2, LR, D), F32), pltpu.VMEM((RB, D), F32),
                        pltpu.SemaphoreType.DMA((2,)), pltpu.SemaphoreType.DMA],
    )
    return pl.pallas_call(
        kern,
        grid_spec=grid_spec,
        out_shape=jax.ShapeDtypeStruct((n_rows, D), F32),
        compiler_params=pltpu.CompilerParams(dimension_semantics=("arbitrary",),
                                             vmem_limit_bytes=VMEM_LIMIT_BYTES,
                                             has_side_effects=True),
        name="moe_dispatch",
    )(*seg_tables, zero_start, zero_valid, n_valid, lpos_t, x1)


def _expert_kernel(be_ref, nv_ref, nxt_ref, par_ref, x_ref, wgu_hbm, bg_ref, bl_ref, wd_hbm, bd_ref, y_ref,
                   wgu_buf, wd_buf, wg_s, wl_s, wd_s, sem, *, layer):
    p = pl.program_id(0)
    valid = p < nv_ref[0]
    fresh = (p == 0) | (be_ref[p] != be_ref[jnp.maximum(p - 1, 0)])
    half = MXU_DIM // 2

    def weight_copies(e, s):
        return (pltpu.make_async_copy(wgu_hbm.at[layer, e], wgu_buf.at[s], sem.at[0, s]),
                pltpu.make_async_copy(wd_hbm.at[layer, e], wd_buf.at[s], sem.at[1, s]))

    @pl.when(valid & fresh)
    def _():
        s = par_ref[p]

        @pl.when(p == 0)
        def _():
            for cp in weight_copies(be_ref[0], s):
                cp.start()

        @pl.when(nxt_ref[p] >= 0)
        def _():
            for cp in weight_copies(nxt_ref[p], 1 - s):
                cp.start()

        for cp in weight_copies(be_ref[p], s):
            cp.wait()

        r_ = lax.broadcasted_iota(I32, (MXU_DIM, MXU_DIM), 0)
        c_ = lax.broadcasted_iota(I32, (MXU_DIM, MXU_DIM), 1)
        src = jnp.where(c_ < half, 2 * c_, 2 * (c_ - half) + 1)
        perm = jnp.where(r_ == src, 1.0, 0.0).astype(BF16)
        for cb in range(wgu_buf.shape[2] // MXU_DIM):
            blk = wgu_buf[s, :, cb * MXU_DIM:(cb + 1) * MXU_DIM].astype(BF16)
            sep = jnp.dot(blk, perm, preferred_element_type=F32).astype(BF16)
            wg_s[:, cb * half:(cb + 1) * half] = sep[:, :half]
            wl_s[:, cb * half:(cb + 1) * half] = sep[:, half:]
        wd_s[...] = wd_buf[s].astype(BF16)

    @pl.when(valid)
    def _():
        x = x_ref[...].astype(BF16)
        hg = jnp.dot(x, wg_s[...], preferred_element_type=F32) + bg_ref[...]
        hl = jnp.dot(x, wl_s[...], preferred_element_type=F32) + bl_ref[...]
        xg = jnp.minimum(hg, SWIGLU_LIMIT)
        xl = jnp.clip(hl, -SWIGLU_LIMIT, SWIGLU_LIMIT)
        act = xg * jax.nn.sigmoid(SWIGLU_ALPHA * xg) * (xl + 1.0)
        y_ref[...] = jnp.dot(act.astype(BF16), wd_s[...], preferred_element_type=F32) + bd_ref[...]

    @pl.when(jnp.logical_not(valid))
    def _():
        y_ref[...] = jnp.zeros_like(y_ref)


def _experts(x_rows, blk_e, n_valid, blk_next, blk_parity, layer, w_gate_up, b_glu, b_lin, w_down, b_down):
    n_rows, D = x_rows.shape
    _, E, _, F2 = w_gate_up.shape
    F = F2 // 2
    RB = ROW_BLOCK
    n_blocks = n_rows // RB
    xblk = lambda p, be, nv, *_: (jnp.minimum(p, nv[0] - 1), 0)
    yblk = lambda p, *_: (p, 0)
    bsel = lambda p, be, *_: (layer, be[p], 0, 0)
    grid_spec = pltpu.PrefetchScalarGridSpec(
        num_scalar_prefetch=4,
        grid=(n_blocks,),
        in_specs=[pl.BlockSpec((RB, D), xblk),
                  pl.BlockSpec(memory_space=pl.ANY),
                  pl.BlockSpec((None, None, 1, F), bsel), pl.BlockSpec((None, None, 1, F), bsel),
                  pl.BlockSpec(memory_space=pl.ANY), pl.BlockSpec((None, None, 1, D), bsel)],
        out_specs=pl.BlockSpec((RB, D), yblk),
        scratch_shapes=[pltpu.VMEM((2, D, F2), F32), pltpu.VMEM((2, F, D), F32),
                        pltpu.VMEM((D, F), BF16), pltpu.VMEM((D, F), BF16), pltpu.VMEM((F, D), BF16),
                        pltpu.SemaphoreType.DMA((2, 2))],
    )
    return pl.pallas_call(
        functools.partial(_expert_kernel, layer=layer),
        grid_spec=grid_spec,
        out_shape=jax.ShapeDtypeStruct((n_rows, D), F32),
        compiler_params=_cparams(("arbitrary",), EXPERT_VMEM_LIMIT_BYTES),
        name="moe_experts",
    )(blk_e, n_valid, blk_next, blk_parity, x_rows, w_gate_up, b_glu, b_lin, w_down, b_down)


def _combine_kernel(c8_ref, lo_ref, gd_ref, tr_ref, y_ref, lpos_ref, gate_ref, x_ref, g_ref, b_ref, o_ref,
                    ys, sem, *, tm, K, E, LR, CH):
    i = pl.program_id(0)
    n_tiles = pl.num_programs(0)
    slot = i % 2
    tables = (c8_ref, lo_ref, gd_ref)

    def seg_copy(tile, e, s):
        n, local, glob = _segment(tables, tile, e, E)
        return pltpu.make_async_copy(y_ref.at[pl.ds(glob, n), :], ys.at[s, pl.ds(local, n), :], sem.at[s])

    def for_segments(tile, s, fn):
        for e in range(E):
            @pl.when(c8_ref[tile * E + e] > 0)
            def _():
                fn(seg_copy(tile, e, s))

    @pl.when(i == 0)
    def _():
        ys[...] = jnp.zeros_like(ys)
        for_segments(0, 0, lambda cp: cp.start())

    @pl.when(i + 1 < n_tiles)
    def _():
        for_segments(i + 1, 1 - slot, lambda cp: cp.start())

    n_rows = pl.multiple_of(tr_ref[i], SUBLANES)
    pltpu.make_async_copy(y_ref.at[pl.ds(0, n_rows), :], ys.at[slot, pl.ds(0, n_rows), :], sem.at[slot]).wait()

    lpos = lpos_ref[...]
    gate = gate_ref[...]
    f = jnp.zeros((tm, x_ref.shape[1]), F32)
    for c in range(LR // CH):
        cols = lax.broadcasted_iota(I32, (tm, CH), 1) + c * CH
        w = jnp.zeros((tm, CH), F32)
        for r in range(K):
            w = w + jnp.where(cols == lpos[:, r:r + 1], gate[:, r:r + 1], 0.0)
        f = f + jnp.dot(w.astype(BF16), ys[slot, c * CH:(c + 1) * CH, :].astype(BF16),
                        preferred_element_type=F32)
    o_ref[...] = _layer_norm(DEEPNORM_ALPHA * x_ref[...] + f, g_ref[...], b_ref[...])


def _combine(y_rows, lpos_c, seg_tables, x1, gates_c, ln_g, ln_b):
    T, D = x1.shape
    K, E = MOE_TOP_K, MOE_EXPERTS
    tm = MOE_TILE
    LR = _local_rows(tm)
    kern = functools.partial(_combine_kernel, tm=tm, K=K, E=E, LR=LR, CH=SORT_CHUNK)
    grid_spec = pltpu.PrefetchScalarGridSpec(
        num_scalar_prefetch=4,
        grid=(T // tm,),
        in_specs=[pl.BlockSpec(memory_space=pl.ANY),
                  pl.BlockSpec((tm, K), lambda i, *_: (i, 0)),
                  pl.BlockSpec((tm, K), lambda i, *_: (i, 0)),
                  pl.BlockSpec((tm, D), lambda i, *_: (i, 0)),
                  pl.BlockSpec((1, D), lambda i, *_: (0, 0)), pl.BlockSpec((1, D), lambda i, *_: (0, 0))],
        out_specs=pl.BlockSpec((tm, D), lambda i, *_: (i, 0)),
        scratch_shapes=[pltpu.VMEM((2, LR, D), F32), pltpu.SemaphoreType.DMA((2,))],
    )
    return pl.pallas_call(
        kern,
        grid_spec=grid_spec,
        out_shape=jax.ShapeDtypeStruct((T, D), F32),
        compiler_params=_cparams(("arbitrary",)),
        name="moe_combine",
    )(*seg_tables, y_rows, lpos_c, gates_c, x1, ln_g.reshape(1, D), ln_b.reshape(1, D))


def _moe(x1, e_t, g_t, p_t, cnt, layer, w_gate_up, b_glu, b_lin, w_down, b_down, ln_g, ln_b):
    T, D = x1.shape
    E, K, RB = MOE_EXPERTS, MOE_TOP_K, ROW_BLOCK
    tm = MOE_TILE
    n_tiles = T // tm
    n_blocks = -(-(T * K + n_tiles * E * (SUBLANES - 1) + E * (RB - 1)) // RB)
    n_rows = n_blocks * RB

    seg_len = (cnt[:, ::LANES].T.astype(I32) + SUBLANES - 1) // SUBLANES * SUBLANES
    local_off = jnp.cumsum(seg_len, axis=1) - seg_len
    group_rows = jnp.sum(seg_len, axis=0)
    padded = (group_rows + RB - 1) // RB * RB
    pad_end = jnp.cumsum(padded)
    pad_start = pad_end - padded
    global_off = pad_start[None, :] + jnp.cumsum(seg_len, axis=0) - seg_len
    seg_tables = (seg_len.reshape(-1), local_off.reshape(-1).astype(I32), global_off.reshape(-1).astype(I32),
                  jnp.sum(seg_len, axis=1).astype(I32))

    experts = jnp.arange(E, dtype=I32)
    e_tiles = e_t.reshape(K, n_tiles, tm)
    off_sel = jnp.sum(jnp.where(e_tiles[..., None] == experts, local_off[None, :, None, :], 0), axis=-1)
    lpos_t = (off_sel.reshape(K, T) + p_t).astype(I32)

    n_valid = (pad_end[-1] // RB).astype(I32)
    blk_ids = jnp.minimum(jnp.arange(n_blocks, dtype=I32), n_valid - 1)
    blk_e = jnp.minimum(jnp.sum((pad_end[None, :] <= blk_ids[:, None] * RB).astype(I32), axis=1), E - 1)
    zero_start = jnp.maximum(pad_end - RB, 0).astype(I32)
    nonempty = group_rows > 0
    zero_valid = nonempty.astype(I32)
    n_valid = n_valid.reshape(1)
    later = (experts[None, :] > experts[:, None]) & nonempty[None, :]
    next_e = jnp.min(jnp.where(later, experts[None, :], E), axis=1)
    next_e = jnp.where(next_e < E, next_e, -1).astype(I32)
    parity_e = ((jnp.cumsum(nonempty.astype(I32)) - nonempty.astype(I32)) % 2).astype(I32)
    at_block = blk_e[:, None] == experts[None, :]
    blk_next = jnp.sum(jnp.where(at_block, next_e[None, :], 0), axis=1).astype(I32)
    blk_parity = jnp.sum(jnp.where(at_block, parity_e[None, :], 0), axis=1).astype(I32)

    x_rows = _dispatch(x1, lpos_t, seg_tables, zero_start, zero_valid, n_valid, n_rows)
    y_rows = _experts(x_rows, blk_e, n_valid, blk_next, blk_parity, layer, w_gate_up, b_glu, b_lin, w_down, b_down)
    return _combine(y_rows, lpos_t.T, seg_tables, x1, g_t.T, ln_g, ln_b)


def kernel(x, positions, attn_w_qkv, attn_w_o, mlstm_w_in, mlstm_b_gates, mlstm_norm_g, mlstm_w_out,
           ln_mix_g, ln_mix_b, ln_ffn_g, ln_ffn_b, router_w, router_b,
           w_gate_up, b_gate_up, w_down, b_down):
    B, S, D = x.shape
    T = B * S
    xt = x.reshape(T, D)
    NH = MLSTM_HEADS
    b_glu = b_gate_up[:, :, None, 0::2]
    b_lin = b_gate_up[:, :, None, 1::2]
    b_dn = b_down[:, :, None, :]

    for layer in range(DEPTH):
        slot = layer // 2
        if layer % 2 == 0:
            hd = attn_w_o.shape[1]
            qkv = _qkv_rope(xt, positions.reshape(T, 1), attn_w_qkv[slot].astype(BF16), hd)
            mixed = _moba_attention(qkv, B, S, hd)
            w_out = attn_w_o[slot].astype(BF16)
        else:
            vw = mlstm_w_out.shape[1]
            qw = vw // 2
            w_in = mlstm_w_in[slot]
            dqk = qw // NH
            col_scale = jnp.concatenate([jnp.ones((qw,), F32), jnp.full((qw,), dqk ** -0.5, F32),
                                         jnp.ones((2 * vw,), F32)])
            w_main = (w_in[:, :2 * qw + 2 * vw] * col_scale).astype(BF16)
            w_gate = jnp.pad(w_in[:, 2 * qw + 2 * vw:], ((0, 0), (0, LANES - 2 * NH))).astype(BF16)
            proj, gates = _mlstm_proj(xt, w_main, w_gate)
            bg = mlstm_b_gates[slot]
            b_col = jnp.pad(bg, (0, LANES - 2 * NH)).reshape(1, LANES)
            b_row = bg.reshape(2 * NH, 1)
            mixed = _mlstm_mix(proj, gates, gates[:, :2 * NH].T, b_col, b_row,
                               mlstm_norm_g[slot].reshape(1, vw), B, S)
            w_out = mlstm_w_out[slot].astype(BF16)
        x1, e_t, g_t, p_t, cnt = _mix_out_router(mixed, w_out, xt, ln_mix_g[layer], ln_mix_b[layer],
                                                 router_w[layer], router_b[layer])
        xt = _moe(x1, e_t, g_t, p_t, cnt, layer, w_gate_up, b_glu, b_lin, w_down, b_dn,
                  ln_ffn_g[layer], ln_ffn_b[layer])
    return xt.reshape(B, S, D)
```

```python
import functools

import jax
import jax.numpy as jnp
from jax import lax
from jax.experimental import pallas as pl
from jax.experimental.pallas import tpu as pltpu

F32 = jnp.float32
BF16 = jnp.bfloat16
I32 = jnp.int32

DEPTH = 2
ATTN_HEAD_DIM = 64
ROT_DIM = ATTN_HEAD_DIM // 4
ROPE_THETA = 500000.0
MOBA_BLOCK = 256
MOBA_TOP_K = 3
MLSTM_HEADS = 4
MOE_EXPERTS = 32
MOE_TOP_K = 4
SWIGLU_LIMIT = 7.0
SWIGLU_ALPHA = 1.702
DEEPNORM_ALPHA = (2 * DEPTH) ** 0.25
LN_EPS = 1e-5

LANES = 128
SUBLANES = 8
BF16_SUBLANES = 16
MXU_DIM = 256
VMEM_LIMIT_BYTES = 48 * 1024 * 1024
EXPERT_VMEM_LIMIT_BYTES = 58 * 1024 * 1024

PROJ_TM = 512
PROJ_TN = 512
MOE_TILE = 512
MLSTM_L = 256
ROW_BLOCK = 512
SORT_CHUNK = 256
SEG_ALIGN = 16

NT_DIMS = (((1,), (1,)), ((), ()))
TN_DIMS = (((0,), (0,)), ((), ()))
NEG_INF = float("-inf")
MASKED = -1e30
LOG2_E = 1.4426950408889634


def _cparams(sem, vmem=VMEM_LIMIT_BYTES):
    return pltpu.CompilerParams(dimension_semantics=sem, vmem_limit_bytes=vmem)


def _local_rows(tm):
    rows = tm * MOE_TOP_K + MOE_EXPERTS * SEG_ALIGN
    return -(-rows // SORT_CHUNK) * SORT_CHUNK


def _qkv_rope_kernel(pos_ref, invf_ref, x_ref, w_ref, o_ref, *, n_rope_chunks, n_q_chunks, tn):
    ang = pos_ref[...].astype(F32) * invf_ref[...]
    d = lax.broadcasted_iota(I32, ang.shape, 1) & (ATTN_HEAD_DIM - 1)
    cos = jnp.cos(ang)
    sin = jnp.sin(ang)
    half = ROT_DIM // 2
    c_tab = jnp.where(d < ROT_DIM, cos, 1.0)
    s_up = jnp.where(d < half, -sin, 0.0)
    s_dn = jnp.where((d >= half) & (d < ROT_DIM), sin, 0.0)

    xb = x_ref[...].astype(BF16)
    for c in range(w_ref.shape[1] // tn):
        acc = jnp.dot(xb, w_ref[:, c * tn:(c + 1) * tn], preferred_element_type=F32)
        if c >= n_rope_chunks:
            o_ref[:, c * tn:(c + 1) * tn] = acc.astype(o_ref.dtype)
            continue
        scale = ATTN_HEAD_DIM ** -0.5 * LOG2_E if c < n_q_chunks else 1.0
        for s in range(tn // LANES):
            blk = acc[:, s * LANES:(s + 1) * LANES]
            r = (blk * c_tab + pltpu.roll(blk, LANES - half, 1) * s_up + pltpu.roll(blk, half, 1) * s_dn)
            lo = c * tn + s * LANES
            o_ref[:, lo:lo + LANES] = (r * scale).astype(o_ref.dtype)


def _qkv_rope(x, pos, w_bf16, hd):
    T, D = x.shape
    N = w_bf16.shape[1]
    tm, tn = PROJ_TM, PROJ_TN
    inv_freq = ROPE_THETA ** (-jnp.arange(0, ROT_DIM, 2, dtype=F32) / ROT_DIM)
    lane_d = jnp.arange(LANES) % ATTN_HEAD_DIM
    invf = jnp.where(lane_d < ROT_DIM, inv_freq[lane_d % (ROT_DIM // 2)], 0.0).reshape(1, LANES).astype(F32)
    kern = functools.partial(_qkv_rope_kernel, n_rope_chunks=2 * hd // tn, n_q_chunks=hd // tn, tn=tn)
    return pl.pallas_call(
        kern,
        grid=(T // tm,),
        in_specs=[pl.BlockSpec((tm, 1), lambda i: (i, 0)),
                  pl.BlockSpec((1, LANES), lambda i: (0, 0)),
                  pl.BlockSpec((tm, D), lambda i: (i, 0)),
                  pl.BlockSpec((D, N), lambda i: (0, 0))],
        out_specs=pl.BlockSpec((tm, N), lambda i: (i, 0)),
        out_shape=jax.ShapeDtypeStruct((T, N), BF16),
        compiler_params=_cparams(("parallel",)),
        name="qkv_rope",
    )(pos, invf, x, w_bf16)


def _moba_kernel(q_ref, k_ref, v_ref, o_ref, kmh_ref, kml_ref, vt_ref, q2t_ref, bias_ref,
                 m_ref, acc_ref, *score_bufs, nb, blk):
    s_refs, smax_refs = score_bufs[:4], score_bufs[4:]
    i = pl.program_id(2)
    hd = ATTN_HEAD_DIM
    nbp = kmh_ref.shape[0]
    va = vt_ref.shape[2]

    @pl.when(i == 0)
    def _():
        rows = [jnp.mean(k_ref[j * blk:(j + 1) * blk, :].astype(F32), axis=0, keepdims=True)
                for j in range(nb)]
        if nbp > nb:
            rows.append(jnp.zeros((nbp - nb, LANES), F32))
        km = jnp.concatenate(rows, axis=0)
        hi = km.astype(BF16)
        kmh_ref[...] = hi
        kml_ref[...] = (km - hi.astype(F32)).astype(BF16)
        sub_v = lax.broadcasted_iota(I32, (va - hd, blk), 0)
        ones_row = jnp.where(sub_v == 0, 1.0, 0.0).astype(BF16)
        for j in range(nb):
            vt = v_ref[j * blk:(j + 1) * blk, :].astype(F32).T.astype(BF16)
            vt_ref[j, 0] = jnp.concatenate([vt[:hd], ones_row], axis=0)
            vt_ref[j, 1] = jnp.concatenate([vt[hd:], ones_row], axis=0)

    qt = q_ref[...].astype(F32).T
    sub_q = lax.broadcasted_iota(I32, qt.shape, 0)
    zero = jnp.zeros_like(qt)
    q2t = jnp.concatenate([jnp.where(sub_q < hd, qt, zero), jnp.where(sub_q >= hd, qt, zero)],
                          axis=1).astype(BF16)
    q2t_ref[...] = q2t

    gate = (jnp.dot(kmh_ref[...], q2t, preferred_element_type=F32)
            + jnp.dot(kml_ref[...], q2t, preferred_element_type=F32))
    sub = lax.broadcasted_iota(I32, gate.shape, 0)
    sub_f = sub.astype(F32)
    g = jnp.where(sub < i, gate, NEG_INF)
    sel = jnp.zeros(gate.shape, F32)
    for _ in range(MOBA_TOP_K):
        mx = jnp.max(g, axis=0, keepdims=True)
        idx = jnp.min(jnp.where(g == mx, sub_f, float(nbp)), axis=0, keepdims=True)
        hit = (sub_f == idx) & (mx > NEG_INF)
        sel = jnp.where(hit, 1.0, sel)
        g = jnp.where(hit, NEG_INF, g)
    bias_ref[...] = jnp.where(sel > 0.0, 0.0, MASKED)

    def block_bias(pos, j):
        return jnp.where(pos == 0, 0.0, bias_ref[pl.ds(j, 1), :])

    def produce(pos, buf):
        s_ref, smax_ref = buf
        if isinstance(pos, int) and pos == 0:
            ki = k_ref[pl.ds(pl.multiple_of(i * blk, blk), blk), :]
            s = jnp.dot(ki, q2t, preferred_element_type=F32)
            key = lax.broadcasted_iota(I32, s.shape, 0)
            qry = lax.broadcasted_iota(I32, s.shape, 1) & (blk - 1)
            s = jnp.where(key <= qry, s, MASKED)
            smax = jnp.max(s, axis=0, keepdims=True)
        else:
            j = jnp.minimum(pos - 1, i)
            kj = k_ref[pl.ds(pl.multiple_of(j * blk, blk), blk), :]
            s = jnp.dot(kj, q2t_ref[...], preferred_element_type=F32)
            smax = jnp.max(s, axis=0, keepdims=True) + block_bias(pos, j)
        s_ref[...] = s
        smax_ref[...] = smax

    def consume(pos, buf):
        s_ref, smax_ref = buf
        j = jnp.where(pos == 0, i, jnp.minimum(pos - 1, i))
        m_old = m_ref[...]
        m_new = jnp.maximum(m_old, smax_ref[...])
        a = jnp.exp2(m_old - m_new)
        pb = jnp.exp2(s_ref[...] - (m_new - block_bias(pos, j))).astype(BF16)
        upd = jnp.concatenate([jnp.dot(vt_ref[j, 0], pb[:, :blk], preferred_element_type=F32),
                               jnp.dot(vt_ref[j, 1], pb[:, blk:], preferred_element_type=F32)], axis=1)
        acc_ref[...] = a * acc_ref[...] + upd
        m_ref[...] = m_new

    m_ref[...] = jnp.full(m_ref.shape, MASKED, F32)
    acc_ref[...] = jnp.zeros_like(acc_ref)
    buf_a, buf_b, buf_c, buf_d = [(s_refs[n], smax_refs[n]) for n in range(4)]
    produce(0, buf_a)
    produce(1, buf_b)

    def body(t, carry):
        base = 4 * t
        produce(base + 2, buf_c)
        produce(base + 3, buf_d)
        consume(base, buf_a)
        consume(base + 1, buf_b)
        produce(base + 4, buf_a)
        produce(base + 5, buf_b)
        consume(base + 2, buf_c)
        consume(base + 3, buf_d)
        return carry

    full = (i + 1) // 4
    left = (i + 1) % 4
    lax.fori_loop(0, full, body, 0)
    base = 4 * full

    @pl.when(left == 1)
    def _():
        consume(base, buf_a)

    @pl.when(left == 2)
    def _():
        consume(base, buf_a)
        consume(base + 1, buf_b)

    @pl.when(left == 3)
    def _():
        produce(base + 2, buf_c)
        consume(base, buf_a)
        consume(base + 1, buf_b)
        consume(base + 2, buf_c)

    acc = acc_ref[...]
    out_t = acc[:hd] / acc[hd:hd + 1]
    o_ref[...] = jnp.concatenate([out_t[:, :blk], out_t[:, blk:]], axis=0).T.astype(o_ref.dtype)


def _moba_attention(qkv, B, S, hd):
    T = B * S
    blk = MOBA_BLOCK
    nb = S // blk
    nbp = -(-nb // BF16_SUBLANES) * BF16_SUBLANES
    va = ATTN_HEAD_DIM + BF16_SUBLANES
    n_pairs = hd // LANES
    kern = functools.partial(_moba_kernel, nb=nb, blk=blk)
    return pl.pallas_call(
        kern,
        grid=(B, n_pairs, nb),
        in_specs=[pl.BlockSpec((blk, LANES), lambda b, h, i: (b * nb + i, h)),
                  pl.BlockSpec((S, LANES), lambda b, h, i: (b, n_pairs + h)),
                  pl.BlockSpec((S, LANES), lambda b, h, i: (b, 2 * n_pairs + h))],
        out_specs=pl.BlockSpec((blk, LANES), lambda b, h, i: (b * nb + i, h)),
        out_shape=jax.ShapeDtypeStruct((T, hd), BF16),
        scratch_shapes=[pltpu.VMEM((nbp, LANES), BF16), pltpu.VMEM((nbp, LANES), BF16),
                        pltpu.VMEM((nb, 2, va, blk), BF16),
                        pltpu.VMEM((LANES, 2 * blk), BF16), pltpu.VMEM((nbp, 2 * blk), F32),
                        pltpu.VMEM((1, 2 * blk), F32), pltpu.VMEM((va, 2 * blk), F32)]
                       + [pltpu.VMEM((blk, 2 * blk), F32)] * 4 + [pltpu.VMEM((1, 2 * blk), F32)] * 4,
        compiler_params=_cparams(("parallel", "parallel", "arbitrary")),
        name="moba_attention",
    )(qkv, qkv, qkv)


def _mlstm_proj_kernel(x_ref, w_ref, wg_ref, o_ref, g_ref, *, tn):
    xb = x_ref[...].astype(BF16)
    for c in range(w_ref.shape[1] // tn):
        o_ref[:, c * tn:(c + 1) * tn] = jnp.dot(xb, w_ref[:, c * tn:(c + 1) * tn],
                                                preferred_element_type=F32).astype(o_ref.dtype)
    g_ref[...] = jnp.dot(xb, wg_ref[...], preferred_element_type=F32)


def _mlstm_proj(x, w_main, w_gate):
    T, D = x.shape
    N = w_main.shape[1]
    tm = PROJ_TM
    return pl.pallas_call(
        functools.partial(_mlstm_proj_kernel, tn=PROJ_TN),
        grid=(T // tm,),
        in_specs=[pl.BlockSpec((tm, D), lambda i: (i, 0)),
                  pl.BlockSpec((D, N), lambda i: (0, 0)),
                  pl.BlockSpec((D, LANES), lambda i: (0, 0))],
        out_specs=[pl.BlockSpec((tm, N), lambda i: (i, 0)),
                   pl.BlockSpec((tm, LANES), lambda i: (i, 0))],
        out_shape=[jax.ShapeDtypeStruct((T, N), BF16), jax.ShapeDtypeStruct((T, LANES), F32)],
        compiler_params=_cparams(("parallel",)),
        name="mlstm_proj",
    )(x, w_main, w_gate)


def _log_sigmoid(x):
    return jnp.minimum(x, 0.0) - jnp.log(1.0 + jnp.exp(-jnp.abs(x)))


def _mlstm_kernel(q_ref, k_ref, v_ref, o_ref, gc_ref, gr_ref, bc_ref, br_ref, ng_ref, out_ref,
                  C_ref, n_ref, m_ref, *, L, NH, DQK, DV):
    c = pl.program_id(1)

    @pl.when(c == 0)
    def _():
        C_ref[...] = jnp.zeros_like(C_ref)
        n_ref[...] = jnp.zeros_like(n_ref)
        m_ref[...] = jnp.zeros_like(m_ref)

    gcol = gc_ref[...] + bc_ref[...]
    grow = gr_ref[...] + br_ref[...]
    lane = lax.broadcasted_iota(I32, gcol.shape, 1)
    lf_col = jnp.where((lane >= NH) & (lane < 2 * NH), _log_sigmoid(gcol), 0.0)
    lf_row = _log_sigmoid(grow)
    row = lax.broadcasted_iota(I32, (L, L), 0)
    col = lax.broadcasted_iota(I32, (L, L), 1)
    causal = col <= row
    tri = jnp.where(causal, 1.0, 0.0).astype(BF16)

    def pieces(a):
        out = []
        for _ in range(3):
            p = a.astype(BF16)
            out.append(p)
            a = a - p.astype(F32)
        return out

    b_cols = sum(jnp.dot(tri, p, preferred_element_type=F32) for p in pieces(lf_col))
    b_rows = sum(lax.dot_general(p, tri, NT_DIMS, preferred_element_type=F32) for p in pieces(lf_row))

    for h in range(NH):
        i_row = grow[h:h + 1, :]
        i_col = gcol[:, h:h + 1]
        b_row = b_rows[NH + h:NH + h + 1, :]
        b_col = b_cols[:, NH + h:NH + h + 1]
        m_prev = m_ref[h:h + 1, 0:1]
        qh = q_ref[:, h * DQK:(h + 1) * DQK]
        kh = k_ref[:, h * DQK:(h + 1) * DQK]
        vh = v_ref[:, h * DV:(h + 1) * DV]

        D = jnp.where(causal, b_col - b_row + i_row, NEG_INF)
        g = b_col + m_prev
        m_t = jnp.maximum(g, jnp.max(D, axis=1, keepdims=True))
        w_inter = jnp.exp(g - m_t)
        qk = lax.dot_general(qh, kh, NT_DIMS, preferred_element_type=F32)
        A = jnp.exp(D - m_t) * qk
        num = (w_inter * jnp.dot(qh, C_ref[h].astype(BF16), preferred_element_type=F32)
               + jnp.dot(A.astype(BF16), vh, preferred_element_type=F32))
        qn = jnp.sum(qh.astype(F32) * n_ref[h:h + 1, :], axis=1, keepdims=True)
        den = w_inter * qn + jnp.sum(A, axis=1, keepdims=True)
        hh = num / jnp.maximum(jnp.abs(den), jnp.exp(-m_t))

        m_new = m_t[L - 1:L, :]
        b_last = b_col[L - 1:L, :]
        decay = jnp.exp(b_last + m_prev - m_new)
        w_s = jnp.exp(b_last - b_col + i_col - m_new)
        kw = kh.astype(F32) * w_s
        C_ref[h] = decay * C_ref[h] + lax.dot_general(kw.astype(BF16), vh, TN_DIMS,
                                                      preferred_element_type=F32)
        n_ref[h:h + 1, :] = decay * n_ref[h:h + 1, :] + jnp.sum(kw, axis=0, keepdims=True)
        m_ref[h:h + 1, :] = jnp.broadcast_to(m_new, (1, LANES))

        mu = jnp.mean(hh, axis=1, keepdims=True)
        hc = hh - mu
        var = jnp.mean(hc * hc, axis=1, keepdims=True)
        hn = hc * lax.rsqrt(var + LN_EPS) * ng_ref[:, h * DV:(h + 1) * DV]
        og = jax.nn.sigmoid(o_ref[:, h * DV:(h + 1) * DV].astype(F32))
        out_ref[:, h * DV:(h + 1) * DV] = (og * hn).astype(out_ref.dtype)


def _mlstm_mix(proj, gates, gates_t, b_col, b_row, norm_g, B, S):
    T = B * S
    NH = MLSTM_HEADS
    DV = norm_g.shape[1] // NH
    DQK = DV // 2
    L = MLSTM_L
    nc = S // L
    qw = NH * DQK
    vw = NH * DV
    kern = functools.partial(_mlstm_kernel, L=L, NH=NH, DQK=DQK, DV=DV)
    return pl.pallas_call(
        kern,
        grid=(B, nc),
        in_specs=[pl.BlockSpec((L, qw), lambda b, c: (b * nc + c, 0)),
                  pl.BlockSpec((L, qw), lambda b, c: (b * nc + c, 1)),
                  pl.BlockSpec((L, vw), lambda b, c: (b * nc + c, 2 * qw // vw)),
                  pl.BlockSpec((L, vw), lambda b, c: (b * nc + c, 2 * qw // vw + 1)),
                  pl.BlockSpec((L, LANES), lambda b, c: (b * nc + c, 0)),
                  pl.BlockSpec((SUBLANES, L), lambda b, c: (0, b * nc + c)),
                  pl.BlockSpec((1, LANES), lambda b, c: (0, 0)),
                  pl.BlockSpec((SUBLANES, 1), lambda b, c: (0, 0)),
                  pl.BlockSpec((1, vw), lambda b, c: (0, 0))],
        out_specs=pl.BlockSpec((L, vw), lambda b, c: (b * nc + c, 0)),
        out_shape=jax.ShapeDtypeStruct((T, vw), BF16),
        scratch_shapes=[pltpu.VMEM((NH, DQK, DV), F32), pltpu.VMEM((SUBLANES, DQK), F32),
                        pltpu.VMEM((SUBLANES, LANES), F32)],
        compiler_params=_cparams(("parallel", "arbitrary")),
        name="mlstm_mix",
    )(proj, proj, proj, proj, gates, gates_t, b_col, b_row, norm_g)


def _layer_norm(z, g, b):
    mu = jnp.mean(z, axis=-1, keepdims=True)
    zc = z - mu
    var = jnp.mean(zc * zc, axis=-1, keepdims=True)
    return zc * lax.rsqrt(var + LN_EPS) * g + b


def _mix_out_router_kernel(y_ref, w_ref, x_ref, g_ref, b_ref, rwt_ref, rb_ref,
                           x1_ref, et_ref, gt_ref, pt_ref, cnt_ref, *, tm, E, K):
    y = jnp.dot(y_ref[...], w_ref[...], preferred_element_type=F32)
    x1 = _layer_norm(DEEPNORM_ALPHA * x_ref[...] + y, g_ref[...], b_ref[...])
    x1_ref[...] = x1

    def split(a):
        hi = a.astype(BF16)
        return hi, (a - hi.astype(F32)).astype(BF16)

    x_hi, x_lo = split(x1)
    w_hi, w_lo = split(rwt_ref[...])
    nt = lambda a, b: lax.dot_general(a, b, NT_DIMS, preferred_element_type=F32)
    logits = nt(w_hi, x_hi) + nt(w_lo, x_hi) + nt(w_hi, x_lo) + rb_ref[...]
    sub = lax.broadcasted_iota(I32, logits.shape, 0).astype(F32)
    vals, hits = [], []
    cur = logits
    for _ in range(K):
        mx = jnp.max(cur, axis=0, keepdims=True)
        idx = jnp.min(jnp.where(cur == mx, sub, float(E)), axis=0, keepdims=True)
        hit = sub == idx
        vals.append(mx)
        hits.append(hit)
        cur = jnp.where(hit, NEG_INF, cur)

    chosen = jnp.zeros(logits.shape, F32)
    for hit in hits:
        chosen = jnp.where(hit, 1.0, chosen)
    r_ = lax.broadcasted_iota(I32, (tm, tm), 0)
    c_ = lax.broadcasted_iota(I32, (tm, tm), 1)
    before = (r_ < c_).astype(BF16)
    pos = jnp.dot(chosen.astype(BF16), before, preferred_element_type=F32)
    cnt_ref[...] = jnp.broadcast_to(jnp.sum(chosen, axis=1, keepdims=True), cnt_ref.shape)

    ex = [jnp.exp(v - vals[0]) for v in vals]
    den = ex[0]
    for e_ in ex[1:]:
        den = den + e_
    for r in range(K):
        gt_ref[r:r + 1, :] = ex[r] / den
        et_ref[r:r + 1, :] = jnp.sum(jnp.where(hits[r], sub, 0.0), axis=0, keepdims=True).astype(I32)
        pt_ref[r:r + 1, :] = jnp.sum(jnp.where(hits[r], pos, 0.0), axis=0, keepdims=True).astype(I32)


def _mix_out_router(y, w_bf16, x, ln_g, ln_b, router_w, router_b):
    T, D = x.shape
    Kin = y.shape[1]
    E, K = MOE_EXPERTS, MOE_TOP_K
    tm = MOE_TILE
    kern = functools.partial(_mix_out_router_kernel, tm=tm, E=E, K=K)
    n_tiles = T // tm
    row = lambda i: (i, 0)
    fixed = lambda i: (0, 0)
    tok = lambda i: (0, i)
    return pl.pallas_call(
        kern,
        grid=(n_tiles,),
        in_specs=[pl.BlockSpec((tm, Kin), row), pl.BlockSpec((Kin, D), fixed),
                  pl.BlockSpec((tm, D), row), pl.BlockSpec((1, D), fixed), pl.BlockSpec((1, D), fixed),
                  pl.BlockSpec((E, D), fixed), pl.BlockSpec((E, 1), fixed)],
        out_specs=[pl.BlockSpec((tm, D), row), pl.BlockSpec((K, tm), tok), pl.BlockSpec((K, tm), tok),
                   pl.BlockSpec((K, tm), tok), pl.BlockSpec((E, LANES), tok)],
        out_shape=[jax.ShapeDtypeStruct((T, D), F32), jax.ShapeDtypeStruct((K, T), I32),
                   jax.ShapeDtypeStruct((K, T), F32), jax.ShapeDtypeStruct((K, T), I32),
                   jax.ShapeDtypeStruct((E, n_tiles * LANES), F32)],
        compiler_params=_cparams(("parallel",)),
        name="mix_out_router",
    )(y, w_bf16, x, ln_g.reshape(1, D), ln_b.reshape(1, D), router_w.T, router_b.reshape(E, 1))


def _segment(tables, tile, e, E):
    c8_ref, lo_ref, gd_ref = tables
    idx = tile * E + e
    n = pl.multiple_of(c8_ref[idx], SEG_ALIGN)
    local = pl.multiple_of(lo_ref[idx], SEG_ALIGN)
    glob = pl.multiple_of(gd_ref[idx], SEG_ALIGN)
    return n, local, glob


def _dispatch_kernel(c8_ref, lo_ref, gd_ref, zs_ref, zv_ref, nv_ref, lpos_ref, x_ref, xr_ref,
                     sorted_s, zbuf, sem, zsem, *, tm, K, E, RB, n_blocks, LR, CH):
    i = pl.program_id(0)
    tables = (c8_ref, lo_ref, gd_ref)

    def zero_copy(start):
        return pltpu.make_async_copy(zbuf, xr_ref.at[pl.ds(pl.multiple_of(start, RB), RB), :], zsem)

    @pl.when(i == 0)
    def _():
        zbuf[...] = jnp.zeros_like(zbuf)
        for e in range(E):
            @pl.when(zv_ref[e] > 0)
            def _():
                zero_copy(zs_ref[e]).start()

        def tail_start(p, carry):
            zero_copy(p * RB).start()
            return carry

        def tail_wait(p, carry):
            zero_copy(p * RB).wait()
            return carry

        lax.fori_loop(nv_ref[0], n_blocks, tail_start, 0)
        for e in range(E):
            @pl.when(zv_ref[e] > 0)
            def _():
                zero_copy(zs_ref[e]).wait()
        lax.fori_loop(nv_ref[0], n_blocks, tail_wait, 0)

    slot = i % 2

    def seg_copy(tile, e, s):
        n, local, glob = _segment(tables, tile, e, E)
        return pltpu.make_async_copy(sorted_s.at[s, pl.ds(local, n), :], xr_ref.at[pl.ds(glob, n), :], sem.at[s])

    def for_segments(tile, s, fn):
        for e in range(E):
            @pl.when(c8_ref[tile * E + e] > 0)
            def _():
                fn(seg_copy(tile, e, s))

    @pl.when(i >= 2)
    def _():
        for_segments(i - 2, slot, lambda cp: cp.wait())

    lpos = lpos_ref[...]
    xb = x_ref[...].astype(BF16)
    for c in range(LR // CH):
        rows = lax.broadcasted_iota(I32, (CH, tm), 0) + c * CH
        hit = jnp.zeros((CH, tm), F32)
        for r in range(K):
            hit = hit + jnp.where(rows == lpos[r:r + 1, :], 1.0, 0.0)
        sorted_s[slot, c * CH:(c + 1) * CH, :] = jnp.dot(hit.astype(BF16), xb,
                                                         preferred_element_type=F32).astype(BF16)
    for_segments(i, slot, lambda cp: cp.start())

    @pl.when(i == pl.num_programs(0) - 1)
    def _():
        @pl.when(i >= 1)
        def _():
            for_segments(i - 1, 1 - slot, lambda cp: cp.wait())
        for_segments(i, slot, lambda cp: cp.wait())


def _dispatch(x1, lpos_t, seg_tables, zero_start, zero_valid, n_valid, n_rows):
    T, D = x1.shape
    K, E, RB = MOE_TOP_K, MOE_EXPERTS, ROW_BLOCK
    tm = MOE_TILE
    LR = _local_rows(tm)
    kern = functools.partial(_dispatch_kernel, tm=tm, K=K, E=E, RB=RB, n_blocks=n_rows // RB, LR=LR, CH=SORT_CHUNK)
    grid_spec = pltpu.PrefetchScalarGridSpec(
        num_scalar_prefetch=6,
        grid=(T // tm,),
        in_specs=[pl.BlockSpec((K, tm), lambda i, *_: (0, i)),
                  pl.BlockSpec((tm, D), lambda i, *_: (i, 0))],
        out_specs=pl.BlockSpec(memory_space=pl.ANY),
        scratch_shapes=[pltpu.VMEM((2, LR, D), BF16), pltpu.VMEM((RB, D), BF16),
                        pltpu.SemaphoreType.DMA((2,)), pltpu.SemaphoreType.DMA],
    )
    return pl.pallas_call(
        kern,
        grid_spec=grid_spec,
        out_shape=jax.ShapeDtypeStruct((n_rows, D), BF16),
        compiler_params=pltpu.CompilerParams(dimension_semantics=("arbitrary",),
                                             vmem_limit_bytes=VMEM_LIMIT_BYTES,
                                             has_side_effects=True),
        name="moe_dispatch",
    )(*seg_tables, zero_start, zero_valid, n_valid, lpos_t, x1)


def _expert_kernel(be_ref, nv_ref, nxt_ref, par_ref, x_ref, wgu_hbm, bg_ref, bl_ref, wd_hbm, bd_ref, y_ref,
                   wgu_buf, wd_buf, wg_s, wl_s, wd_s, sem, *, layer):
    p = pl.program_id(0)
    valid = p < nv_ref[0]
    fresh = (p == 0) | (be_ref[p] != be_ref[jnp.maximum(p - 1, 0)])
    half = MXU_DIM // 2

    def weight_copies(e, s):
        return (pltpu.make_async_copy(wgu_hbm.at[layer, e], wgu_buf.at[s], sem.at[0, s]),
                pltpu.make_async_copy(wd_hbm.at[layer, e], wd_buf.at[s], sem.at[1, s]))

    @pl.when(valid & fresh)
    def _():
        s = par_ref[p]

        @pl.when(p == 0)
        def _():
            for cp in weight_copies(be_ref[0], s):
                cp.start()

        @pl.when(nxt_ref[p] >= 0)
        def _():
            for cp in weight_copies(nxt_ref[p], 1 - s):
                cp.start()

        for cp in weight_copies(be_ref[p], s):
            cp.wait()

        r_ = lax.broadcasted_iota(I32, (MXU_DIM, MXU_DIM), 0)
        c_ = lax.broadcasted_iota(I32, (MXU_DIM, MXU_DIM), 1)
        src = jnp.where(c_ < half, 2 * c_, 2 * (c_ - half) + 1)
        perm = jnp.where(r_ == src, 1.0, 0.0).astype(BF16)
        for cb in range(wgu_buf.shape[2] // MXU_DIM):
            blk = wgu_buf[s, :, cb * MXU_DIM:(cb + 1) * MXU_DIM].astype(BF16)
            sep = jnp.dot(blk, perm, preferred_element_type=F32).astype(BF16)
            wg_s[:, cb * half:(cb + 1) * half] = sep[:, :half]
            wl_s[:, cb * half:(cb + 1) * half] = sep[:, half:]
        wd_s[...] = wd_buf[s].astype(BF16)

    @pl.when(valid)
    def _():
        x = x_ref[...]
        hg = jnp.dot(x, wg_s[...], preferred_element_type=F32) + bg_ref[...]
        hl = jnp.dot(x, wl_s[...], preferred_element_type=F32) + bl_ref[...]
        xg = jnp.minimum(hg, SWIGLU_LIMIT)
        xl = jnp.clip(hl, -SWIGLU_LIMIT, SWIGLU_LIMIT)
        act = xg * jax.nn.sigmoid(SWIGLU_ALPHA * xg) * (xl + 1.0)
        y_ref[...] = (jnp.dot(act.astype(BF16), wd_s[...], preferred_element_type=F32)
                      + bd_ref[...]).astype(y_ref.dtype)

    @pl.when(jnp.logical_not(valid))
    def _():
        y_ref[...] = jnp.zeros_like(y_ref)


def _experts(x_rows, blk_e, n_valid, blk_next, blk_parity, layer, w_gate_up, b_glu, b_lin, w_down, b_down):
    n_rows, D = x_rows.shape
    _, E, _, F2 = w_gate_up.shape
    F = F2 // 2
    RB = ROW_BLOCK
    n_blocks = n_rows // RB
    xblk = lambda p, be, nv, *_: (jnp.minimum(p, nv[0] - 1), 0)
    yblk = lambda p, *_: (p, 0)
    bsel = lambda p, be, *_: (layer, be[p], 0, 0)
    grid_spec = pltpu.PrefetchScalarGridSpec(
        num_scalar_prefetch=4,
        grid=(n_blocks,),
        in_specs=[pl.BlockSpec((RB, D), xblk),
                  pl.BlockSpec(memory_space=pl.ANY),
                  pl.BlockSpec((None, None, 1, F), bsel), pl.BlockSpec((None, None, 1, F), bsel),
                  pl.BlockSpec(memory_space=pl.ANY), pl.BlockSpec((None, None, 1, D), bsel)],
        out_specs=pl.BlockSpec((RB, D), yblk),
        scratch_shapes=[pltpu.VMEM((2, D, F2), F32), pltpu.VMEM((2, F, D), F32),
                        pltpu.VMEM((D, F), BF16), pltpu.VMEM((D, F), BF16), pltpu.VMEM((F, D), BF16),
                        pltpu.SemaphoreType.DMA((2, 2))],
    )
    return pl.pallas_call(
        functools.partial(_expert_kernel, layer=layer),
        grid_spec=grid_spec,
        out_shape=jax.ShapeDtypeStruct((n_rows, D), BF16),
        compiler_params=_cparams(("arbitrary",), EXPERT_VMEM_LIMIT_BYTES),
        name="moe_experts",
    )(blk_e, n_valid, blk_next, blk_parity, x_rows, w_gate_up, b_glu, b_lin, w_down, b_down)


def _combine_kernel(c8_ref, lo_ref, gd_ref, y_ref, lpos_ref, gate_ref, x_ref, g_ref, b_ref, o_ref,
                    ys, sem, *, tm, K, E, LR, CH):
    i = pl.program_id(0)
    n_tiles = pl.num_programs(0)
    slot = i % 2
    tables = (c8_ref, lo_ref, gd_ref)

    def seg_copy(tile, e, s):
        n, local, glob = _segment(tables, tile, e, E)
        return pltpu.make_async_copy(y_ref.at[pl.ds(glob, n), :], ys.at[s, pl.ds(local, n), :], sem.at[s])

    def for_segments(tile, s, fn):
        for e in range(E):
            @pl.when(c8_ref[tile * E + e] > 0)
            def _():
                fn(seg_copy(tile, e, s))

    @pl.when(i == 0)
    def _():
        ys[...] = jnp.zeros_like(ys)
        for_segments(0, 0, lambda cp: cp.start())

    @pl.when(i + 1 < n_tiles)
    def _():
        for_segments(i + 1, 1 - slot, lambda cp: cp.start())

    for_segments(i, slot, lambda cp: cp.wait())

    lpos = lpos_ref[...]
    gate = gate_ref[...]
    f = jnp.zeros((tm, x_ref.shape[1]), F32)
    for c in range(LR // CH):
        cols = lax.broadcasted_iota(I32, (tm, CH), 1) + c * CH
        w = jnp.zeros((tm, CH), F32)
        for r in range(K):
            w = w + jnp.where(cols == lpos[:, r:r + 1], gate[:, r:r + 1], 0.0)
        f = f + jnp.dot(w.astype(BF16), ys[slot, c * CH:(c + 1) * CH, :], preferred_element_type=F32)
    o_ref[...] = _layer_norm(DEEPNORM_ALPHA * x_ref[...] + f, g_ref[...], b_ref[...])


def _combine(y_rows, lpos_c, seg_tables, x1, gates_c, ln_g, ln_b):
    T, D = x1.shape
    K, E = MOE_TOP_K, MOE_EXPERTS
    tm = MOE_TILE
    LR = _local_rows(tm)
    kern = functools.partial(_combine_kernel, tm=tm, K=K, E=E, LR=LR, CH=SORT_CHUNK)
    grid_spec = pltpu.PrefetchScalarGridSpec(
        num_scalar_prefetch=3,
        grid=(T // tm,),
        in_specs=[pl.BlockSpec(memory_space=pl.ANY),
                  pl.BlockSpec((tm, K), lambda i, *_: (i, 0)),
                  pl.BlockSpec((tm, K), lambda i, *_: (i, 0)),
                  pl.BlockSpec((tm, D), lambda i, *_: (i, 0)),
                  pl.BlockSpec((1, D), lambda i, *_: (0, 0)), pl.BlockSpec((1, D), lambda i, *_: (0, 0))],
        out_specs=pl.BlockSpec((tm, D), lambda i, *_: (i, 0)),
        scratch_shapes=[pltpu.VMEM((2, LR, D), BF16), pltpu.SemaphoreType.DMA((2,))],
    )
    return pl.pallas_call(
        kern,
        grid_spec=grid_spec,
        out_shape=jax.ShapeDtypeStruct((T, D), F32),
        compiler_params=_cparams(("arbitrary",)),
        name="moe_combine",
    )(*seg_tables, y_rows, lpos_c, gates_c, x1, ln_g.reshape(1, D), ln_b.reshape(1, D))


def _moe(x1, e_t, g_t, p_t, cnt, layer, w_gate_up, b_glu, b_lin, w_down, b_down, ln_g, ln_b):
    T, D = x1.shape
    E, K, RB = MOE_EXPERTS, MOE_TOP_K, ROW_BLOCK
    tm = MOE_TILE
    n_tiles = T // tm
    n_blocks = -(-(T * K + n_tiles * E * (SEG_ALIGN - 1) + E * (RB - 1)) // RB)
    n_rows = n_blocks * RB

    seg_len = (cnt[:, ::LANES].T.astype(I32) + SEG_ALIGN - 1) // SEG_ALIGN * SEG_ALIGN
    local_off = jnp.cumsum(seg_len, axis=1) - seg_len
    group_rows = jnp.sum(seg_len, axis=0)
    padded = (group_rows + RB - 1) // RB * RB
    pad_end = jnp.cumsum(padded)
    pad_start = pad_end - padded
    global_off = pad_start[None, :] + jnp.cumsum(seg_len, axis=0) - seg_len
    seg_tables = (seg_len.reshape(-1), local_off.reshape(-1).astype(I32), global_off.reshape(-1).astype(I32))

    experts = jnp.arange(E, dtype=I32)
    e_tiles = e_t.reshape(K, n_tiles, tm)
    off_sel = jnp.sum(jnp.where(e_tiles[..., None] == experts, local_off[None, :, None, :], 0), axis=-1)
    lpos_t = (off_sel.reshape(K, T) + p_t).astype(I32)

    n_valid = (pad_end[-1] // RB).astype(I32)
    blk_ids = jnp.minimum(jnp.arange(n_blocks, dtype=I32), n_valid - 1)
    blk_e = jnp.minimum(jnp.sum((pad_end[None, :] <= blk_ids[:, None] * RB).astype(I32), axis=1), E - 1)
    zero_start = jnp.maximum(pad_end - RB, 0).astype(I32)
    nonempty = group_rows > 0
    zero_valid = nonempty.astype(I32)
    n_valid = n_valid.reshape(1)
    later = (experts[None, :] > experts[:, None]) & nonempty[None, :]
    next_e = jnp.min(jnp.where(later, experts[None, :], E), axis=1)
    next_e = jnp.where(next_e < E, next_e, -1).astype(I32)
    parity_e = ((jnp.cumsum(nonempty.astype(I32)) - nonempty.astype(I32)) % 2).astype(I32)
    at_block = blk_e[:, None] == experts[None, :]
    blk_next = jnp.sum(jnp.where(at_block, next_e[None, :], 0), axis=1).astype(I32)
    blk_parity = jnp.sum(jnp.where(at_block, parity_e[None, :], 0), axis=1).astype(I32)

    x_rows = _dispatch(x1, lpos_t, seg_tables, zero_start, zero_valid, n_valid, n_rows)
    y_rows = _experts(x_rows, blk_e, n_valid, blk_next, blk_parity, layer, w_gate_up, b_glu, b_lin, w_down, b_down)
    return _combine(y_rows, lpos_t.T, seg_tables, x1, g_t.T, ln_g, ln_b)


def kernel(x, positions, attn_w_qkv, attn_w_o, mlstm_w_in, mlstm_b_gates, mlstm_norm_g, mlstm_w_out,
           ln_mix_g, ln_mix_b, ln_ffn_g, ln_ffn_b, router_w, router_b,
           w_gate_up, b_gate_up, w_down, b_down):
    B, S, D = x.shape
    T = B * S
    xt = x.reshape(T, D)
    NH = MLSTM_HEADS
    b_glu = b_gate_up[:, :, None, 0::2]
    b_lin = b_gate_up[:, :, None, 1::2]
    b_dn = b_down[:, :, None, :]

    for layer in range(DEPTH):
        slot = layer // 2
        if layer % 2 == 0:
            hd = attn_w_o.shape[1]
            qkv = _qkv_rope(xt, positions.reshape(T, 1), attn_w_qkv[slot].astype(BF16), hd)
            mixed = _moba_attention(qkv, B, S, hd)
            w_out = attn_w_o[slot].astype(BF16)
        else:
            vw = mlstm_w_out.shape[1]
            qw = vw // 2
            w_in = mlstm_w_in[slot]
            dqk = qw // NH
            col_scale = jnp.concatenate([jnp.ones((qw,), F32), jnp.full((qw,), dqk ** -0.5, F32),
                                         jnp.ones((2 * vw,), F32)])
            w_main = (w_in[:, :2 * qw + 2 * vw] * col_scale).astype(BF16)
            w_gate = jnp.pad(w_in[:, 2 * qw + 2 * vw:], ((0, 0), (0, LANES - 2 * NH))).astype(BF16)
            proj, gates = _mlstm_proj(xt, w_main, w_gate)
            bg = mlstm_b_gates[slot]
            b_col = jnp.pad(bg, (0, LANES - 2 * NH)).reshape(1, LANES)
            b_row = bg.reshape(2 * NH, 1)
            mixed = _mlstm_mix(proj, gates, gates[:, :2 * NH].T, b_col, b_row,
                               mlstm_norm_g[slot].reshape(1, vw), B, S)
            w_out = mlstm_w_out[slot].astype(BF16)
        x1, e_t, g_t, p_t, cnt = _mix_out_router(mixed, w_out, xt, ln_mix_g[layer], ln_mix_b[layer],
                                                 router_w[layer], router_b[layer])
        xt = _moe(x1, e_t, g_t, p_t, cnt, layer, w_gate_up, b_glu, b_lin, w_down, b_dn,
                  ln_ffn_g[layer], ln_ffn_b[layer])
    return xt.reshape(B, S, D)
```

```python
import functools

import jax
import jax.numpy as jnp
from jax import lax
from jax.experimental import pallas as pl
from jax.experimental.pallas import tpu as pltpu

F32 = jnp.float32
BF16 = jnp.bfloat16
I32 = jnp.int32

DEPTH = 2
ATTN_HEAD_DIM = 64
ROT_DIM = ATTN_HEAD_DIM // 4
ROPE_THETA = 500000.0
MOBA_BLOCK = 256
MOBA_TOP_K = 3
MLSTM_HEADS = 4
MOE_EXPERTS = 32
MOE_TOP_K = 4
SWIGLU_LIMIT = 7.0
SWIGLU_ALPHA = 1.702
DEEPNORM_ALPHA = (2 * DEPTH) ** 0.25
LN_EPS = 1e-5

LANES = 128
SUBLANES = 8
BF16_SUBLANES = 16
MXU_DIM = 256
VMEM_LIMIT_BYTES = 48 * 1024 * 1024
EXPERT_VMEM_LIMIT_BYTES = 58 * 1024 * 1024

PROJ_TM = 512
PROJ_TN = 512
MOE_TILE = 512
MLSTM_L = 256
ROW_BLOCK = 512
SORT_CHUNK = 256

NT_DIMS = (((1,), (1,)), ((), ()))
TN_DIMS = (((0,), (0,)), ((), ()))
NEG_INF = float("-inf")
MASKED = -1e30
LOG2_E = 1.4426950408889634


def _cparams(sem, vmem=VMEM_LIMIT_BYTES):
    return pltpu.CompilerParams(dimension_semantics=sem, vmem_limit_bytes=vmem)


def _local_rows(tm):
    rows = tm * MOE_TOP_K + MOE_EXPERTS * SUBLANES
    return -(-rows // SORT_CHUNK) * SORT_CHUNK


def _qkv_rope_kernel(pos_ref, invf_ref, x_ref, w_ref, o_ref, *, n_rope_chunks, n_q_chunks, tn):
    ang = pos_ref[...].astype(F32) * invf_ref[...]
    d = lax.broadcasted_iota(I32, ang.shape, 1) & (ATTN_HEAD_DIM - 1)
    cos = jnp.cos(ang)
    sin = jnp.sin(ang)
    half = ROT_DIM // 2
    c_tab = jnp.where(d < ROT_DIM, cos, 1.0)
    s_up = jnp.where(d < half, -sin, 0.0)
    s_dn = jnp.where((d >= half) & (d < ROT_DIM), sin, 0.0)

    xb = x_ref[...].astype(BF16)
    for c in range(w_ref.shape[1] // tn):
        acc = jnp.dot(xb, w_ref[:, c * tn:(c + 1) * tn], preferred_element_type=F32)
        if c >= n_rope_chunks:
            o_ref[:, c * tn:(c + 1) * tn] = acc.astype(o_ref.dtype)
            continue
        scale = ATTN_HEAD_DIM ** -0.5 * LOG2_E if c < n_q_chunks else 1.0
        for s in range(tn // LANES):
            blk = acc[:, s * LANES:(s + 1) * LANES]
            r = (blk * c_tab + pltpu.roll(blk, LANES - half, 1) * s_up + pltpu.roll(blk, half, 1) * s_dn)
            lo = c * tn + s * LANES
            o_ref[:, lo:lo + LANES] = (r * scale).astype(o_ref.dtype)


def _qkv_rope(x, pos, w_bf16, hd):
    T, D = x.shape
    N = w_bf16.shape[1]
    tm, tn = PROJ_TM, PROJ_TN
    inv_freq = ROPE_THETA ** (-jnp.arange(0, ROT_DIM, 2, dtype=F32) / ROT_DIM)
    lane_d = jnp.arange(LANES) % ATTN_HEAD_DIM
    invf = jnp.where(lane_d < ROT_DIM, inv_freq[lane_d % (ROT_DIM // 2)], 0.0).reshape(1, LANES).astype(F32)
    kern = functools.partial(_qkv_rope_kernel, n_rope_chunks=2 * hd // tn, n_q_chunks=hd // tn, tn=tn)
    return pl.pallas_call(
        kern,
        grid=(T // tm,),
        in_specs=[pl.BlockSpec((tm, 1), lambda i: (i, 0)),
                  pl.BlockSpec((1, LANES), lambda i: (0, 0)),
                  pl.BlockSpec((tm, D), lambda i: (i, 0)),
                  pl.BlockSpec((D, N), lambda i: (0, 0))],
        out_specs=pl.BlockSpec((tm, N), lambda i: (i, 0)),
        out_shape=jax.ShapeDtypeStruct((T, N), BF16),
        compiler_params=_cparams(("parallel",)),
        name="qkv_rope",
    )(pos, invf, x, w_bf16)


def _moba_kernel(q_ref, k_ref, v_ref, o_ref, kmh_ref, kml_ref, vt_ref, q2t_ref, bias_ref,
                 m_ref, acc_ref, *score_bufs, nb, blk):
    s_refs, smax_refs = score_bufs[:4], score_bufs[4:]
    i = pl.program_id(2)
    hd = ATTN_HEAD_DIM
    nbp = kmh_ref.shape[0]
    va = vt_ref.shape[2]

    @pl.when(i == 0)
    def _():
        rows = [jnp.mean(k_ref[j * blk:(j + 1) * blk, :].astype(F32), axis=0, keepdims=True)
                for j in range(nb)]
        if nbp > nb:
            rows.append(jnp.zeros((nbp - nb, LANES), F32))
        km = jnp.concatenate(rows, axis=0)
        hi = km.astype(BF16)
        kmh_ref[...] = hi
        kml_ref[...] = (km - hi.astype(F32)).astype(BF16)
        sub_v = lax.broadcasted_iota(I32, (va - hd, blk), 0)
        ones_row = jnp.where(sub_v == 0, 1.0, 0.0).astype(BF16)
        for j in range(nb):
            vt = v_ref[j * blk:(j + 1) * blk, :].astype(F32).T.astype(BF16)
            vt_ref[j, 0] = jnp.concatenate([vt[:hd], ones_row], axis=0)
            vt_ref[j, 1] = jnp.concatenate([vt[hd:], ones_row], axis=0)

    qt = q_ref[...].astype(F32).T
    sub_q = lax.broadcasted_iota(I32, qt.shape, 0)
    zero = jnp.zeros_like(qt)
    q2t = jnp.concatenate([jnp.where(sub_q < hd, qt, zero), jnp.where(sub_q >= hd, qt, zero)],
                          axis=1).astype(BF16)
    q2t_ref[...] = q2t

    gate = (jnp.dot(kmh_ref[...], q2t, preferred_element_type=F32)
            + jnp.dot(kml_ref[...], q2t, preferred_element_type=F32))
    sub = lax.broadcasted_iota(I32, gate.shape, 0)
    sub_f = sub.astype(F32)
    g = jnp.where(sub < i, gate, NEG_INF)
    sel = jnp.zeros(gate.shape, F32)
    for _ in range(MOBA_TOP_K):
        mx = jnp.max(g, axis=0, keepdims=True)
        idx = jnp.min(jnp.where(g == mx, sub_f, float(nbp)), axis=0, keepdims=True)
        hit = (sub_f == idx) & (mx > NEG_INF)
        sel = jnp.where(hit, 1.0, sel)
        g = jnp.where(hit, NEG_INF, g)
    bias_ref[...] = jnp.where(sel > 0.0, 0.0, MASKED)

    def block_bias(pos, j):
        return jnp.where(pos == 0, 0.0, bias_ref[pl.ds(j, 1), :])

    def produce(pos, buf):
        s_ref, smax_ref = buf
        if isinstance(pos, int) and pos == 0:
            ki = k_ref[pl.ds(pl.multiple_of(i * blk, blk), blk), :]
            s = jnp.dot(ki, q2t, preferred_element_type=F32)
            key = lax.broadcasted_iota(I32, s.shape, 0)
            qry = lax.broadcasted_iota(I32, s.shape, 1) & (blk - 1)
            s = jnp.where(key <= qry, s, MASKED)
            smax = jnp.max(s, axis=0, keepdims=True)
        else:
            j = jnp.minimum(pos - 1, i)
            kj = k_ref[pl.ds(pl.multiple_of(j * blk, blk), blk), :]
            s = jnp.dot(kj, q2t_ref[...], preferred_element_type=F32)
            smax = jnp.max(s, axis=0, keepdims=True) + block_bias(pos, j)
        s_ref[...] = s
        smax_ref[...] = smax

    def consume(pos, buf):
        s_ref, smax_ref = buf
        j = jnp.where(pos == 0, i, jnp.minimum(pos - 1, i))
        m_old = m_ref[...]
        m_new = jnp.maximum(m_old, smax_ref[...])
        a = jnp.exp2(m_old - m_new)
        pb = jnp.exp2(s_ref[...] - (m_new - block_bias(pos, j))).astype(BF16)
        upd = jnp.concatenate([jnp.dot(vt_ref[j, 0], pb[:, :blk], preferred_element_type=F32),
                               jnp.dot(vt_ref[j, 1], pb[:, blk:], preferred_element_type=F32)], axis=1)
        acc_ref[...] = a * acc_ref[...] + upd
        m_ref[...] = m_new

    m_ref[...] = jnp.full(m_ref.shape, MASKED, F32)
    acc_ref[...] = jnp.zeros_like(acc_ref)
    buf_a, buf_b, buf_c, buf_d = [(s_refs[n], smax_refs[n]) for n in range(4)]
    produce(0, buf_a)
    produce(1, buf_b)

    def body(t, carry):
        base = 4 * t
        produce(base + 2, buf_c)
        consume(base, buf_a)
        produce(base + 3, buf_d)
        consume(base + 1, buf_b)
        produce(base + 4, buf_a)
        consume(base + 2, buf_c)
        produce(base + 5, buf_b)
        consume(base + 3, buf_d)
        return carry

    full = (i + 1) // 4
    left = (i + 1) % 4
    lax.fori_loop(0, full, body, 0)
    base = 4 * full

    @pl.when(left == 1)
    def _():
        consume(base, buf_a)

    @pl.when(left == 2)
    def _():
        consume(base, buf_a)
        consume(base + 1, buf_b)

    @pl.when(left == 3)
    def _():
        produce(base + 2, buf_c)
        consume(base, buf_a)
        consume(base + 1, buf_b)
        consume(base + 2, buf_c)

    acc = acc_ref[...]
    out_t = acc[:hd] / acc[hd:hd + 1]
    o_ref[...] = jnp.concatenate([out_t[:, :blk], out_t[:, blk:]], axis=0).T.astype(o_ref.dtype)


def _moba_attention(qkv, B, S, hd):
    T = B * S
    blk = MOBA_BLOCK
    nb = S // blk
    nbp = -(-nb // BF16_SUBLANES) * BF16_SUBLANES
    va = ATTN_HEAD_DIM + BF16_SUBLANES
    n_pairs = hd // LANES
    kern = functools.partial(_moba_kernel, nb=nb, blk=blk)
    return pl.pallas_call(
        kern,
        grid=(B, n_pairs, nb),
        in_specs=[pl.BlockSpec((blk, LANES), lambda b, h, i: (b * nb + i, h)),
                  pl.BlockSpec((S, LANES), lambda b, h, i: (b, n_pairs + h)),
                  pl.BlockSpec((S, LANES), lambda b, h, i: (b, 2 * n_pairs + h))],
        out_specs=pl.BlockSpec((blk, LANES), lambda b, h, i: (b * nb + i, h)),
        out_shape=jax.ShapeDtypeStruct((T, hd), BF16),
        scratch_shapes=[pltpu.VMEM((nbp, LANES), BF16), pltpu.VMEM((nbp, LANES), BF16),
                        pltpu.VMEM((nb, 2, va, blk), BF16),
                        pltpu.VMEM((LANES, 2 * blk), BF16), pltpu.VMEM((nbp, 2 * blk), F32),
                        pltpu.VMEM((1, 2 * blk), F32), pltpu.VMEM((va, 2 * blk), F32)]
                       + [pltpu.VMEM((blk, 2 * blk), F32)] * 4 + [pltpu.VMEM((1, 2 * blk), F32)] * 4,
        compiler_params=_cparams(("parallel", "parallel", "arbitrary")),
        name="moba_attention",
    )(qkv, qkv, qkv)


def _mlstm_proj_kernel(x_ref, w_ref, wg_ref, o_ref, g_ref, *, tn):
    xb = x_ref[...].astype(BF16)
    for c in range(w_ref.shape[1] // tn):
        o_ref[:, c * tn:(c + 1) * tn] = jnp.dot(xb, w_ref[:, c * tn:(c + 1) * tn],
                                                preferred_element_type=F32).astype(o_ref.dtype)
    g_ref[...] = jnp.dot(xb, wg_ref[...], preferred_element_type=F32)


def _mlstm_proj(x, w_main, w_gate):
    T, D = x.shape
    N = w_main.shape[1]
    tm = PROJ_TM
    return pl.pallas_call(
        functools.partial(_mlstm_proj_kernel, tn=PROJ_TN),
        grid=(T // tm,),
        in_specs=[pl.BlockSpec((tm, D), lambda i: (i, 0)),
                  pl.BlockSpec((D, N), lambda i: (0, 0)),
                  pl.BlockSpec((D, LANES), lambda i: (0, 0))],
        out_specs=[pl.BlockSpec((tm, N), lambda i: (i, 0)),
                   pl.BlockSpec((tm, LANES), lambda i: (i, 0))],
        out_shape=[jax.ShapeDtypeStruct((T, N), BF16), jax.ShapeDtypeStruct((T, LANES), F32)],
        compiler_params=_cparams(("parallel",)),
        name="mlstm_proj",
    )(x, w_main, w_gate)


def _log_sigmoid(x):
    return jnp.minimum(x, 0.0) - jnp.log(1.0 + jnp.exp(-jnp.abs(x)))


def _mlstm_kernel(q_ref, k_ref, v_ref, o_ref, gc_ref, gr_ref, bc_ref, br_ref, ng_ref, out_ref,
                  C_ref, n_ref, m_ref, *, L, NH, DQK, DV):
    c = pl.program_id(1)

    @pl.when(c == 0)
    def _():
        C_ref[...] = jnp.zeros_like(C_ref)
        n_ref[...] = jnp.zeros_like(n_ref)
        m_ref[...] = jnp.zeros_like(m_ref)

    gcol = gc_ref[...] + bc_ref[...]
    grow = gr_ref[...] + br_ref[...]
    lane = lax.broadcasted_iota(I32, gcol.shape, 1)
    lf_col = jnp.where((lane >= NH) & (lane < 2 * NH), _log_sigmoid(gcol), 0.0)
    lf_row = _log_sigmoid(grow)
    row = lax.broadcasted_iota(I32, (L, L), 0)
    col = lax.broadcasted_iota(I32, (L, L), 1)
    causal = col <= row
    tri = jnp.where(causal, 1.0, 0.0).astype(BF16)

    def pieces(a):
        out = []
        for _ in range(3):
            p = a.astype(BF16)
            out.append(p)
            a = a - p.astype(F32)
        return out

    b_cols = sum(jnp.dot(tri, p, preferred_element_type=F32) for p in pieces(lf_col))
    b_rows = sum(lax.dot_general(p, tri, NT_DIMS, preferred_element_type=F32) for p in pieces(lf_row))

    for h in range(NH):
        i_row = grow[h:h + 1, :]
        i_col = gcol[:, h:h + 1]
        b_row = b_rows[NH + h:NH + h + 1, :]
        b_col = b_cols[:, NH + h:NH + h + 1]
        m_prev = m_ref[h:h + 1, 0:1]
        qh = q_ref[:, h * DQK:(h + 1) * DQK]
        kh = k_ref[:, h * DQK:(h + 1) * DQK]
        vh = v_ref[:, h * DV:(h + 1) * DV]

        D = jnp.where(causal, b_col - b_row + i_row, NEG_INF)
        g = b_col + m_prev
        m_t = jnp.maximum(g, jnp.max(D, axis=1, keepdims=True))
        w_inter = jnp.exp(g - m_t)
        qk = lax.dot_general(qh, kh, NT_DIMS, preferred_element_type=F32)
        A = jnp.exp(D - m_t) * qk
        num = (w_inter * jnp.dot(qh, C_ref[h].astype(BF16), preferred_element_type=F32)
               + jnp.dot(A.astype(BF16), vh, preferred_element_type=F32))
        qn = jnp.sum(qh.astype(F32) * n_ref[h:h + 1, :], axis=1, keepdims=True)
        den = w_inter * qn + jnp.sum(A, axis=1, keepdims=True)
        hh = num / jnp.maximum(jnp.abs(den), jnp.exp(-m_t))

        m_new = m_t[L - 1:L, :]
        b_last = b_col[L - 1:L, :]
        decay = jnp.exp(b_last + m_prev - m_new)
        w_s = jnp.exp(b_last - b_col + i_col - m_new)
        kw = kh.astype(F32) * w_s
        C_ref[h] = decay * C_ref[h] + lax.dot_general(kw.astype(BF16), vh, TN_DIMS,
                                                      preferred_element_type=F32)
        n_ref[h:h + 1, :] = decay * n_ref[h:h + 1, :] + jnp.sum(kw, axis=0, keepdims=True)
        m_ref[h:h + 1, :] = jnp.broadcast_to(m_new, (1, LANES))

        mu = jnp.mean(hh, axis=1, keepdims=True)
        hc = hh - mu
        var = jnp.mean(hc * hc, axis=1, keepdims=True)
        hn = hc * lax.rsqrt(var + LN_EPS) * ng_ref[:, h * DV:(h + 1) * DV]
        og = jax.nn.sigmoid(o_ref[:, h * DV:(h + 1) * DV].astype(F32))
        out_ref[:, h * DV:(h + 1) * DV] = (og * hn).astype(out_ref.dtype)


def _mlstm_mix(proj, gates, gates_t, b_col, b_row, norm_g, B, S):
    T = B * S
    NH = MLSTM_HEADS
    DV = norm_g.shape[1] // NH
    DQK = DV // 2
    L = MLSTM_L
    nc = S // L
    qw = NH * DQK
    vw = NH * DV
    kern = functools.partial(_mlstm_kernel, L=L, NH=NH, DQK=DQK, DV=DV)
    return pl.pallas_call(
        kern,
        grid=(B, nc),
        in_specs=[pl.BlockSpec((L, qw), lambda b, c: (b * nc + c, 0)),
                  pl.BlockSpec((L, qw), lambda b, c: (b * nc + c, 1)),
                  pl.BlockSpec((L, vw), lambda b, c: (b * nc + c, 2 * qw // vw)),
                  pl.BlockSpec((L, vw), lambda b, c: (b * nc + c, 2 * qw // vw + 1)),
                  pl.BlockSpec((L, LANES), lambda b, c: (b * nc + c, 0)),
                  pl.BlockSpec((SUBLANES, L), lambda b, c: (0, b * nc + c)),
                  pl.BlockSpec((1, LANES), lambda b, c: (0, 0)),
                  pl.BlockSpec((SUBLANES, 1), lambda b, c: (0, 0)),
                  pl.BlockSpec((1, vw), lambda b, c: (0, 0))],
        out_specs=pl.BlockSpec((L, vw), lambda b, c: (b * nc + c, 0)),
        out_shape=jax.ShapeDtypeStruct((T, vw), BF16),
        scratch_shapes=[pltpu.VMEM((NH, DQK, DV), F32), pltpu.VMEM((SUBLANES, DQK), F32),
                        pltpu.VMEM((SUBLANES, LANES), F32)],
        compiler_params=_cparams(("parallel", "arbitrary")),
        name="mlstm_mix",
    )(proj, proj, proj, proj, gates, gates_t, b_col, b_row, norm_g)


def _layer_norm(z, g, b):
    mu = jnp.mean(z, axis=-1, keepdims=True)
    zc = z - mu
    var = jnp.mean(zc * zc, axis=-1, keepdims=True)
    return zc * lax.rsqrt(var + LN_EPS) * g + b


def _mix_out_router_kernel(y_ref, w_ref, x_ref, g_ref, b_ref, rwt_ref, rb_ref,
                           x1_ref, et_ref, gt_ref, pt_ref, cnt_ref, *, tm, E, K):
    y = jnp.dot(y_ref[...], w_ref[...], preferred_element_type=F32)
    x1 = _layer_norm(DEEPNORM_ALPHA * x_ref[...] + y, g_ref[...], b_ref[...])
    x1_ref[...] = x1

    def split(a):
        hi = a.astype(BF16)
        return hi, (a - hi.astype(F32)).astype(BF16)

    x_hi, x_lo = split(x1)
    w_hi, w_lo = split(rwt_ref[...])
    nt = lambda a, b: lax.dot_general(a, b, NT_DIMS, preferred_element_type=F32)
    logits = nt(w_hi, x_hi) + nt(w_lo, x_hi) + nt(w_hi, x_lo) + rb_ref[...]
    sub = lax.broadcasted_iota(I32, logits.shape, 0).astype(F32)
    vals, hits = [], []
    cur = logits
    for _ in range(K):
        mx = jnp.max(cur, axis=0, keepdims=True)
        idx = jnp.min(jnp.where(cur == mx, sub, float(E)), axis=0, keepdims=True)
        hit = sub == idx
        vals.append(mx)
        hits.append(hit)
        cur = jnp.where(hit, NEG_INF, cur)

    chosen = jnp.zeros(logits.shape, F32)
    for hit in hits:
        chosen = jnp.where(hit, 1.0, chosen)
    r_ = lax.broadcasted_iota(I32, (tm, tm), 0)
    c_ = lax.broadcasted_iota(I32, (tm, tm), 1)
    before = (r_ < c_).astype(BF16)
    pos = jnp.dot(chosen.astype(BF16), before, preferred_element_type=F32)
    cnt_ref[...] = jnp.broadcast_to(jnp.sum(chosen, axis=1, keepdims=True), cnt_ref.shape)

    ex = [jnp.exp(v - vals[0]) for v in vals]
    den = ex[0]
    for e_ in ex[1:]:
        den = den + e_
    for r in range(K):
        gt_ref[r:r + 1, :] = ex[r] / den
        et_ref[r:r + 1, :] = jnp.sum(jnp.where(hits[r], sub, 0.0), axis=0, keepdims=True).astype(I32)
        pt_ref[r:r + 1, :] = jnp.sum(jnp.where(hits[r], pos, 0.0), axis=0, keepdims=True).astype(I32)


def _mix_out_router(y, w_bf16, x, ln_g, ln_b, router_w, router_b):
    T, D = x.shape
    Kin = y.shape[1]
    E, K = MOE_EXPERTS, MOE_TOP_K
    tm = MOE_TILE
    kern = functools.partial(_mix_out_router_kernel, tm=tm, E=E, K=K)
    n_tiles = T // tm
    row = lambda i: (i, 0)
    fixed = lambda i: (0, 0)
    tok = lambda i: (0, i)
    return pl.pallas_call(
        kern,
        grid=(n_tiles,),
        in_specs=[pl.BlockSpec((tm, Kin), row), pl.BlockSpec((Kin, D), fixed),
                  pl.BlockSpec((tm, D), row), pl.BlockSpec((1, D), fixed), pl.BlockSpec((1, D), fixed),
                  pl.BlockSpec((E, D), fixed), pl.BlockSpec((E, 1), fixed)],
        out_specs=[pl.BlockSpec((tm, D), row), pl.BlockSpec((K, tm), tok), pl.BlockSpec((K, tm), tok),
                   pl.BlockSpec((K, tm), tok), pl.BlockSpec((E, LANES), tok)],
        out_shape=[jax.ShapeDtypeStruct((T, D), F32), jax.ShapeDtypeStruct((K, T), I32),
                   jax.ShapeDtypeStruct((K, T), F32), jax.ShapeDtypeStruct((K, T), I32),
                   jax.ShapeDtypeStruct((E, n_tiles * LANES), F32)],
        compiler_params=_cparams(("parallel",)),
        name="mix_out_router",
    )(y, w_bf16, x, ln_g.reshape(1, D), ln_b.reshape(1, D), router_w.T, router_b.reshape(E, 1))


def _segment(tables, tile, e, E):
    c8_ref, lo_ref, gd_ref = tables
    idx = tile * E + e
    n = pl.multiple_of(c8_ref[idx], SUBLANES)
    local = pl.multiple_of(lo_ref[idx], SUBLANES)
    glob = pl.multiple_of(gd_ref[idx], SUBLANES)
    return n, local, glob


def _dispatch_kernel(c8_ref, lo_ref, gd_ref, zs_ref, zv_ref, nv_ref, lpos_ref, x_ref, xr_ref,
                     sorted_s, zbuf, sem, zsem, *, tm, K, E, RB, n_blocks, LR, CH):
    i = pl.program_id(0)
    tables = (c8_ref, lo_ref, gd_ref)

    def zero_copy(start):
        return pltpu.make_async_copy(zbuf, xr_ref.at[pl.ds(pl.multiple_of(start, RB), RB), :], zsem)

    @pl.when(i == 0)
    def _():
        zbuf[...] = jnp.zeros_like(zbuf)
        for e in range(E):
            @pl.when(zv_ref[e] > 0)
            def _():
                zero_copy(zs_ref[e]).start()

        def tail_start(p, carry):
            zero_copy(p * RB).start()
            return carry

        def tail_wait(p, carry):
            zero_copy(p * RB).wait()
            return carry

        lax.fori_loop(nv_ref[0], n_blocks, tail_start, 0)
        for e in range(E):
            @pl.when(zv_ref[e] > 0)
            def _():
                zero_copy(zs_ref[e]).wait()
        lax.fori_loop(nv_ref[0], n_blocks, tail_wait, 0)

    slot = i % 2

    def seg_copy(tile, e, s):
        n, local, glob = _segment(tables, tile, e, E)
        return pltpu.make_async_copy(sorted_s.at[s, pl.ds(local, n), :], xr_ref.at[pl.ds(glob, n), :], sem.at[s])

    def for_segments(tile, s, fn):
        for e in range(E):
            @pl.when(c8_ref[tile * E + e] > 0)
            def _():
                fn(seg_copy(tile, e, s))

    @pl.when(i >= 2)
    def _():
        for_segments(i - 2, slot, lambda cp: cp.wait())

    lpos = lpos_ref[...]
    xb = x_ref[...].astype(BF16)
    for c in range(LR // CH):
        rows = lax.broadcasted_iota(I32, (CH, tm), 0) + c * CH
        hit = jnp.zeros((CH, tm), F32)
        for r in range(K):
            hit = hit + jnp.where(rows == lpos[r:r + 1, :], 1.0, 0.0)
        sorted_s[slot, c * CH:(c + 1) * CH, :] = jnp.dot(hit.astype(BF16), xb, preferred_element_type=F32)
    for_segments(i, slot, lambda cp: cp.start())

    @pl.when(i == pl.num_programs(0) - 1)
    def _():
        @pl.when(i >= 1)
        def _():
            for_segments(i - 1, 1 - slot, lambda cp: cp.wait())
        for_segments(i, slot, lambda cp: cp.wait())


def _dispatch(x1, lpos_t, seg_tables, zero_start, zero_valid, n_valid, n_rows):
    T, D = x1.shape
    K, E, RB = MOE_TOP_K, MOE_EXPERTS, ROW_BLOCK
    tm = MOE_TILE
    LR = _local_rows(tm)
    kern = functools.partial(_dispatch_kernel, tm=tm, K=K, E=E, RB=RB, n_blocks=n_rows // RB, LR=LR, CH=SORT_CHUNK)
    grid_spec = pltpu.PrefetchScalarGridSpec(
        num_scalar_prefetch=6,
        grid=(T // tm,),
        in_specs=[pl.BlockSpec((K, tm), lambda i, *_: (0, i)),
                  pl.BlockSpec((tm, D), lambda i, *_: (i, 0))],
        out_specs=pl.BlockSpec(memory_space=pl.ANY),
        scratch_shapes=[pltpu.VMEM((2, LR, D), F32), pltpu.VMEM((RB, D), F32),
                        pltpu.SemaphoreType.DMA((2,)), pltpu.SemaphoreType.DMA],
    )
    return pl.pallas_call(
        kern,
        grid_spec=grid_spec,
        out_shape=jax.ShapeDtypeStruct((n_rows, D), F32),
        compiler_params=pltpu.CompilerParams(dimension_semantics=("arbitrary",),
                                             vmem_limit_bytes=VMEM_LIMIT_BYTES,
                                             has_side_effects=True),
        name="moe_dispatch",
    )(*seg_tables, zero_start, zero_valid, n_valid, lpos_t, x1)


def _expert_kernel(be_ref, nv_ref, nxt_ref, par_ref, x_ref, wgu_hbm, bg_ref, bl_ref, wd_hbm, bd_ref, y_ref,
                   wgu_buf, wd_buf, wg_s, wl_s, wd_s, sem, *, layer):
    p = pl.program_id(0)
    valid = p < nv_ref[0]
    fresh = (p == 0) | (be_ref[p] != be_ref[jnp.maximum(p - 1, 0)])
    half = MXU_DIM // 2

    def weight_copies(e, s):
        return (pltpu.make_async_copy(wgu_hbm.at[layer, e], wgu_buf.at[s], sem.at[0, s]),
                pltpu.make_async_copy(wd_hbm.at[layer, e], wd_buf.at[s], sem.at[1, s]))

    @pl.when(valid & fresh)
    def _():
        s = par_ref[p]

        @pl.when(p == 0)
        def _():
            for cp in weight_copies(be_ref[0], s):
                cp.start()

        @pl.when(nxt_ref[p] >= 0)
        def _():
            for cp in weight_copies(nxt_ref[p], 1 - s):
                cp.start()

        for cp in weight_copies(be_ref[p], s):
            cp.wait()

        r_ = lax.broadcasted_iota(I32, (MXU_DIM, MXU_DIM), 0)
        c_ = lax.broadcasted_iota(I32, (MXU_DIM, MXU_DIM), 1)
        src = jnp.where(c_ < half, 2 * c_, 2 * (c_ - half) + 1)
        perm = jnp.where(r_ == src, 1.0, 0.0).astype(BF16)
        for cb in range(wgu_buf.shape[2] // MXU_DIM):
            blk = wgu_buf[s, :, cb * MXU_DIM:(cb + 1) * MXU_DIM].astype(BF16)
            sep = jnp.dot(blk, perm, preferred_element_type=F32).astype(BF16)
            wg_s[:, cb * half:(cb + 1) * half] = sep[:, :half]
            wl_s[:, cb * half:(cb + 1) * half] = sep[:, half:]
        wd_s[...] = wd_buf[s].astype(BF16)

    @pl.when(valid)
    def _():
        x = x_ref[...].astype(BF16)
        hg = jnp.dot(x, wg_s[...], preferred_element_type=F32) + bg_ref[...]
        hl = jnp.dot(x, wl_s[...], preferred_element_type=F32) + bl_ref[...]
        xg = jnp.minimum(hg, SWIGLU_LIMIT)
        xl = jnp.clip(hl, -SWIGLU_LIMIT, SWIGLU_LIMIT)
        act = xg * jax.nn.sigmoid(SWIGLU_ALPHA * xg) * (xl + 1.0)
        y_ref[...] = jnp.dot(act.astype(BF16), wd_s[...], preferred_element_type=F32) + bd_ref[...]

    @pl.when(jnp.logical_not(valid))
    def _():
        y_ref[...] = jnp.zeros_like(y_ref)


def _experts(x_rows, blk_e, n_valid, blk_next, blk_parity, layer, w_gate_up, b_glu, b_lin, w_down, b_down):
    n_rows, D = x_rows.shape
    _, E, _, F2 = w_gate_up.shape
    F = F2 // 2
    RB = ROW_BLOCK
    n_blocks = n_rows // RB
    xblk = lambda p, be, nv, *_: (jnp.minimum(p, nv[0] - 1), 0)
    yblk = lambda p, *_: (p, 0)
    bsel = lambda p, be, *_: (layer, be[p], 0, 0)
    grid_spec = pltpu.PrefetchScalarGridSpec(
        num_scalar_prefetch=4,
        grid=(n_blocks,),
        in_specs=[pl.BlockSpec((RB, D), xblk),
                  pl.BlockSpec(memory_space=pl.ANY),
                  pl.BlockSpec((None, None, 1, F), bsel), pl.BlockSpec((None, None, 1, F), bsel),
                  pl.BlockSpec(memory_space=pl.ANY), pl.BlockSpec((None, None, 1, D), bsel)],
        out_specs=pl.BlockSpec((RB, D), yblk),
        scratch_shapes=[pltpu.VMEM((2, D, F2), F32), pltpu.VMEM((2, F, D), F32),
                        pltpu.VMEM((D, F), BF16), pltpu.VMEM((D, F), BF16), pltpu.VMEM((F, D), BF16),
                        pltpu.SemaphoreType.DMA((2, 2))],
    )
    return pl.pallas_call(
        functools.partial(_expert_kernel, layer=layer),
        grid_spec=grid_spec,
        out_shape=jax.ShapeDtypeStruct((n_rows, D), F32),
        compiler_params=_cparams(("arbitrary",), EXPERT_VMEM_LIMIT_BYTES),
        name="moe_experts",
    )(blk_e, n_valid, blk_next, blk_parity, x_rows, w_gate_up, b_glu, b_lin, w_down, b_down)


def _combine_kernel(c8_ref, lo_ref, gd_ref, y_ref, lpos_ref, gate_ref, x_ref, g_ref, b_ref, o_ref,
                    ys, sem, *, tm, K, E, LR, CH):
    i = pl.program_id(0)
    n_tiles = pl.num_programs(0)
    slot = i % 2
    tables = (c8_ref, lo_ref, gd_ref)

    def seg_copy(tile, e, s):
        n, local, glob = _segment(tables, tile, e, E)
        return pltpu.make_async_copy(y_ref.at[pl.ds(glob, n), :], ys.at[s, pl.ds(local, n), :], sem.at[s])

    def for_segments(tile, s, fn):
        for e in range(E):
            @pl.when(c8_ref[tile * E + e] > 0)
            def _():
                fn(seg_copy(tile, e, s))

    @pl.when(i == 0)
    def _():
        ys[...] = jnp.zeros_like(ys)
        for_segments(0, 0, lambda cp: cp.start())

    @pl.when(i + 1 < n_tiles)
    def _():
        for_segments(i + 1, 1 - slot, lambda cp: cp.start())

    for_segments(i, slot, lambda cp: cp.wait())

    lpos = lpos_ref[...]
    gate = gate_ref[...]
    f = jnp.zeros((tm, x_ref.shape[1]), F32)
    for c in range(LR // CH):
        cols = lax.broadcasted_iota(I32, (tm, CH), 1) + c * CH
        w = jnp.zeros((tm, CH), F32)
        for r in range(K):
            w = w + jnp.where(cols == lpos[:, r:r + 1], gate[:, r:r + 1], 0.0)
        f = f + jnp.dot(w.astype(BF16), ys[slot, c * CH:(c + 1) * CH, :].astype(BF16),
                        preferred_element_type=F32)
    o_ref[...] = _layer_norm(DEEPNORM_ALPHA * x_ref[...] + f, g_ref[...], b_ref[...])


def _combine(y_rows, lpos_c, seg_tables, x1, gates_c, ln_g, ln_b):
    T, D = x1.shape
    K, E = MOE_TOP_K, MOE_EXPERTS
    tm = MOE_TILE
    LR = _local_rows(tm)
    kern = functools.partial(_combine_kernel, tm=tm, K=K, E=E, LR=LR, CH=SORT_CHUNK)
    grid_spec = pltpu.PrefetchScalarGridSpec(
        num_scalar_prefetch=3,
        grid=(T // tm,),
        in_specs=[pl.BlockSpec(memory_space=pl.ANY),
                  pl.BlockSpec((tm, K), lambda i, *_: (i, 0)),
                  pl.BlockSpec((tm, K), lambda i, *_: (i, 0)),
                  pl.BlockSpec((tm, D), lambda i, *_: (i, 0)),
                  pl.BlockSpec((1, D), lambda i, *_: (0, 0)), pl.BlockSpec((1, D), lambda i, *_: (0, 0))],
        out_specs=pl.BlockSpec((tm, D), lambda i, *_: (i, 0)),
        scratch_shapes=[pltpu.VMEM((2, LR, D), F32), pltpu.SemaphoreType.DMA((2,))],
    )
    return pl.pallas_call(
        kern,
        grid_spec=grid_spec,
        out_shape=jax.ShapeDtypeStruct((T, D), F32),
        compiler_params=_cparams(("arbitrary",)),
        name="moe_combine",
    )(*seg_tables, y_rows, lpos_c, gates_c, x1, ln_g.reshape(1, D), ln_b.reshape(1, D))


def _moe(x1, e_t, g_t, p_t, cnt, layer, w_gate_up, b_glu, b_lin, w_down, b_down, ln_g, ln_b):
    T, D = x1.shape
    E, K, RB = MOE_EXPERTS, MOE_TOP_K, ROW_BLOCK
    tm = MOE_TILE
    n_tiles = T // tm
    n_blocks = -(-(T * K + n_tiles * E * (SUBLANES - 1) + E * (RB - 1)) // RB)
    n_rows = n_blocks * RB

    seg_len = (cnt[:, ::LANES].T.astype(I32) + SUBLANES - 1) // SUBLANES * SUBLANES
    local_off = jnp.cumsum(seg_len, axis=1) - seg_len
    group_rows = jnp.sum(seg_len, axis=0)
    padded = (group_rows + RB - 1) // RB * RB
    pad_end = jnp.cumsum(padded)
    pad_start = pad_end - padded
    global_off = pad_start[None, :] + jnp.cumsum(seg_len, axis=0) - seg_len
    seg_tables = (seg_len.reshape(-1), local_off.reshape(-1).astype(I32), global_off.reshape(-1).astype(I32))

    experts = jnp.arange(E, dtype=I32)
    e_tiles = e_t.reshape(K, n_tiles, tm)
    off_sel = jnp.sum(jnp.where(e_tiles[..., None] == experts, local_off[None, :, None, :], 0), axis=-1)
    lpos_t = (off_sel.reshape(K, T) + p_t).astype(I32)

    n_valid = (pad_end[-1] // RB).astype(I32)
    blk_ids = jnp.minimum(jnp.arange(n_blocks, dtype=I32), n_valid - 1)
    blk_e = jnp.minimum(jnp.sum((pad_end[None, :] <= blk_ids[:, None] * RB).astype(I32), axis=1), E - 1)
    zero_start = jnp.maximum(pad_end - RB, 0).astype(I32)
    nonempty = group_rows > 0
    zero_valid = nonempty.astype(I32)
    n_valid = n_valid.reshape(1)
    later = (experts[None, :] > experts[:, None]) & nonempty[None, :]
    next_e = jnp.min(jnp.where(later, experts[None, :], E), axis=1)
    next_e = jnp.where(next_e < E, next_e, -1).astype(I32)
    parity_e = ((jnp.cumsum(nonempty.astype(I32)) - nonempty.astype(I32)) % 2).astype(I32)
    at_block = blk_e[:, None] == experts[None, :]
    blk_next = jnp.sum(jnp.where(at_block, next_e[None, :], 0), axis=1).astype(I32)
    blk_parity = jnp.sum(jnp.where(at_block, parity_e[None, :], 0), axis=1).astype(I32)

    x_rows = _dispatch(x1, lpos_t, seg_tables, zero_start, zero_valid, n_valid, n_rows)
    y_rows = _experts(x_rows, blk_e, n_valid, blk_next, blk_parity, layer, w_gate_up, b_glu, b_lin, w_down, b_down)
    return _combine(y_rows, lpos_t.T, seg_tables, x1, g_t.T, ln_g, ln_b)


def kernel(x, positions, attn_w_qkv, attn_w_o, mlstm_w_in, mlstm_b_gates, mlstm_norm_g, mlstm_w_out,
           ln_mix_g, ln_mix_b, ln_ffn_g, ln_ffn_b, router_w, router_b,
           w_gate_up, b_gate_up, w_down, b_down):
    B, S, D = x.shape
    T = B * S
    xt = x.reshape(T, D)
    NH = MLSTM_HEADS
    b_glu = b_gate_up[:, :, None, 0::2]
    b_lin = b_gate_up[:, :, None, 1::2]
    b_dn = b_down[:, :, None, :]

    for layer in range(DEPTH):
        slot = layer // 2
        if layer % 2 == 0:
            hd = attn_w_o.shape[1]
            qkv = _qkv_rope(xt, positions.reshape(T, 1), attn_w_qkv[slot].astype(BF16), hd)
            mixed = _moba_attention(qkv, B, S, hd)
            w_out = attn_w_o[slot].astype(BF16)
        else:
            vw = mlstm_w_out.shape[1]
            qw = vw // 2
            w_in = mlstm_w_in[slot]
            dqk = qw // NH
            col_scale = jnp.concatenate([jnp.ones((qw,), F32), jnp.full((qw,), dqk ** -0.5, F32),
                                         jnp.ones((2 * vw,), F32)])
            w_main = (w_in[:, :2 * qw + 2 * vw] * col_scale).astype(BF16)
            w_gate = jnp.pad(w_in[:, 2 * qw + 2 * vw:], ((0, 0), (0, LANES - 2 * NH))).astype(BF16)
            proj, gates = _mlstm_proj(xt, w_main, w_gate)
            bg = mlstm_b_gates[slot]
            b_col = jnp.pad(bg, (0, LANES - 2 * NH)).reshape(1, LANES)
            b_row = bg.reshape(2 * NH, 1)
            mixed = _mlstm_mix(proj, gates, gates[:, :2 * NH].T, b_col, b_row,
                               mlstm_norm_g[slot].reshape(1, vw), B, S)
            w_out = mlstm_w_out[slot].astype(BF16)
        x1, e_t, g_t, p_t, cnt = _mix_out_router(mixed, w_out, xt, ln_mix_g[layer], ln_mix_b[layer],
                                                 router_w[layer], router_b[layer])
        xt = _moe(x1, e_t, g_t, p_t, cnt, layer, w_gate_up, b_glu, b_lin, w_down, b_dn,
                  ln_ffn_g[layer], ln_ffn_b[layer])
    return xt.reshape(B, S, D)
```
